```python
import math
import jax
import jax.numpy as jnp
from jax import lax
import numpy as np

D_MODEL = 2048
BATCH = 32
SEQ = 256
DEPTH = 2
DEC_BATCH = 4
DEC_SEQ = 4096
PAST_LEN = 256

GRID_W = 64
N_HEADS = 16
N_KV_HEADS = 4
KV_REP = N_HEADS // N_KV_HEADS
HEAD_DIM = 128
Q_W = N_HEADS * HEAD_DIM
KV_W = N_KV_HEADS * HEAD_DIM
WINDOW = 128
ATT_BLOCK = 128
ATT_SCALE = HEAD_DIM ** -0.5
ROPE_THETA = 10000.0
SSM_HEADS = 32
SSM_HEAD_DIM = 64
SSM_INNER = SSM_HEADS * SSM_HEAD_DIM
SSM_GROUPS = 4
SSM_STATE = 128
SSM_BC = SSM_GROUPS * SSM_STATE
SSM_CONV_CH = SSM_INNER + 2 * SSM_BC
SSM_CONV = 5
SSM_CHUNK = 128
SC_WIDTH = D_MODEL
SC_CONV = 3
N_BRANCH = 3
D_FF = ((8 * D_MODEL + 3 * 256 - 1) // (3 * 256)) * 256
N_MOD = 6
EPS = 1e-6
IN_SIZES = (Q_W, KV_W, KV_W, SSM_INNER, SSM_INNER, SSM_BC, SSM_BC, 2 * SSM_HEADS, SC_WIDTH, SC_WIDTH, SC_WIDTH, N_BRANCH * D_MODEL)
D_IN = Q_W + 2 * KV_W + 2 * SSM_INNER + 2 * SSM_BC + 2 * SSM_HEADS + 3 * SC_WIDTH + N_BRANCH * D_MODEL

kernel_name = 'hybrid_ssd_shortconv_swa_flow_step'


def split_cols(t, sizes):
    idx = np.cumsum(np.array(sizes))[:-1].tolist()
    return jnp.split(t, idx, axis=-1)


def rmsnorm(x, g):
    xf = x.astype(jnp.float32)
    y = xf * lax.rsqrt(jnp.mean(jnp.square(xf), axis=-1, keepdims=True) + EPS)
    return (y * g.astype(jnp.float32)).astype(x.dtype)


def dwconv(x, w):
    k = w.shape[0]
    return lax.conv_general_dilated(
        x, w[:, None, :].astype(x.dtype), window_strides=(1,),
        padding=[((k - 1) // 2, (k - 1) // 2)],
        dimension_numbers=('NWC', 'WIO', 'NWC'),
        feature_group_count=x.shape[-1])


def axial_rope(x):
    L = x.shape[1]
    rows = L // GRID_W
    row = jnp.repeat(jnp.arange(rows, dtype=jnp.float32), GRID_W)
    col = jnp.tile(jnp.arange(GRID_W, dtype=jnp.float32), rows)
    n_pairs = HEAD_DIM // 4
    inv = ROPE_THETA ** (-jnp.arange(n_pairs, dtype=jnp.float32) / n_pairs)
    ang = jnp.concatenate([row[:, None] * inv, col[:, None] * inv], axis=-1)
    cos = jnp.cos(ang)[None, :, None, :]
    sin = jnp.sin(ang)[None, :, None, :]
    xf = x.astype(jnp.float32).reshape(x.shape[:-1] + (HEAD_DIM // 2, 2))
    x1, x2 = xf[..., 0], xf[..., 1]
    out = jnp.stack([x1 * cos - x2 * sin, x1 * sin + x2 * cos], axis=-1)
    return out.reshape(x.shape).astype(x.dtype)


def sink_softmax(logits, sink):
    s = jnp.broadcast_to(sink.astype(jnp.float32).reshape(1, N_KV_HEADS, KV_REP, 1, 1), logits.shape[:-1] + (1,))
    p = jax.nn.softmax(jnp.concatenate([s, logits], axis=-1), axis=-1)
    return p[..., 1:]


def context_attention(q, k, v, sink):
    b, s = q.shape[:2]
    nb = s // ATT_BLOCK
    qb = jnp.moveaxis(q.reshape(b, nb, ATT_BLOCK, N_KV_HEADS, KV_REP, HEAD_DIM), 1, 0)

    def block(qi):
        logits = jnp.einsum('bqgrd,bkgd->bgrqk', qi, k).astype(jnp.float32) * ATT_SCALE
        p = sink_softmax(logits, sink).astype(v.dtype)
        return jnp.einsum('bgrqk,bkgd->bqgrd', p, v)

    o = lax.map(block, qb)
    return jnp.moveaxis(o, 0, 1).reshape(b, s, Q_W)


def latent_attention(q, k, v, ctx_k, ctx_v, sink):
    b, L = q.shape[:2]
    nb = L // ATT_BLOCK
    band = ATT_BLOCK + 2 * WINDOW
    kp = jnp.pad(k, ((0, 0), (WINDOW, WINDOW), (0, 0), (0, 0)))
    vp = jnp.pad(v, ((0, 0), (WINDOW, WINDOW), (0, 0), (0, 0)))
    qb = jnp.moveaxis(q.reshape(b, nb, ATT_BLOCK, N_KV_HEADS, KV_REP, HEAD_DIM), 1, 0)

    def block(args):
        i, qi = args
        start = i * ATT_BLOCK
        kb = lax.dynamic_slice_in_dim(kp, start, band, axis=1)
        vb = lax.dynamic_slice_in_dim(vp, start, band, axis=1)
        kabs = start - WINDOW + jnp.arange(band)
        qabs = start + jnp.arange(ATT_BLOCK)
        mask = (jnp.abs(kabs[None, :] - qabs[:, None]) <= WINDOW) & (kabs[None, :] >= 0) & (kabs[None, :] < L)
        s_loc = jnp.einsum('bqgrd,bkgd->bgrqk', qi, kb).astype(jnp.float32) * ATT_SCALE
        s_loc = jnp.where(mask, s_loc, -jnp.inf)
        s_ctx = jnp.einsum('bqgrd,bkgd->bgrqk', qi, ctx_k).astype(jnp.float32) * ATT_SCALE
        p = sink_softmax(jnp.concatenate([s_loc, s_ctx], axis=-1), sink).astype(v.dtype)
        return (jnp.einsum('bgrqk,bkgd->bqgrd', p[..., :band], vb)
                + jnp.einsum('bgrqk,bkgd->bqgrd', p[..., band:], ctx_v))

    o = lax.map(block, (jnp.arange(nb), qb))
    return jnp.moveaxis(o, 0, 1).reshape(b, L, Q_W)


def ssd_scan(x, dt, a_coef, bm, cm, h0):
    b, L = x.shape[:2]
    nc = L // SSM_CHUNK
    G, R, P, N, Q = SSM_GROUPS, SSM_HEADS // SSM_GROUPS, SSM_HEAD_DIM, SSM_STATE, SSM_CHUNK
    xdt = (x * dt[..., None]).reshape(b, nc, Q, G, R, P)
    bm = bm.reshape(b, nc, Q, G, N)
    cm = cm.reshape(b, nc, Q, G, N)
    a = jnp.moveaxis((dt * a_coef).reshape(b, nc, Q, G, R), 2, -1)
    acum = jnp.cumsum(a, axis=-1)
    seg = acum[..., :, None] - acum[..., None, :]
    lower = jnp.tril(jnp.ones((Q, Q), dtype=bool))
    decay = jnp.exp(jnp.where(lower, seg, -jnp.inf))
    cb = jnp.einsum('bcign,bcjgn->bcgij', cm, bm)
    y_diag = jnp.einsum('bcgrij,bcjgrp->bcigrp', cb[:, :, :, None] * decay, xdt)
    to_end = jnp.exp(acum[..., -1:] - acum)
    states = jnp.einsum('bcjgn,bcgrj,bcjgrp->bcgrpn', bm, to_end, xdt)
    chunk_decay = jnp.exp(acum[..., -1])

    def carry_step(h, inp):
        st, dec = inp
        return h * dec[..., None, None] + st, h

    h_last, h_in = lax.scan(carry_step, h0.reshape(b, G, R, P, N),
                            (jnp.moveaxis(states, 1, 0), jnp.moveaxis(chunk_decay, 1, 0)))
    h_in = jnp.moveaxis(h_in, 0, 1)
    y_off = jnp.einsum('bcign,bcgrpn,bcgri->bcigrp', cm, h_in, jnp.exp(acum))
    y = (y_diag + y_off).reshape(b, L, SSM_HEADS, P)
    return y, h_last.reshape(b, SSM_HEADS, P, N)


def ssd_mixer(z, xs, bm, cm, dt_raw, lw, h0):
    b, L, _ = xs.shape
    xbc = dwconv(jnp.concatenate([xs, bm, cm], axis=-1), lw['ssm_conv_w'])
    xbc = jax.nn.silu(xbc + lw['ssm_conv_b'].astype(xbc.dtype))
    xs, bm, cm = split_cols(xbc, (SSM_INNER, SSM_BC, SSM_BC))
    x4 = xs.reshape(b, L, SSM_HEADS, SSM_HEAD_DIM).astype(jnp.float32)
    bm = bm.reshape(b, L, SSM_GROUPS, SSM_STATE).astype(jnp.float32)
    cm = cm.reshape(b, L, SSM_GROUPS, SSM_STATE).astype(jnp.float32)
    dt = jax.nn.softplus(dt_raw.reshape(b, L, 2, SSM_HEADS).astype(jnp.float32)
                         + lw['ssm_dt_bias'].astype(jnp.float32))
    a_coef = -jnp.exp(lw['ssm_a_log'].astype(jnp.float32))
    h0 = h0.astype(jnp.float32)
    rev = lambda t: jnp.flip(t, axis=1)
    y_f, h_f = ssd_scan(x4, dt[:, :, 0], a_coef[0], bm, cm, h0[:, 0])
    y_b, h_b = ssd_scan(rev(x4), rev(dt[:, :, 1]), a_coef[1], rev(bm), rev(cm), h0[:, 1])
    y = y_f + rev(y_b) + x4 * lw['ssm_d'].astype(jnp.float32)[:, None]
    y = y.reshape(b, L, SSM_INNER).astype(z.dtype) * jax.nn.silu(z)
    return rmsnorm(y, lw['ssm_norm_g']), jnp.stack([h_f, h_b], axis=1).astype(z.dtype)


def trunk_layer(x, cond, lw, ctx_cache):
    b, L, _ = x.shape
    mod = (jax.nn.silu(cond) @ lw['w_mod'] + lw['b_mod']).reshape(cond.shape[0], 1, N_MOD, D_MODEL)
    shift_m, scale_m, gate_m, shift_f, scale_f, gate_f = (mod[:, :, i] for i in range(N_MOD))
    h = rmsnorm(x, lw['g_pre_mix']) * (1.0 + scale_m) + shift_m
    q, k, v, z, xs, bm, cm, dt_raw, sc_b, sc_c, sc_h, gates = split_cols(h @ lw['w_in'], IN_SIZES)
    q = q.reshape(b, L, N_HEADS, HEAD_DIM)
    k = k.reshape(b, L, N_KV_HEADS, HEAD_DIM)
    v = v.reshape(b, L, N_KV_HEADS, HEAD_DIM)
    if ctx_cache is None:
        y_att = context_attention(q, k, v, lw['sink'])
        h0 = jnp.zeros((b, 2, SSM_HEADS, SSM_HEAD_DIM, SSM_STATE), jnp.float32)
    else:
        ctx_k, ctx_v, h0 = ctx_cache
        y_att = latent_attention(axial_rope(q), axial_rope(k), v, ctx_k, ctx_v, lw['sink'])
    y_ssd, h_last = ssd_mixer(z, xs, bm, cm, dt_raw, lw, h0)
    y_sc = sc_b * dwconv(sc_c * sc_h, lw['sc_conv_w'])
    g = jax.nn.sigmoid(gates.reshape(b, L, N_BRANCH, D_MODEL))
    merged = (g[:, :, 0] * (y_att @ lw['w_att_out'])
              + g[:, :, 1] * (y_ssd @ lw['w_ssd_out'])
              + g[:, :, 2] * (y_sc @ lw['w_sc_out']))
    x = x + gate_m * rmsnorm(merged @ lw['w_o'], lw['g_post_mix'])
    h = rmsnorm(x, lw['g_pre_ffn']) * (1.0 + scale_f) + shift_f
    f_gate, f_up = jnp.split(h @ lw['w_gate_up'], 2, axis=-1)
    x = x + gate_f * rmsnorm((jax.nn.silu(f_gate) * f_up) @ lw['w_down'], lw['g_post_ffn'])
    new_ctx = (k, v, h_last) if ctx_cache is None else None
    return x, new_ctx


def setup_inputs(seed: int = 0) -> dict:
    key = jax.random.key(seed)
    ks = iter(jax.random.split(key, 32))
    f32 = jnp.float32

    def nrm(shape, scale):
        return jax.random.normal(next(ks), shape, f32) * scale

    def gain(shape):
        return 1.0 + nrm(shape, 0.02)

    x_prompt = nrm((BATCH, SEQ, D_MODEL), 1.0)
    x_sample = nrm((DEC_BATCH, DEC_SEQ, D_MODEL), 1.0)
    cache_k = nrm((DEC_BATCH, DEPTH, PAST_LEN, N_KV_HEADS, HEAD_DIM), 1.0)
    cache_v = nrm((DEC_BATCH, DEPTH, PAST_LEN, N_KV_HEADS, HEAD_DIM), 1.0)
    state_ssm = nrm((DEC_BATCH, DEPTH, 2, SSM_HEADS, SSM_HEAD_DIM, SSM_STATE), 0.5)
    c = nrm((DEC_BATCH, D_MODEL), 1.0)
    c_ctx = nrm((D_MODEL,), 1.0)
    w_mod = nrm((DEPTH, D_MODEL, N_MOD * D_MODEL), 0.5 * D_MODEL ** -0.5)
    b_mod = nrm((DEPTH, N_MOD * D_MODEL), 0.02)
    g_pre_mix = gain((DEPTH, D_MODEL))
    w_in = nrm((DEPTH, D_MODEL, D_IN), D_MODEL ** -0.5)
    sink = nrm((DEPTH, N_HEADS), 1.0)
    ssm_conv_w = nrm((DEPTH, SSM_CONV, SSM_CONV_CH), SSM_CONV ** -0.5)
    ssm_conv_b = nrm((DEPTH, SSM_CONV_CH), 0.02)
    dt0 = jnp.exp(jax.random.uniform(next(ks), (DEPTH, 2, SSM_HEADS), f32, math.log(1e-3), math.log(1e-1)))
    ssm_dt_bias = dt0 + jnp.log(-jnp.expm1(-dt0))
    ssm_a_log = jnp.log(jax.random.uniform(next(ks), (DEPTH, 2, SSM_HEADS), f32, 1.0, 16.0))
    ssm_d = gain((DEPTH, SSM_HEADS))
    ssm_norm_g = gain((DEPTH, SSM_INNER))
    sc_conv_w = nrm((DEPTH, SC_CONV, SC_WIDTH), SC_CONV ** -0.5)
    w_att_out = nrm((DEPTH, Q_W, D_MODEL), Q_W ** -0.5)
    w_ssd_out = nrm((DEPTH, SSM_INNER, D_MODEL), SSM_INNER ** -0.5)
    w_sc_out = nrm((DEPTH, SC_WIDTH, D_MODEL), SC_WIDTH ** -0.5)
    w_o = nrm((DEPTH, D_MODEL, D_MODEL), D_MODEL ** -0.5)
    g_post_mix = gain((DEPTH, D_MODEL))
    g_pre_ffn = gain((DEPTH, D_MODEL))
    w_gate_up = nrm((DEPTH, D_MODEL, 2 * D_FF), D_MODEL ** -0.5)
    w_down = nrm((DEPTH, D_FF, D_MODEL), D_FF ** -0.5)
    g_post_ffn = gain((DEPTH, D_MODEL))
    return {'x_prompt': x_prompt, 'x_sample': x_sample, 'cache_k': cache_k, 'cache_v': cache_v,
            'state_ssm': state_ssm, 'c': c, 'c_ctx': c_ctx, 'w_mod': w_mod, 'b_mod': b_mod,
            'g_pre_mix': g_pre_mix, 'w_in': w_in, 'sink': sink, 'ssm_conv_w': ssm_conv_w,
            'ssm_conv_b': ssm_conv_b, 'ssm_dt_bias': ssm_dt_bias, 'ssm_a_log': ssm_a_log,
            'ssm_d': ssm_d, 'ssm_norm_g': ssm_norm_g, 'sc_conv_w': sc_conv_w,
            'w_att_out': w_att_out, 'w_ssd_out': w_ssd_out, 'w_sc_out': w_sc_out, 'w_o': w_o,
            'g_post_mix': g_post_mix, 'g_pre_ffn': g_pre_ffn, 'w_gate_up': w_gate_up,
            'w_down': w_down, 'g_post_ffn': g_post_ffn}


def reference(x_prompt, x_sample, cache_k, cache_v, state_ssm, c, c_ctx, w_mod, b_mod, g_pre_mix,
              w_in, sink, ssm_conv_w, ssm_conv_b, ssm_dt_bias, ssm_a_log, ssm_d, ssm_norm_g,
              sc_conv_w, w_att_out, w_ssd_out, w_sc_out, w_o, g_post_mix, g_pre_ffn, w_gate_up,
              w_down, g_post_ffn):
    y_prompt = x_prompt
    y_sample = x_sample
    ks_out, vs_out, ss_out = [], [], []
    for l in range(DEPTH):
        lw = {'w_mod': w_mod[l], 'b_mod': b_mod[l], 'g_pre_mix': g_pre_mix[l], 'w_in': w_in[l],
              'sink': sink[l], 'ssm_conv_w': ssm_conv_w[l], 'ssm_conv_b': ssm_conv_b[l],
              'ssm_dt_bias': ssm_dt_bias[l], 'ssm_a_log': ssm_a_log[l], 'ssm_d': ssm_d[l],
              'ssm_norm_g': ssm_norm_g[l], 'sc_conv_w': sc_conv_w[l], 'w_att_out': w_att_out[l],
              'w_ssd_out': w_ssd_out[l], 'w_sc_out': w_sc_out[l], 'w_o': w_o[l],
              'g_post_mix': g_post_mix[l], 'g_pre_ffn': g_pre_ffn[l], 'w_gate_up': w_gate_up[l],
              'w_down': w_down[l], 'g_post_ffn': g_post_ffn[l]}
        y_prompt, (k_l, v_l, s_l) = trunk_layer(y_prompt, c_ctx[None, :], lw, None)
        ks_out.append(k_l)
        vs_out.append(v_l)
        ss_out.append(s_l)
        y_sample, _ = trunk_layer(y_sample, c, lw, (cache_k[:, l], cache_v[:, l], state_ssm[:, l]))
    new_cache_k = jnp.stack(ks_out, axis=1)
    new_cache_v = jnp.stack(vs_out, axis=1)
    new_state_ssm = jnp.stack(ss_out, axis=1)
    return (y_prompt, y_sample, new_cache_k, new_cache_v, new_state_ssm)
```

```python
import functools

import jax
import jax.numpy as jnp
from jax import lax
from jax.experimental import pallas as pl
from jax.experimental.pallas import tpu as pltpu

F32 = jnp.float32
BF16 = jnp.bfloat16

D_MODEL = 2048
DEPTH = 2
GRID_W = 64
N_HEADS = 16
N_KV_HEADS = 4
KV_REP = N_HEADS // N_KV_HEADS
HEAD_DIM = 128
Q_W = N_HEADS * HEAD_DIM
KV_W = N_KV_HEADS * HEAD_DIM
WINDOW = 128
ATT_BLOCK = 128
ATT_SCALE = HEAD_DIM ** -0.5
ROPE_THETA = 10000.0
SSM_HEADS = 32
SSM_HEAD_DIM = 64
SSM_INNER = SSM_HEADS * SSM_HEAD_DIM
SSM_GROUPS = 4
SSM_STATE = 128
SSM_BC = SSM_GROUPS * SSM_STATE
SSM_CONV_CH = SSM_INNER + 2 * SSM_BC
SSM_CONV = 5
SSM_CHUNK = 128
SC_WIDTH = D_MODEL
SC_CONV = 3
N_BRANCH = 3
D_FF = ((8 * D_MODEL + 3 * 256 - 1) // (3 * 256)) * 256
N_MOD = 6
EPS = 1e-6

_O_Q = 0
_O_K = _O_Q + Q_W
_O_V = _O_K + KV_W
_O_Z = _O_V + KV_W
_O_XS = _O_Z + SSM_INNER
_O_BM = _O_XS + SSM_INNER
_O_CM = _O_BM + SSM_BC
_O_DT = _O_CM + SSM_BC
_O_SCB = _O_DT + 2 * SSM_HEADS
_O_SCC = _O_SCB + SC_WIDTH
_O_SCH = _O_SCC + SC_WIDTH
_O_G = _O_SCH + SC_WIDTH

P_Q = 0
P_Z = 2048
P_SCB = 4096
P_SCC = 6144
P_SCH = 8192
P_G = 10240
P_XS = 16384
P_BM = 18432
P_CM = 18944
P_K = 19456
P_V = 19968
P_N = 20480

LANES = 128
HALO = 16
VMEM_LIMIT = 56 * 1024 * 1024


def _cparams(sem):
    return pltpu.CompilerParams(dimension_semantics=sem, vmem_limit_bytes=VMEM_LIMIT)


def _rms(x, g):
    return x * lax.rsqrt(jnp.mean(x * x, axis=-1, keepdims=True) + EPS) * g


def _silu(x):
    return x * jax.nn.sigmoid(x)


def _mod_kernel(c_ref, w_ref, b_ref, o_ref):
    a = _silu(c_ref[...]).astype(BF16)
    o_ref[0] = jnp.dot(a, w_ref[0].astype(BF16), preferred_element_type=F32) + b_ref[0]


def modulation(cond, w_mod, b_mod):
    rows = cond.shape[0]
    n = N_MOD * D_MODEL
    tn = 1024
    return pl.pallas_call(
        _mod_kernel,
        grid=(DEPTH, n // tn),
        in_specs=[pl.BlockSpec((rows, D_MODEL), lambda l, j: (0, 0)),
                  pl.BlockSpec((1, D_MODEL, tn), lambda l, j: (l, 0, j)),
                  pl.BlockSpec((1, 1, tn), lambda l, j: (l, 0, j))],
        out_specs=pl.BlockSpec((1, rows, tn), lambda l, j: (l, 0, j)),
        out_shape=jax.ShapeDtypeStruct((DEPTH, rows, n), F32),
        compiler_params=_cparams(("parallel", "parallel")),
        name="modulation",
    )(cond, w_mod, b_mod.reshape(DEPTH, 1, n))


def _norm_mod(x_ref, g_ref, mod_ref, shift_i, scale_i):
    y = _rms(x_ref[...], g_ref[...])
    return y * (1.0 + mod_ref[0, scale_i:scale_i + 1, :]) + mod_ref[0, shift_i:shift_i + 1, :]


def _nm_matmul_kernel(x_ref, g_ref, mod_ref, w_ref, o_ref, h_ref, *, shift_i, scale_i):
    @pl.when(pl.program_id(1) == 0)
    def _():
        h_ref[...] = _norm_mod(x_ref, g_ref, mod_ref, shift_i, scale_i).astype(BF16)

    o_ref[...] = jnp.dot(h_ref[...], w_ref[...], preferred_element_type=F32).astype(o_ref.dtype)


def norm_matmul(x, g, mod, mod_row, w, out_dtype, tm, tn, shift_i, scale_i):
    m = x.shape[0]
    n = w.shape[1]
    return pl.pallas_call(
        functools.partial(_nm_matmul_kernel, shift_i=shift_i, scale_i=scale_i),
        grid=(m // tm, n // tn),
        in_specs=[pl.BlockSpec((tm, D_MODEL), lambda i, j: (i, 0)),
                  pl.BlockSpec((1, D_MODEL), lambda i, j: (0, 0)),
                  pl.BlockSpec((1, N_MOD, D_MODEL), lambda i, j: (mod_row(i), 0, 0)),
                  pl.BlockSpec((D_MODEL, tn), lambda i, j: (0, j))],
        out_specs=pl.BlockSpec((tm, tn), lambda i, j: (i, j)),
        out_shape=jax.ShapeDtypeStruct((m, n), out_dtype),
        scratch_shapes=[pltpu.VMEM((tm, D_MODEL), BF16)],
        compiler_params=_cparams(("parallel", "arbitrary")),
        name="norm_matmul",
    )(x, g.reshape(1, D_MODEL), mod, w)


def _nm_swiglu_kernel(x_ref, g_ref, mod_ref, wg_ref, wu_ref, o_ref, h_ref):
    @pl.when(pl.program_id(1) == 0)
    def _():
        h_ref[...] = _norm_mod(x_ref, g_ref, mod_ref, 3, 4).astype(BF16)

    h = h_ref[...]
    a = jnp.dot(h, wg_ref[...], preferred_element_type=F32)
    b = jnp.dot(h, wu_ref[...], preferred_element_type=F32)
    o_ref[...] = (_silu(a) * b).astype(o_ref.dtype)


def norm_swiglu(x, g, mod, mod_row, w_gate_up, tm, tn):
    m = x.shape[0]
    nb = D_FF // tn
    return pl.pallas_call(
        _nm_swiglu_kernel,
        grid=(m // tm, nb),
        in_specs=[pl.BlockSpec((tm, D_MODEL), lambda i, j: (i, 0)),
                  pl.BlockSpec((1, D_MODEL), lambda i, j: (0, 0)),
                  pl.BlockSpec((1, N_MOD, D_MODEL), lambda i, j: (mod_row(i), 0, 0)),
                  pl.BlockSpec((D_MODEL, tn), lambda i, j: (0, j)),
                  pl.BlockSpec((D_MODEL, tn), lambda i, j: (0, j + nb))],
        out_specs=pl.BlockSpec((tm, tn), lambda i, j: (i, j)),
        out_shape=jax.ShapeDtypeStruct((m, D_FF), BF16),
        scratch_shapes=[pltpu.VMEM((tm, D_MODEL), BF16)],
        compiler_params=_cparams(("parallel", "arbitrary")),
        name="norm_swiglu",
    )(x, g.reshape(1, D_MODEL), mod, w_gate_up, w_gate_up)


def _matmul_norm_res_kernel(a_ref, w_ref, x_ref, g_ref, mod_ref, o_ref, acc_ref, *, gate_i):
    k = pl.program_id(1)

    @pl.when(k == 0)
    def _():
        acc_ref[...] = jnp.zeros_like(acc_ref)

    acc_ref[...] += jnp.dot(a_ref[...], w_ref[...], preferred_element_type=F32)

    @pl.when(k == pl.num_programs(1) - 1)
    def _():
        y = _rms(acc_ref[...], g_ref[...])
        o_ref[...] = x_ref[...] + mod_ref[0, gate_i:gate_i + 1, :] * y


def matmul_norm_residual(a, w, x, g, mod, mod_row, gate_i, tm, tk):
    m, kdim = a.shape
    return pl.pallas_call(
        functools.partial(_matmul_norm_res_kernel, gate_i=gate_i),
        grid=(m // tm, kdim // tk),
        in_specs=[pl.BlockSpec((tm, tk), lambda i, k: (i, k)),
                  pl.BlockSpec((tk, D_MODEL), lambda i, k: (k, 0)),
                  pl.BlockSpec((tm, D_MODEL), lambda i, k: (i, 0)),
                  pl.BlockSpec((1, D_MODEL), lambda i, k: (0, 0)),
                  pl.BlockSpec((1, N_MOD, D_MODEL), lambda i, k: (mod_row(i), 0, 0))],
        out_specs=pl.BlockSpec((tm, D_MODEL), lambda i, k: (i, 0)),
        out_shape=jax.ShapeDtypeStruct((m, D_MODEL), F32),
        scratch_shapes=[pltpu.VMEM((tm, D_MODEL), F32)],
        compiler_params=_cparams(("parallel", "arbitrary")),
        name="matmul_norm_residual",
    )(a, w, x, g.reshape(1, D_MODEL), mod)


def _merge_kernel(ya_ref, ys_ref, yc_ref, wa_ref, ws_ref, wc_ref, ga_ref, gs_ref, gc_ref, o_ref):
    def branch(y_ref, w_ref, gate_ref):
        return jax.nn.sigmoid(gate_ref[...].astype(F32)) * jnp.dot(
            y_ref[...], w_ref[...], preferred_element_type=F32)

    o_ref[...] = (branch(ya_ref, wa_ref, ga_ref) + branch(ys_ref, ws_ref, gs_ref)
                  + branch(yc_ref, wc_ref, gc_ref)).astype(o_ref.dtype)


def merge_branches(y_att, y_ssd, y_sc, w_att, w_ssd, w_sc, proj, tm, tn):
    m = y_att.shape[0]
    gb = P_G // tn
    nb = D_MODEL // tn
    y_spec = pl.BlockSpec((tm, D_MODEL), lambda i, j: (i, 0))
    w_spec = pl.BlockSpec((D_MODEL, tn), lambda i, j: (0, j))

    def g_spec(b):
        return pl.BlockSpec((tm, tn), lambda i, j: (i, gb + b * nb + j))

    return pl.pallas_call(
        _merge_kernel,
        grid=(m // tm, nb),
        in_specs=[y_spec, y_spec, y_spec, w_spec, w_spec, w_spec, g_spec(0), g_spec(1), g_spec(2)],
        out_specs=pl.BlockSpec((tm, tn), lambda i, j: (i, j)),
        out_shape=jax.ShapeDtypeStruct((m, D_MODEL), BF16),
        compiler_params=_cparams(("parallel", "parallel")),
        name="merge_branches",
    )(y_att, y_ssd, y_sc, w_att, w_ssd, w_sc, proj, proj, proj)


def _softmax_pv(s, sink, v):
    m = jnp.maximum(jnp.max(s, axis=-1, keepdims=True), sink)
    p = jnp.exp(s - m)
    den = jnp.sum(p, axis=-1, keepdims=True) + jnp.exp(sink - m)
    return jnp.dot(p.astype(BF16), v, preferred_element_type=F32) / den


def _ctx_attn_kernel(sink_ref, q_ref, k_ref, v_ref, o_ref):
    for g in range(N_KV_HEADS):
        k_g = k_ref[:, g * HEAD_DIM:(g + 1) * HEAD_DIM]
        v_g = v_ref[:, g * HEAD_DIM:(g + 1) * HEAD_DIM]
        for r in range(KV_REP):
            h = g * KV_REP + r
            q_h = q_ref[:, h * HEAD_DIM:(h + 1) * HEAD_DIM]
            s = lax.dot_general(q_h, k_g, (((1,), (1,)), ((), ())),
                                preferred_element_type=F32) * ATT_SCALE
            o = _softmax_pv(s, sink_ref[h], v_g)
            o_ref[:, h * HEAD_DIM:(h + 1) * HEAD_DIM] = o.astype(o_ref.dtype)


def context_attention(proj, sink, batch, seq):
    m = batch * seq
    return pl.pallas_call(
        _ctx_attn_kernel,
        grid=(batch,),
        in_specs=[pl.BlockSpec(memory_space=pltpu.SMEM),
                  pl.BlockSpec((seq, Q_W), lambda b: (b, P_Q // Q_W)),
                  pl.BlockSpec((seq, KV_W), lambda b: (b, P_K // KV_W)),
                  pl.BlockSpec((seq, KV_W), lambda b: (b, P_V // KV_W))],
        out_specs=pl.BlockSpec((seq, Q_W), lambda b: (b, 0)),
        out_shape=jax.ShapeDtypeStruct((m, Q_W), BF16),
        compiler_params=_cparams(("parallel",)),
        name="context_attention",
    )(sink, proj, proj, proj)


def _rope_tile(x, cos, sin_signed, even_lane):
    nxt = pltpu.roll(x, LANES - 1, axis=1)
    prv = pltpu.roll(x, 1, axis=1)
    return x * cos + jnp.where(even_lane, nxt, prv) * sin_signed


def _lat_attn_kernel(sink_ref, q_ref, k_ref, v_ref, ck_ref, cv_ref, cosq_ref, sinq_ref,
                     cosk_ref, sink_tab_ref, o_ref, krot_ref, *, seq, past):
    i = pl.program_id(1)
    band = ATT_BLOCK + 2 * WINDOW
    rows = 256

    @pl.when(i == 0)
    def _():
        def body(c, carry):
            r0 = pl.multiple_of(c * rows, rows)
            cos = cosk_ref[pl.ds(r0, rows), :]
            sin = sink_tab_ref[pl.ds(r0, rows), :]
            even = lax.broadcasted_iota(jnp.int32, (rows, LANES), 1) % 2 == 0
            for g in range(N_KV_HEADS):
                x = k_ref[pl.ds(r0, rows), g * HEAD_DIM:(g + 1) * HEAD_DIM].astype(F32)
                krot_ref[pl.ds(r0, rows), g * HEAD_DIM:(g + 1) * HEAD_DIM] = _rope_tile(
                    x, cos, sin, even).astype(BF16)
            return carry

        lax.fori_loop(0, seq // rows, body, 0)

    start = pl.multiple_of(jnp.clip(i * ATT_BLOCK - WINDOW, 0, seq - band), ATT_BLOCK)
    kb = krot_ref[pl.ds(start, band), :]
    vb = v_ref[pl.ds(start, band), :]
    ck = ck_ref[0]
    cv = cv_ref[0]
    qpos = i * ATT_BLOCK + lax.broadcasted_iota(jnp.int32, (ATT_BLOCK, band + past), 0)
    col = lax.broadcasted_iota(jnp.int32, (ATT_BLOCK, band + past), 1)
    keep = (col >= band) | (jnp.abs(start + col - qpos) <= WINDOW)
    cos_q = cosq_ref[...]
    sin_q = sinq_ref[...]
    even_q = lax.broadcasted_iota(jnp.int32, (ATT_BLOCK, LANES), 1) % 2 == 0
    for g in range(N_KV_HEADS):
        sl = slice(g * HEAD_DIM, (g + 1) * HEAD_DIM)
        kcat = jnp.concatenate([kb[:, sl], ck[:, sl]], axis=0)
        vcat = jnp.concatenate([vb[:, sl], cv[:, sl]], axis=0)
        for r in range(KV_REP):
            h = g * KV_REP + r
            hs = slice(h * HEAD_DIM, (h + 1) * HEAD_DIM)
            q_h = _rope_tile(q_ref[:, hs].astype(F32), cos_q, sin_q, even_q).astype(BF16)
            s = lax.dot_general(q_h, kcat, (((1,), (1,)), ((), ())),
                                preferred_element_type=F32) * ATT_SCALE
            s = jnp.where(keep, s, -jnp.inf)
            o_ref[:, hs] = _softmax_pv(s, sink_ref[h], vcat).astype(o_ref.dtype)


def latent_attention(proj, ctx_k, ctx_v, sink, cos_t, sin_t, batch, seq):
    m = batch * seq
    nb = seq // ATT_BLOCK
    past = ctx_k.shape[1]
    return pl.pallas_call(
        functools.partial(_lat_attn_kernel, seq=seq, past=past),
        grid=(batch, nb),
        in_specs=[pl.BlockSpec(memory_space=pltpu.SMEM),
                  pl.BlockSpec((ATT_BLOCK, Q_W), lambda b, i: (b * nb + i, P_Q // Q_W)),
                  pl.BlockSpec((seq, KV_W), lambda b, i: (b, P_K // KV_W)),
                  pl.BlockSpec((seq, KV_W), lambda b, i: (b, P_V // KV_W)),
                  pl.BlockSpec((1, past, KV_W), lambda b, i: (b, 0, 0)),
                  pl.BlockSpec((1, past, KV_W), lambda b, i: (b, 0, 0)),
                  pl.BlockSpec((ATT_BLOCK, LANES), lambda b, i: (i, 0)),
                  pl.BlockSpec((ATT_BLOCK, LANES), lambda b, i: (i, 0)),
                  pl.BlockSpec((seq, LANES), lambda b, i: (0, 0)),
                  pl.BlockSpec((seq, LANES), lambda b, i: (0, 0))],
        out_specs=pl.BlockSpec((ATT_BLOCK, Q_W), lambda b, i: (b * nb + i, 0)),
        out_shape=jax.ShapeDtypeStruct((m, Q_W), BF16),
        scratch_shapes=[pltpu.VMEM((seq, KV_W), BF16)],
        compiler_params=_cparams(("parallel", "arbitrary")),
        name="latent_attention",
    )(sink, proj, proj, proj, ctx_k, ctx_v, cos_t, sin_t, cos_t, sin_t)


def rope_tables(seq):
    rows = seq // GRID_W
    row = jnp.repeat(jnp.arange(rows, dtype=F32), GRID_W)
    col = jnp.tile(jnp.arange(GRID_W, dtype=F32), rows)
    n_pairs = HEAD_DIM // 4
    inv = ROPE_THETA ** (-jnp.arange(n_pairs, dtype=F32) / n_pairs)
    ang = jnp.concatenate([row[:, None] * inv, col[:, None] * inv], axis=-1)
    ang = jnp.repeat(ang, 2, axis=-1)
    sign = jnp.where(jnp.arange(HEAD_DIM) % 2 == 0, -1.0, 1.0).astype(F32)
    return jnp.cos(ang), jnp.sin(ang) * sign


def _fill_halo_buffer(buf_ref, prev, main, nxt, t, nt, tile):
    buf_ref[0:HALO, :] = jnp.where(t == 0, 0.0, prev)
    buf_ref[HALO:HALO + tile, :] = main
    buf_ref[HALO + tile:HALO + tile + HALO, :] = jnp.where(t == nt - 1, 0.0, nxt)


def _conv_taps(buf_ref, w_ref, taps, tile):
    half = (taps - 1) // 2
    acc = None
    for k in range(taps):
        term = w_ref[k:k + 1, :] * buf_ref[pl.ds(HALO - half + k, tile), :]
        acc = term if acc is None else acc + term
    return acc


def _ssm_conv_kernel(prev_ref, main_ref, next_ref, w_ref, b_ref, o_ref, buf_ref, *, nt, tile):
    t = pl.program_id(0) % nt
    _fill_halo_buffer(buf_ref, prev_ref[...].astype(F32), main_ref[...].astype(F32),
                      next_ref[...].astype(F32), t, nt, tile)
    y = _conv_taps(buf_ref, w_ref, SSM_CONV, tile) + b_ref[...]
    o_ref[...] = _silu(y).astype(o_ref.dtype)


def _halo_specs(tile, width, col0, m):
    hb = tile // HALO
    last = m // HALO - 1
    prev = pl.BlockSpec((HALO, width), lambda i, j: (jnp.maximum(i * hb - 1, 0), col0 + j))
    main = pl.BlockSpec((tile, width), lambda i, j: (i, col0 + j))
    nxt = pl.BlockSpec((HALO, width), lambda i, j: (jnp.minimum((i + 1) * hb, last), col0 + j))
    return prev, main, nxt


def ssm_conv(proj, w, b, seq, tile):
    m = proj.shape[0]
    width = 1024
    nt = seq // tile
    prev, main, nxt = _halo_specs(tile, width, P_XS // width, m)
    return pl.pallas_call(
        functools.partial(_ssm_conv_kernel, nt=nt, tile=tile),
        grid=(m // tile, SSM_CONV_CH // width),
        in_specs=[prev, main, nxt,
                  pl.BlockSpec((SSM_CONV, width), lambda i, j: (0, j)),
                  pl.BlockSpec((1, width), lambda i, j: (0, j))],
        out_specs=pl.BlockSpec((tile, width), lambda i, j: (i, j)),
        out_shape=jax.ShapeDtypeStruct((m, SSM_CONV_CH), BF16),
        scratch_shapes=[pltpu.VMEM((tile + 2 * HALO, width), F32)],
        compiler_params=_cparams(("parallel", "parallel")),
        name="ssm_conv",
    )(proj, proj, proj, w, b.reshape(1, SSM_CONV_CH))


def _short_conv_kernel(b_ref, cp_ref, cm_ref, cn_ref, hp_ref, hm_ref, hn_ref, w_ref, o_ref, buf_ref,
                       *, nt, tile):
    t = pl.program_id(0) % nt

    def prod(c_ref, h_ref):
        return c_ref[...].astype(F32) * h_ref[...].astype(F32)

    _fill_halo_buffer(buf_ref, prod(cp_ref, hp_ref), prod(cm_ref, hm_ref), prod(cn_ref, hn_ref),
                      t, nt, tile)
    y = _conv_taps(buf_ref, w_ref, SC_CONV, tile)
    o_ref[...] = (b_ref[...].astype(F32) * y).astype(o_ref.dtype)


def short_conv(proj, w, seq, tile):
    m = proj.shape[0]
    width = 1024
    nt = seq // tile
    cp, cm, cn = _halo_specs(tile, width, P_SCC // width, m)
    hp, hm, hn = _halo_specs(tile, width, P_SCH // width, m)
    return pl.pallas_call(
        functools.partial(_short_conv_kernel, nt=nt, tile=tile),
        grid=(m // tile, SC_WIDTH // width),
        in_specs=[pl.BlockSpec((tile, width), lambda i, j: (i, P_SCB // width + j)),
                  cp, cm, cn, hp, hm, hn,
                  pl.BlockSpec((SC_CONV, width), lambda i, j: (0, j))],
        out_specs=pl.BlockSpec((tile, width), lambda i, j: (i, j)),
        out_shape=jax.ShapeDtypeStruct((m, SC_WIDTH), BF16),
        scratch_shapes=[pltpu.VMEM((tile + 2 * HALO, width), F32)],
        compiler_params=_cparams(("parallel", "parallel")),
        name="short_conv",
    )(proj, proj, proj, proj, proj, proj, proj, w)


def _split3(a):
    a1 = a.astype(BF16)
    r1 = a - a1.astype(F32)
    a2 = r1.astype(BF16)
    a3 = (r1 - a2.astype(F32)).astype(BF16)
    return a1, a2, a3


def _softplus(x):
    return jnp.maximum(x, 0.0) + jnp.log1p(jnp.exp(-jnp.abs(x)))


def _ssd_kernel(x_ref, b_ref, c_ref, dt_ref, dtb_ref, alog_ref, h0_ref, y_ref, st_ref, *, backward):
    q = SSM_CHUNK

    @pl.when(pl.program_id(1) == 0)
    def _():
        st_ref[...] = h0_ref[...]

    lo = SSM_HEADS if backward else 0
    dt = _softplus(dt_ref[...] + dtb_ref[...])
    a = dt * (-jnp.exp(alog_ref[...]))
    ii = lax.broadcasted_iota(jnp.int32, (q, q), 0)
    jj = lax.broadcasted_iota(jnp.int32, (q, q), 1)
    tri = (jj >= ii) if backward else (jj <= ii)
    tri_b = jnp.where(tri, 1.0, 0.0).astype(BF16)
    u = None
    for piece in _split3(a):
        d = jnp.dot(tri_b, piece, preferred_element_type=F32)
        u = d if u is None else u + d
    u_t = u.T
    edge = 0 if backward else q - 1
    t_row = u[edge:edge + 1, :]
    lane_lo = lax.broadcasted_iota(jnp.int32, (1, LANES), 1) < SSM_HEAD_DIM
    heads_per_group = SSM_HEADS // SSM_GROUPS
    pairs_per_group = heads_per_group // 2
    gw = heads_per_group * SSM_HEAD_DIM

    for g in range(SSM_GROUPS):
        b_g = b_ref[:, g * SSM_STATE:(g + 1) * SSM_STATE]
        c_g = c_ref[:, g * SSM_STATE:(g + 1) * SSM_STATE]
        cb = lax.dot_general(c_g, b_g, (((1,), (1,)), ((), ())), preferred_element_type=F32)
        h_t = st_ref[0, g]
        y_off = jnp.dot(c_g, h_t.astype(BF16), preferred_element_type=F32)
        xs_parts = []
        dec_parts = []
        for pr in range(pairs_per_group):
            h_a = g * heads_per_group + 2 * pr
            ca = lo + h_a
            cbk = ca + 1
            pc = (h_a // 2) * LANES
            xp = x_ref[:, pc:pc + LANES].astype(F32)
            dtp = jnp.where(lane_lo, dt[:, ca:ca + 1], dt[:, cbk:cbk + 1])
            up = jnp.where(lane_lo, u[:, ca:ca + 1], u[:, cbk:cbk + 1])
            tp = jnp.where(lane_lo, t_row[:, ca:ca + 1], t_row[:, cbk:cbk + 1])
            xdt = xp * dtp
            xdt_b = xdt.astype(BF16)
            ys = []
            for col in (ca, cbk):
                seg = u[:, col:col + 1] - u_t[col:col + 1, :]
                decay = jnp.exp(jnp.where(tri, seg, -jnp.inf))
                ys.append(jnp.dot((cb * decay).astype(BF16), xdt_b, preferred_element_type=F32))
            y_pair = jnp.where(lane_lo, ys[0], ys[1]) + jnp.exp(up) * y_off[:, pr * LANES:(pr + 1) * LANES]
            y_ref[:, pc:pc + LANES] = y_pair
            xs_parts.append((xdt * jnp.exp(tp - up)).astype(BF16))
            dec_parts.append(jnp.exp(tp))
        xs_g = jnp.concatenate(xs_parts, axis=1)
        dec_g = jnp.concatenate(dec_parts, axis=1)
        upd = lax.dot_general(b_g, xs_g, (((0,), (0,)), ((), ())), preferred_element_type=F32)
        st_ref[0, g] = dec_g * h_t + upd


def ssd_scan(xbc, dt_src, dt_block, dt_bias, a_log, h0, batch, seq, backward):
    m = batch * seq
    nc = seq // SSM_CHUNK
    gw = SSM_INNER // SSM_GROUPS

    def row(b, c):
        return b * nc + (nc - 1 - c if backward else c)

    return pl.pallas_call(
        functools.partial(_ssd_kernel, backward=backward),
        grid=(batch, nc),
        in_specs=[pl.BlockSpec((SSM_CHUNK, SSM_INNER), lambda b, c: (row(b, c), 0)),
                  pl.BlockSpec((SSM_CHUNK, SSM_BC), lambda b, c: (row(b, c), SSM_INNER // SSM_BC)),
                  pl.BlockSpec((SSM_CHUNK, SSM_BC), lambda b, c: (row(b, c), SSM_INNER // SSM_BC + 1)),
                  pl.BlockSpec((SSM_CHUNK, LANES), lambda b, c: (row(b, c), dt_block)),
                  pl.BlockSpec((1, LANES), lambda b, c: (0, 0)),
                  pl.BlockSpec((1, LANES), lambda b, c: (0, 0)),
                  pl.BlockSpec((1, SSM_GROUPS, SSM_STATE, gw), lambda b, c: (b, 0, 0, 0))],
        out_specs=[pl.BlockSpec((SSM_CHUNK, SSM_INNER), lambda b, c: (row(b, c), 0)),
                   pl.BlockSpec((1, SSM_GROUPS, SSM_STATE, gw), lambda b, c: (b, 0, 0, 0))],
        out_shape=[jax.ShapeDtypeStruct((m, SSM_INNER), F32),
                   jax.ShapeDtypeStruct((batch, SSM_GROUPS, SSM_STATE, gw), F32)],
        compiler_params=_cparams(("parallel", "arbitrary")),
        name="ssd_scan_bwd" if backward else "ssd_scan_fwd",
    )(xbc, xbc, xbc, dt_src, dt_bias, a_log, h0)


def _ssd_finish_kernel(yf_ref, yb_ref, x_ref, z_ref, d_ref, g_ref, o_ref):
    y = yf_ref[...] + yb_ref[...] + x_ref[...].astype(F32) * d_ref[...]
    y = y * _silu(z_ref[...].astype(F32))
    o_ref[...] = _rms(y, g_ref[...]).astype(o_ref.dtype)


def ssd_finish(y_f, y_b, xbc, proj, d_exp, g, tm):
    m = y_f.shape[0]
    row = pl.BlockSpec((tm, SSM_INNER), lambda i: (i, 0))
    vec = pl.BlockSpec((1, SSM_INNER), lambda i: (0, 0))
    return pl.pallas_call(
        _ssd_finish_kernel,
        grid=(m // tm,),
        in_specs=[row, row, row, pl.BlockSpec((tm, SSM_INNER), lambda i: (i, P_Z // SSM_INNER)), vec, vec],
        out_specs=row,
        out_shape=jax.ShapeDtypeStruct((m, SSM_INNER), BF16),
        compiler_params=_cparams(("parallel",)),
        name="ssd_finish",
    )(y_f, y_b, xbc, proj, d_exp, g.reshape(1, SSM_INNER))


def _state_to_kernel_layout(h):
    b = h.shape[0]
    r = SSM_HEADS // SSM_GROUPS
    h = h.reshape(b, SSM_GROUPS, r, SSM_HEAD_DIM, SSM_STATE)
    return jnp.transpose(h, (0, 1, 4, 2, 3)).reshape(b, SSM_GROUPS, SSM_STATE, r * SSM_HEAD_DIM)


def _state_from_kernel_layout(h):
    b = h.shape[0]
    r = SSM_HEADS // SSM_GROUPS
    h = h.reshape(b, SSM_GROUPS, SSM_STATE, r, SSM_HEAD_DIM)
    return jnp.transpose(h, (0, 1, 3, 4, 2)).reshape(b, SSM_HEADS, SSM_HEAD_DIM, SSM_STATE)


def _pad_lanes(v):
    flat = v.reshape(1, 2 * SSM_HEADS).astype(F32)
    return jnp.pad(flat, ((0, 0), (0, LANES - 2 * SSM_HEADS)))


def _layer_weights(l, w_in, w_att_out, w_ssd_out, w_sc_out, w_o, w_gate_up, w_down):
    w = w_in[l]

    def cols(o, n):
        return w[:, o:o + n]

    w_main = jnp.concatenate(
        [cols(_O_Q, Q_W), cols(_O_Z, SSM_INNER), cols(_O_SCB, SC_WIDTH), cols(_O_SCC, SC_WIDTH),
         cols(_O_SCH, SC_WIDTH), cols(_O_G, N_BRANCH * D_MODEL), cols(_O_XS, SSM_INNER),
         cols(_O_BM, SSM_BC), cols(_O_CM, SSM_BC), cols(_O_K, KV_W), cols(_O_V, KV_W)],
        axis=1).astype(BF16)
    dt_w = jnp.pad(cols(_O_DT, 2 * SSM_HEADS), ((0, 0), (0, LANES - 2 * SSM_HEADS)))
    w_side_ctx = jnp.concatenate([cols(_O_K, KV_W), cols(_O_V, KV_W), dt_w], axis=1).astype(BF16)
    return dict(w_main=w_main, w_side_ctx=w_side_ctx, w_side_lat=dt_w.astype(BF16),
                w_att=w_att_out[l].astype(BF16), w_ssd=w_ssd_out[l].astype(BF16),
                w_sc=w_sc_out[l].astype(BF16), w_o=w_o[l].astype(BF16),
                w_gate_up=w_gate_up[l].astype(BF16), w_down=w_down[l].astype(BF16))


def _trunk_layer(x, mod, mod_row, lw, p, batch, seq, ctx_cache, rope):
    m = batch * seq
    tm = 512
    proj = norm_matmul(x, p['g_pre_mix'], mod, mod_row, lw['w_main'], BF16, tm, 1024, 0, 1)
    if ctx_cache is None:
        side = norm_matmul(x, p['g_pre_mix'], mod, mod_row, lw['w_side_ctx'], F32, tm,
                           lw['w_side_ctx'].shape[1], 0, 1)
        dt_block = 2 * KV_W // LANES
        y_att = context_attention(proj, p['sink'], batch, seq)
        h0 = jnp.zeros((2, batch, SSM_GROUPS, SSM_STATE, SSM_INNER // SSM_GROUPS), F32)
    else:
        side = norm_matmul(x, p['g_pre_mix'], mod, mod_row, lw['w_side_lat'], F32, tm, LANES, 0, 1)
        dt_block = 0
        ctx_k, ctx_v, state = ctx_cache
        y_att = latent_attention(proj, ctx_k.reshape(batch, -1, KV_W).astype(BF16),
                                 ctx_v.reshape(batch, -1, KV_W).astype(BF16), p['sink'],
                                 rope[0], rope[1], batch, seq)
        h0 = jnp.stack([_state_to_kernel_layout(state[:, 0]), _state_to_kernel_layout(state[:, 1])])
    conv_tile = min(seq, 512)
    xbc = ssm_conv(proj, p['ssm_conv_w'], p['ssm_conv_b'], seq, conv_tile)
    dt_bias = _pad_lanes(p['ssm_dt_bias'])
    a_log = _pad_lanes(p['ssm_a_log'])
    y_f, h_f = ssd_scan(xbc, side, dt_block, dt_bias, a_log, h0[0], batch, seq, False)
    y_b, h_b = ssd_scan(xbc, side, dt_block, dt_bias, a_log, h0[1], batch, seq, True)
    d_exp = jnp.repeat(p['ssm_d'].astype(F32), SSM_HEAD_DIM).reshape(1, SSM_INNER)
    y_ssd = ssd_finish(y_f, y_b, xbc, proj, d_exp, p['ssm_norm_g'], tm)
    y_sc = short_conv(proj, p['sc_conv_w'], seq, conv_tile)
    merged = merge_branches(y_att, y_ssd, y_sc, lw['w_att'], lw['w_ssd'], lw['w_sc'], proj, tm, 512)
    x = matmul_norm_residual(merged, lw['w_o'], x, p['g_post_mix'], mod, mod_row, 2, tm, D_MODEL)
    act = norm_swiglu(x, p['g_pre_ffn'], mod, mod_row, lw['w_gate_up'], tm, 512)
    x = matmul_norm_residual(act, lw['w_down'], x, p['g_post_ffn'], mod, mod_row, 5, tm, 512)
    new_ctx = None
    if ctx_cache is None:
        k = side[:, :KV_W].reshape(batch, seq, N_KV_HEADS, HEAD_DIM)
        v = side[:, KV_W:2 * KV_W].reshape(batch, seq, N_KV_HEADS, HEAD_DIM)
        h_last = jnp.stack([_state_from_kernel_layout(h_f), _state_from_kernel_layout(h_b)], axis=1)
        new_ctx = (k, v, h_last)
    return x, new_ctx


def kernel(x_prompt, x_sample, cache_k, cache_v, state_ssm, c, c_ctx, w_mod, b_mod, g_pre_mix, w_in, sink, ssm_conv_w, ssm_conv_b, ssm_dt_bias, ssm_a_log, ssm_d, ssm_norm_g, sc_conv_w, w_att_out, w_ssd_out, w_sc_out, w_o, g_post_mix, g_pre_ffn, w_gate_up, w_down, g_post_ffn):
    batch, seq, _ = x_prompt.shape
    dec_batch, dec_seq, _ = x_sample.shape
    cond = jnp.concatenate([c_ctx[None, :], c, jnp.zeros((8 - 1 - dec_batch, D_MODEL), F32)], axis=0)
    mod = modulation(cond, w_mod, b_mod).reshape(DEPTH, 8, N_MOD, D_MODEL)
    rope = rope_tables(dec_seq)
    tm = 512
    y_prompt = x_prompt.reshape(batch * seq, D_MODEL)
    y_sample = x_sample.reshape(dec_batch * dec_seq, D_MODEL)
    ks_out, vs_out, ss_out = [], [], []
    for l in range(DEPTH):
        lw = _layer_weights(l, w_in, w_att_out, w_ssd_out, w_sc_out, w_o, w_gate_up, w_down)
        p = {'g_pre_mix': g_pre_mix[l], 'sink': sink[l], 'ssm_conv_w': ssm_conv_w[l],
             'ssm_conv_b': ssm_conv_b[l], 'ssm_dt_bias': ssm_dt_bias[l], 'ssm_a_log': ssm_a_log[l],
             'ssm_d': ssm_d[l], 'ssm_norm_g': ssm_norm_g[l], 'sc_conv_w': sc_conv_w[l],
             'g_post_mix': g_post_mix[l], 'g_pre_ffn': g_pre_ffn[l], 'g_post_ffn': g_post_ffn[l]}
        y_prompt, (k_l, v_l, s_l) = _trunk_layer(
            y_prompt, mod[l], lambda i: 0, lw, p, batch, seq, None, None)
        ks_out.append(k_l)
        vs_out.append(v_l)
        ss_out.append(s_l)
        y_sample, _ = _trunk_layer(
            y_sample, mod[l], lambda i: 1 + (i * tm) // dec_seq, lw, p, dec_batch, dec_seq,
            (cache_k[:, l], cache_v[:, l], state_ssm[:, l]), rope)
    return (y_prompt.reshape(batch, seq, D_MODEL), y_sample.reshape(dec_batch, dec_seq, D_MODEL),
            jnp.stack(ks_out, axis=1), jnp.stack(vs_out, axis=1), jnp.stack(ss_out, axis=1))
```

```python
import functools

import jax
import jax.numpy as jnp
from jax import lax
from jax.experimental import pallas as pl
from jax.experimental.pallas import tpu as pltpu

F32 = jnp.float32
BF16 = jnp.bfloat16

D_MODEL = 2048
DEPTH = 2
GRID_W = 64
N_HEADS = 16
N_KV_HEADS = 4
KV_REP = N_HEADS // N_KV_HEADS
HEAD_DIM = 128
Q_W = N_HEADS * HEAD_DIM
KV_W = N_KV_HEADS * HEAD_DIM
WINDOW = 128
ATT_BLOCK = 128
ATT_SCALE = HEAD_DIM ** -0.5
ROPE_THETA = 10000.0
SSM_HEADS = 32
SSM_HEAD_DIM = 64
SSM_INNER = SSM_HEADS * SSM_HEAD_DIM
SSM_GROUPS = 4
SSM_STATE = 128
SSM_BC = SSM_GROUPS * SSM_STATE
SSM_CONV_CH = SSM_INNER + 2 * SSM_BC
SSM_CONV = 5
SSM_CHUNK = 128
SC_WIDTH = D_MODEL
SC_CONV = 3
N_BRANCH = 3
D_FF = ((8 * D_MODEL + 3 * 256 - 1) // (3 * 256)) * 256
N_MOD = 6
EPS = 1e-6
LOG2E = 1.4426950408889634

_O_Q = 0
_O_K = _O_Q + Q_W
_O_V = _O_K + KV_W
_O_Z = _O_V + KV_W
_O_XS = _O_Z + SSM_INNER
_O_BM = _O_XS + SSM_INNER
_O_CM = _O_BM + SSM_BC
_O_DT = _O_CM + SSM_BC
_O_SCB = _O_DT + 2 * SSM_HEADS
_O_SCC = _O_SCB + SC_WIDTH
_O_SCH = _O_SCC + SC_WIDTH
_O_G = _O_SCH + SC_WIDTH

P_Q = 0
P_Z = 2048
P_SCB = 4096
P_SCC = 6144
P_SCH = 8192
P_G = 10240
P_XS = 16384
P_BM = 18432
P_CM = 18944
P_K = 19456
P_V = 19968
P_N = 20480

LANES = 128
HALO = 16
VMEM_LIMIT = 56 * 1024 * 1024


def _cparams(sem):
    return pltpu.CompilerParams(dimension_semantics=sem, vmem_limit_bytes=VMEM_LIMIT)


def _rms(x, g):
    return x * lax.rsqrt(jnp.mean(x * x, axis=-1, keepdims=True) + EPS) * g


def _silu(x):
    return x * jax.nn.sigmoid(x)


def _mod_kernel(c_ref, w_ref, b_ref, o_ref):
    a = _silu(c_ref[...]).astype(BF16)
    o_ref[0] = jnp.dot(a, w_ref[0].astype(BF16), preferred_element_type=F32) + b_ref[0]


def modulation(cond, w_mod, b_mod):
    rows = cond.shape[0]
    n = N_MOD * D_MODEL
    tn = 1024
    return pl.pallas_call(
        _mod_kernel,
        grid=(DEPTH, n // tn),
        in_specs=[pl.BlockSpec((rows, D_MODEL), lambda l, j: (0, 0)),
                  pl.BlockSpec((1, D_MODEL, tn), lambda l, j: (l, 0, j)),
                  pl.BlockSpec((1, 1, tn), lambda l, j: (l, 0, j))],
        out_specs=pl.BlockSpec((1, rows, tn), lambda l, j: (l, 0, j)),
        out_shape=jax.ShapeDtypeStruct((DEPTH, rows, n), F32),
        compiler_params=_cparams(("parallel", "parallel")),
        name="modulation",
    )(cond, w_mod, b_mod.reshape(DEPTH, 1, n))


def _norm_mod(x_ref, g_ref, mod_ref, shift_i, scale_i):
    y = _rms(x_ref[...], g_ref[...])
    return y * (1.0 + mod_ref[0, scale_i:scale_i + 1, :]) + mod_ref[0, shift_i:shift_i + 1, :]


def _in_proj_kernel(x_ref, g_ref, mod_ref, w_ref, ws_ref, o_ref, os_ref, h_ref):
    @pl.when(pl.program_id(1) == 0)
    def _():
        h = _norm_mod(x_ref, g_ref, mod_ref, 0, 1).astype(BF16)
        h_ref[...] = h
        os_ref[...] = jnp.dot(h, ws_ref[...], preferred_element_type=F32)

    o_ref[...] = jnp.dot(h_ref[...], w_ref[...], preferred_element_type=F32).astype(o_ref.dtype)


def in_proj(x, g, mod, mod_row, w, w_side, tm, tn):
    m = x.shape[0]
    n = w.shape[1]
    ns = w_side.shape[1]
    return pl.pallas_call(
        _in_proj_kernel,
        grid=(m // tm, n // tn),
        in_specs=[pl.BlockSpec((tm, D_MODEL), lambda i, j: (i, 0)),
                  pl.BlockSpec((1, D_MODEL), lambda i, j: (0, 0)),
                  pl.BlockSpec((1, N_MOD, D_MODEL), lambda i, j: (mod_row(i * tm), 0, 0)),
                  pl.BlockSpec((D_MODEL, tn), lambda i, j: (0, j)),
                  pl.BlockSpec((D_MODEL, ns), lambda i, j: (0, 0))],
        out_specs=[pl.BlockSpec((tm, tn), lambda i, j: (i, j)),
                   pl.BlockSpec((tm, ns), lambda i, j: (i, 0))],
        out_shape=[jax.ShapeDtypeStruct((m, n), BF16), jax.ShapeDtypeStruct((m, ns), F32)],
        scratch_shapes=[pltpu.VMEM((tm, D_MODEL), BF16)],
        compiler_params=_cparams(("parallel", "arbitrary")),
        name="in_proj",
    )(x, g.reshape(1, D_MODEL), mod, w, w_side)


def _nm_swiglu_kernel(x_ref, g_ref, mod_ref, wg_ref, wu_ref, o_ref, h_ref):
    @pl.when(pl.program_id(1) == 0)
    def _():
        h_ref[...] = _norm_mod(x_ref, g_ref, mod_ref, 3, 4).astype(BF16)

    h = h_ref[...]
    a = jnp.dot(h, wg_ref[...], preferred_element_type=F32)
    b = jnp.dot(h, wu_ref[...], preferred_element_type=F32)
    o_ref[...] = (_silu(a) * b).astype(o_ref.dtype)


def norm_swiglu(x, g, mod, mod_row, w_gate_up, tm, tn):
    m = x.shape[0]
    nb = D_FF // tn
    return pl.pallas_call(
        _nm_swiglu_kernel,
        grid=(m // tm, nb),
        in_specs=[pl.BlockSpec((tm, D_MODEL), lambda i, j: (i, 0)),
                  pl.BlockSpec((1, D_MODEL), lambda i, j: (0, 0)),
                  pl.BlockSpec((1, N_MOD, D_MODEL), lambda i, j: (mod_row(i * tm), 0, 0)),
                  pl.BlockSpec((D_MODEL, tn), lambda i, j: (0, j)),
                  pl.BlockSpec((D_MODEL, tn), lambda i, j: (0, j + nb))],
        out_specs=pl.BlockSpec((tm, tn), lambda i, j: (i, j)),
        out_shape=jax.ShapeDtypeStruct((m, D_FF), BF16),
        scratch_shapes=[pltpu.VMEM((tm, D_MODEL), BF16)],
        compiler_params=_cparams(("parallel", "arbitrary")),
        name="norm_swiglu",
    )(x, g.reshape(1, D_MODEL), mod, w_gate_up, w_gate_up)


def _matmul_norm_res_kernel(a_ref, w_ref, x_ref, g_ref, mod_ref, o_ref, acc_ref, *, gate_i):
    k = pl.program_id(1)

    @pl.when(k == 0)
    def _():
        acc_ref[...] = jnp.zeros_like(acc_ref)

    acc_ref[...] += jnp.dot(a_ref[...], w_ref[...], preferred_element_type=F32)

    @pl.when(k == pl.num_programs(1) - 1)
    def _():
        y = _rms(acc_ref[...], g_ref[...])
        o_ref[...] = x_ref[...] + mod_ref[0, gate_i:gate_i + 1, :] * y


def matmul_norm_residual(a, w, x, g, mod, mod_row, gate_i, tm, tk):
    m, kdim = a.shape
    return pl.pallas_call(
        functools.partial(_matmul_norm_res_kernel, gate_i=gate_i),
        grid=(m // tm, kdim // tk),
        in_specs=[pl.BlockSpec((tm, tk), lambda i, k: (i, k)),
                  pl.BlockSpec((tk, D_MODEL), lambda i, k: (k, 0)),
                  pl.BlockSpec((tm, D_MODEL), lambda i, k: (i, 0)),
                  pl.BlockSpec((1, D_MODEL), lambda i, k: (0, 0)),
                  pl.BlockSpec((1, N_MOD, D_MODEL), lambda i, k: (mod_row(i * tm), 0, 0))],
        out_specs=pl.BlockSpec((tm, D_MODEL), lambda i, k: (i, 0)),
        out_shape=jax.ShapeDtypeStruct((m, D_MODEL), F32),
        scratch_shapes=[pltpu.VMEM((tm, D_MODEL), F32)],
        compiler_params=_cparams(("parallel", "arbitrary")),
        name="matmul_norm_residual",
    )(a, w, x, g.reshape(1, D_MODEL), mod)


def _merge_kernel(ya_ref, ys_ref, yc_ref, wa_ref, ws_ref, wc_ref, ga_ref, gs_ref, gc_ref, o_ref):
    def branch(y_ref, w_ref, gate_ref):
        return jax.nn.sigmoid(gate_ref[...].astype(F32)) * jnp.dot(
            y_ref[...], w_ref[...], preferred_element_type=F32)

    o_ref[...] = (branch(ya_ref, wa_ref, ga_ref) + branch(ys_ref, ws_ref, gs_ref)
                  + branch(yc_ref, wc_ref, gc_ref)).astype(o_ref.dtype)


def merge_branches(y_att, y_ssd, y_sc, w_att, w_ssd, w_sc, proj, tm, tn):
    m = y_att.shape[0]
    gb = P_G // tn
    nb = D_MODEL // tn
    y_spec = pl.BlockSpec((tm, D_MODEL), lambda i, j: (i, 0))
    w_spec = pl.BlockSpec((D_MODEL, tn), lambda i, j: (0, j))

    def g_spec(b):
        return pl.BlockSpec((tm, tn), lambda i, j: (i, gb + b * nb + j))

    return pl.pallas_call(
        _merge_kernel,
        grid=(m // tm, nb),
        in_specs=[y_spec, y_spec, y_spec, w_spec, w_spec, w_spec, g_spec(0), g_spec(1), g_spec(2)],
        out_specs=pl.BlockSpec((tm, tn), lambda i, j: (i, j)),
        out_shape=jax.ShapeDtypeStruct((m, D_MODEL), BF16),
        compiler_params=_cparams(("parallel", "parallel")),
        name="merge_branches",
    )(y_att, y_ssd, y_sc, w_att, w_ssd, w_sc, proj, proj, proj)


def _softmax_pv(s, sink, v):
    m = jnp.maximum(jnp.max(s, axis=-1, keepdims=True), sink)
    p = jnp.exp(s - m)
    den = jnp.sum(p, axis=-1, keepdims=True) + jnp.exp(sink - m)
    return jnp.dot(p.astype(BF16), v, preferred_element_type=F32) / den


def _ctx_attn_kernel(sink_ref, q_ref, k_ref, v_ref, o_ref):
    for g in range(N_KV_HEADS):
        k_g = k_ref[:, g * HEAD_DIM:(g + 1) * HEAD_DIM]
        v_g = v_ref[:, g * HEAD_DIM:(g + 1) * HEAD_DIM]
        for r in range(KV_REP):
            h = g * KV_REP + r
            q_h = q_ref[:, h * HEAD_DIM:(h + 1) * HEAD_DIM]
            s = lax.dot_general(q_h, k_g, (((1,), (1,)), ((), ())),
                                preferred_element_type=F32) * ATT_SCALE
            o = _softmax_pv(s, sink_ref[h], v_g)
            o_ref[:, h * HEAD_DIM:(h + 1) * HEAD_DIM] = o.astype(o_ref.dtype)


def context_attention(proj, sink, batch, seq):
    m = batch * seq
    return pl.pallas_call(
        _ctx_attn_kernel,
        grid=(batch,),
        in_specs=[pl.BlockSpec(memory_space=pltpu.SMEM),
                  pl.BlockSpec((seq, Q_W), lambda b: (b, P_Q // Q_W)),
                  pl.BlockSpec((seq, KV_W), lambda b: (b, P_K // KV_W)),
                  pl.BlockSpec((seq, KV_W), lambda b: (b, P_V // KV_W))],
        out_specs=pl.BlockSpec((seq, Q_W), lambda b: (b, 0)),
        out_shape=jax.ShapeDtypeStruct((m, Q_W), BF16),
        compiler_params=_cparams(("parallel",)),
        name="context_attention",
    )(sink, proj, proj, proj)


def _rope_tile(x, cos, sin_signed, even_lane):
    nxt = pltpu.roll(x, LANES - 1, axis=1)
    prv = pltpu.roll(x, 1, axis=1)
    return x * cos + jnp.where(even_lane, nxt, prv) * sin_signed


def _lat_attn_kernel(sink_ref, q_ref, k_ref, v_ref, ck_ref, cv_ref, cosq_ref, sinq_ref,
                     cosk_ref, sink_tab_ref, o_ref, krot_ref, *, seq, past):
    i = pl.program_id(1)
    band = ATT_BLOCK + 2 * WINDOW
    rows = 256

    @pl.when(i == 0)
    def _():
        def body(c, carry):
            r0 = pl.multiple_of(c * rows, rows)
            cos = cosk_ref[pl.ds(r0, rows), :]
            sin = sink_tab_ref[pl.ds(r0, rows), :]
            even = lax.broadcasted_iota(jnp.int32, (rows, LANES), 1) % 2 == 0
            for g in range(N_KV_HEADS):
                x = k_ref[pl.ds(r0, rows), g * HEAD_DIM:(g + 1) * HEAD_DIM].astype(F32)
                krot_ref[pl.ds(r0, rows), g * HEAD_DIM:(g + 1) * HEAD_DIM] = _rope_tile(
                    x, cos, sin, even).astype(BF16)
            return carry

        lax.fori_loop(0, seq // rows, body, 0)

    start = pl.multiple_of(jnp.clip(i * ATT_BLOCK - WINDOW, 0, seq - band), ATT_BLOCK)
    kb = krot_ref[pl.ds(start, band), :]
    vb = v_ref[pl.ds(start, band), :]
    ck = ck_ref[0]
    cv = cv_ref[0]
    stacked = KV_REP * ATT_BLOCK
    row = lax.broadcasted_iota(jnp.int32, (stacked, band), 0)
    col = lax.broadcasted_iota(jnp.int32, (stacked, band), 1)
    keep = jnp.abs(start + col - (i * ATT_BLOCK + row % ATT_BLOCK)) <= WINDOW
    head_of_row = lax.broadcasted_iota(jnp.int32, (stacked, 1), 0) // ATT_BLOCK
    cos_q = cosq_ref[...] * (ATT_SCALE * LOG2E)
    sin_q = sinq_ref[...] * (ATT_SCALE * LOG2E)
    even_q = lax.broadcasted_iota(jnp.int32, (ATT_BLOCK, LANES), 1) % 2 == 0
    for g in range(N_KV_HEADS):
        sl = slice(g * HEAD_DIM, (g + 1) * HEAD_DIM)
        kcat = jnp.concatenate([kb[:, sl], ck[:, sl]], axis=0)
        vcat = jnp.concatenate([vb[:, sl], cv[:, sl]], axis=0)
        q_parts = []
        sink2 = jnp.zeros((stacked, 1), F32)
        for r in range(KV_REP):
            h = g * KV_REP + r
            hs = slice(h * HEAD_DIM, (h + 1) * HEAD_DIM)
            q_parts.append(_rope_tile(q_ref[:, hs].astype(F32), cos_q, sin_q, even_q).astype(BF16))
            sink2 = jnp.where(head_of_row == r, sink_ref[h] * LOG2E, sink2)
        q_g = jnp.concatenate(q_parts, axis=0)
        s = lax.dot_general(q_g, kcat, (((1,), (1,)), ((), ())), preferred_element_type=F32)
        s_loc = jnp.where(keep, s[:, :band], -jnp.inf)
        s_ctx = s[:, band:]
        m = jnp.maximum(jnp.maximum(jnp.max(s_loc, axis=-1, keepdims=True),
                                    jnp.max(s_ctx, axis=-1, keepdims=True)), sink2)
        p_loc = jnp.exp2(s_loc - m)
        p_ctx = jnp.exp2(s_ctx - m)
        den = (jnp.sum(p_loc, axis=-1, keepdims=True) + jnp.sum(p_ctx, axis=-1, keepdims=True)
               + jnp.exp2(sink2 - m))
        p = jnp.concatenate([p_loc.astype(BF16), p_ctx.astype(BF16)], axis=1)
        o = jnp.dot(p, vcat, preferred_element_type=F32) / den
        for r in range(KV_REP):
            h = g * KV_REP + r
            o_ref[:, h * HEAD_DIM:(h + 1) * HEAD_DIM] = o[r * ATT_BLOCK:(r + 1) * ATT_BLOCK].astype(
                o_ref.dtype)


def latent_attention(proj, ctx_k, ctx_v, sink, cos_t, sin_t, batch, seq):
    m = batch * seq
    nb = seq // ATT_BLOCK
    past = ctx_k.shape[1]
    return pl.pallas_call(
        functools.partial(_lat_attn_kernel, seq=seq, past=past),
        grid=(batch, nb),
        in_specs=[pl.BlockSpec(memory_space=pltpu.SMEM),
                  pl.BlockSpec((ATT_BLOCK, Q_W), lambda b, i: (b * nb + i, P_Q // Q_W)),
                  pl.BlockSpec((seq, KV_W), lambda b, i: (b, P_K // KV_W)),
                  pl.BlockSpec((seq, KV_W), lambda b, i: (b, P_V // KV_W)),
                  pl.BlockSpec((1, past, KV_W), lambda b, i: (b, 0, 0)),
                  pl.BlockSpec((1, past, KV_W), lambda b, i: (b, 0, 0)),
                  pl.BlockSpec((ATT_BLOCK, LANES), lambda b, i: (i, 0)),
                  pl.BlockSpec((ATT_BLOCK, LANES), lambda b, i: (i, 0)),
                  pl.BlockSpec((seq, LANES), lambda b, i: (0, 0)),
                  pl.BlockSpec((seq, LANES), lambda b, i: (0, 0))],
        out_specs=pl.BlockSpec((ATT_BLOCK, Q_W), lambda b, i: (b * nb + i, 0)),
        out_shape=jax.ShapeDtypeStruct((m, Q_W), BF16),
        scratch_shapes=[pltpu.VMEM((seq, KV_W), BF16)],
        compiler_params=_cparams(("parallel", "arbitrary")),
        name="latent_attention",
    )(sink, proj, proj, proj, ctx_k, ctx_v, cos_t, sin_t, cos_t, sin_t)


def rope_tables(seq):
    rows = seq // GRID_W
    row = jnp.repeat(jnp.arange(rows, dtype=F32), GRID_W)
    col = jnp.tile(jnp.arange(GRID_W, dtype=F32), rows)
    n_pairs = HEAD_DIM // 4
    inv = ROPE_THETA ** (-jnp.arange(n_pairs, dtype=F32) / n_pairs)
    ang = jnp.concatenate([row[:, None] * inv, col[:, None] * inv], axis=-1)
    ang = jnp.repeat(ang, 2, axis=-1)
    sign = jnp.where(jnp.arange(HEAD_DIM) % 2 == 0, -1.0, 1.0).astype(F32)
    return jnp.cos(ang), jnp.sin(ang) * sign


def _fill_halo_buffer(buf_ref, prev, main, nxt, t, nt, tile):
    buf_ref[0:HALO, :] = jnp.where(t == 0, 0.0, prev)
    buf_ref[HALO:HALO + tile, :] = main
    buf_ref[HALO + tile:HALO + tile + HALO, :] = jnp.where(t == nt - 1, 0.0, nxt)


def _conv_taps(buf_ref, w_ref, taps, tile):
    half = (taps - 1) // 2
    acc = None
    for k in range(taps):
        term = w_ref[k:k + 1, :] * buf_ref[pl.ds(HALO - half + k, tile), :]
        acc = term if acc is None else acc + term
    return acc


def _ssm_conv_kernel(prev_ref, main_ref, next_ref, w_ref, b_ref, o_ref, buf_ref, *, nt, tile):
    t = pl.program_id(0) % nt
    _fill_halo_buffer(buf_ref, prev_ref[...].astype(F32), main_ref[...].astype(F32),
                      next_ref[...].astype(F32), t, nt, tile)
    y = _conv_taps(buf_ref, w_ref, SSM_CONV, tile) + b_ref[...]
    o_ref[...] = _silu(y).astype(o_ref.dtype)


def _halo_specs(tile, width, col0, m):
    hb = tile // HALO
    last = m // HALO - 1
    prev = pl.BlockSpec((HALO, width), lambda i, j: (jnp.maximum(i * hb - 1, 0), col0 + j))
    main = pl.BlockSpec((tile, width), lambda i, j: (i, col0 + j))
    nxt = pl.BlockSpec((HALO, width), lambda i, j: (jnp.minimum((i + 1) * hb, last), col0 + j))
    return prev, main, nxt


def ssm_conv(proj, w, b, seq, tile):
    m = proj.shape[0]
    width = 1024
    nt = seq // tile
    prev, main, nxt = _halo_specs(tile, width, P_XS // width, m)
    return pl.pallas_call(
        functools.partial(_ssm_conv_kernel, nt=nt, tile=tile),
        grid=(m // tile, SSM_CONV_CH // width),
        in_specs=[prev, main, nxt,
                  pl.BlockSpec((SSM_CONV, width), lambda i, j: (0, j)),
                  pl.BlockSpec((1, width), lambda i, j: (0, j))],
        out_specs=pl.BlockSpec((tile, width), lambda i, j: (i, j)),
        out_shape=jax.ShapeDtypeStruct((m, SSM_CONV_CH), BF16),
        scratch_shapes=[pltpu.VMEM((tile + 2 * HALO, width), F32)],
        compiler_params=_cparams(("parallel", "parallel")),
        name="ssm_conv",
    )(proj, proj, proj, w, b.reshape(1, SSM_CONV_CH))


def _short_conv_kernel(b_ref, cp_ref, cm_ref, cn_ref, hp_ref, hm_ref, hn_ref, w_ref, o_ref, buf_ref,
                       *, nt, tile):
    t = pl.program_id(0) % nt

    def prod(c_ref, h_ref):
        return c_ref[...].astype(F32) * h_ref[...].astype(F32)

    _fill_halo_buffer(buf_ref, prod(cp_ref, hp_ref), prod(cm_ref, hm_ref), prod(cn_ref, hn_ref),
                      t, nt, tile)
    y = _conv_taps(buf_ref, w_ref, SC_CONV, tile)
    o_ref[...] = (b_ref[...].astype(F32) * y).astype(o_ref.dtype)


def short_conv(proj, w, seq, tile):
    m = proj.shape[0]
    width = 1024
    nt = seq // tile
    cp, cm, cn = _halo_specs(tile, width, P_SCC // width, m)
    hp, hm, hn = _halo_specs(tile, width, P_SCH // width, m)
    return pl.pallas_call(
        functools.partial(_short_conv_kernel, nt=nt, tile=tile),
        grid=(m // tile, SC_WIDTH // width),
        in_specs=[pl.BlockSpec((tile, width), lambda i, j: (i, P_SCB // width + j)),
                  cp, cm, cn, hp, hm, hn,
                  pl.BlockSpec((SC_CONV, width), lambda i, j: (0, j))],
        out_specs=pl.BlockSpec((tile, width), lambda i, j: (i, j)),
        out_shape=jax.ShapeDtypeStruct((m, SC_WIDTH), BF16),
        scratch_shapes=[pltpu.VMEM((tile + 2 * HALO, width), F32)],
        compiler_params=_cparams(("parallel", "parallel")),
        name="short_conv",
    )(proj, proj, proj, proj, proj, proj, proj, w)


def _split3(a):
    a1 = a.astype(BF16)
    r1 = a - a1.astype(F32)
    a2 = r1.astype(BF16)
    a3 = (r1 - a2.astype(F32)).astype(BF16)
    return a1, a2, a3


def _softplus(x):
    return jnp.maximum(x, 0.0) + jnp.log1p(jnp.exp(-jnp.abs(x)))


def _ssd_kernel(*refs, backward):
    if backward:
        (x_ref, b_ref, c_ref, dt_ref, dtb_ref, alog_ref, h0_ref, yf_ref, z_ref, d_ref, gn_ref,
         y_ref, st_ref, ybuf_ref) = refs
        acc_ref = ybuf_ref
    else:
        x_ref, b_ref, c_ref, dt_ref, dtb_ref, alog_ref, h0_ref, y_ref, st_ref = refs
        acc_ref = y_ref
    q = SSM_CHUNK

    @pl.when(pl.program_id(1) == 0)
    def _():
        st_ref[...] = h0_ref[...]

    lo = SSM_HEADS if backward else 0
    dt = _softplus(dt_ref[...] + dtb_ref[...])
    a2 = dt * (-LOG2E * jnp.exp(alog_ref[...]))
    ii = lax.broadcasted_iota(jnp.int32, (q, q), 0)
    jj = lax.broadcasted_iota(jnp.int32, (q, q), 1)
    tri = (jj >= ii) if backward else (jj <= ii)
    tri_b = jnp.where(tri, 1.0, 0.0).astype(BF16)
    u2 = None
    for piece in _split3(a2):
        d = jnp.dot(tri_b, piece, preferred_element_type=F32)
        u2 = d if u2 is None else u2 + d
    v_t = (u2 - jnp.log2(dt)).T
    edge = 0 if backward else q - 1
    t_row = u2[edge:edge + 1, :]
    lane_lo = lax.broadcasted_iota(jnp.int32, (1, LANES), 1) < SSM_HEAD_DIM
    heads_per_group = SSM_HEADS // SSM_GROUPS
    pairs_per_group = heads_per_group // 2

    for g in range(SSM_GROUPS):
        b_g = b_ref[:, g * SSM_STATE:(g + 1) * SSM_STATE]
        c_g = c_ref[:, g * SSM_STATE:(g + 1) * SSM_STATE]
        cb = lax.dot_general(c_g, b_g, (((1,), (1,)), ((), ())), preferred_element_type=F32)
        h_t = st_ref[0, g]
        y_off = jnp.dot(c_g, h_t.astype(BF16), preferred_element_type=F32)
        xs_parts = []
        dec_parts = []
        for pr in range(pairs_per_group):
            h_a = g * heads_per_group + 2 * pr
            ca = lo + h_a
            cbk = ca + 1
            pc = (h_a // 2) * LANES
            x_b = x_ref[:, pc:pc + LANES]
            dtp = jnp.where(lane_lo, dt[:, ca:ca + 1], dt[:, cbk:cbk + 1])
            up = jnp.where(lane_lo, u2[:, ca:ca + 1], u2[:, cbk:cbk + 1])
            tp = jnp.where(lane_lo, t_row[:, ca:ca + 1], t_row[:, cbk:cbk + 1])
            ys = []
            for col in (ca, cbk):
                seg = u2[:, col:col + 1] - v_t[col:col + 1, :]
                decay_dt = jnp.exp2(jnp.where(tri, seg, -jnp.inf))
                ys.append(jnp.dot((cb * decay_dt).astype(BF16), x_b, preferred_element_type=F32))
            acc_ref[:, pc:pc + LANES] = (jnp.where(lane_lo, ys[0], ys[1])
                                         + jnp.exp2(up) * y_off[:, pr * LANES:(pr + 1) * LANES])
            xs_parts.append((x_b.astype(F32) * (dtp * jnp.exp2(tp - up))).astype(BF16))
            dec_parts.append(jnp.exp2(tp))
        xs_g = jnp.concatenate(xs_parts, axis=1)
        dec_g = jnp.concatenate(dec_parts, axis=1)
        upd = lax.dot_general(b_g, xs_g, (((0,), (0,)), ((), ())), preferred_element_type=F32)
        st_ref[0, g] = dec_g * h_t + upd

    if backward:
        y = ybuf_ref[...] + yf_ref[...] + x_ref[...].astype(F32) * d_ref[...]
        y = y * _silu(z_ref[...].astype(F32))
        y_ref[...] = _rms(y, gn_ref[...]).astype(y_ref.dtype)


def ssd_scan(xbc, dt_src, dt_block, dt_bias, a_log, h0, batch, seq, finish=None):
    backward = finish is not None
    m = batch * seq
    nc = seq // SSM_CHUNK
    gw = SSM_INNER // SSM_GROUPS

    def row(b, c):
        return b * nc + (nc - 1 - c if backward else c)

    rows = pl.BlockSpec((SSM_CHUNK, SSM_INNER), lambda b, c: (row(b, c), 0))
    vec = pl.BlockSpec((1, SSM_INNER), lambda b, c: (0, 0))
    state = pl.BlockSpec((1, SSM_GROUPS, SSM_STATE, gw), lambda b, c: (b, 0, 0, 0))
    in_specs = [rows,
                pl.BlockSpec((SSM_CHUNK, SSM_BC), lambda b, c: (row(b, c), SSM_INNER // SSM_BC)),
                pl.BlockSpec((SSM_CHUNK, SSM_BC), lambda b, c: (row(b, c), SSM_INNER // SSM_BC + 1)),
                pl.BlockSpec((SSM_CHUNK, LANES), lambda b, c: (row(b, c), dt_block)),
                pl.BlockSpec((1, LANES), lambda b, c: (0, 0)),
                pl.BlockSpec((1, LANES), lambda b, c: (0, 0)),
                state]
    args = [xbc, xbc, xbc, dt_src, dt_bias, a_log, h0]
    scratch = []
    if backward:
        y_fwd, proj, d_exp, norm_g = finish
        in_specs += [rows,
                     pl.BlockSpec((SSM_CHUNK, SSM_INNER), lambda b, c: (row(b, c), P_Z // SSM_INNER)),
                     vec, vec]
        args += [y_fwd, proj, d_exp, norm_g.reshape(1, SSM_INNER)]
        scratch = [pltpu.VMEM((SSM_CHUNK, SSM_INNER), F32)]
    return pl.pallas_call(
        functools.partial(_ssd_kernel, backward=backward),
        grid=(batch, nc),
        in_specs=in_specs,
        out_specs=[rows, state],
        out_shape=[jax.ShapeDtypeStruct((m, SSM_INNER), BF16 if backward else F32),
                   jax.ShapeDtypeStruct((batch, SSM_GROUPS, SSM_STATE, gw), F32)],
        scratch_shapes=scratch,
        compiler_params=_cparams(("parallel", "arbitrary")),
        name="ssd_scan_bwd" if backward else "ssd_scan_fwd",
    )(*args)


def _state_to_kernel_layout(h):
    b = h.shape[0]
    r = SSM_HEADS // SSM_GROUPS
    h = h.reshape(b, SSM_GROUPS, r, SSM_HEAD_DIM, SSM_STATE)
    return jnp.transpose(h, (0, 1, 4, 2, 3)).reshape(b, SSM_GROUPS, SSM_STATE, r * SSM_HEAD_DIM)


def _state_from_kernel_layout(h):
    b = h.shape[0]
    r = SSM_HEADS // SSM_GROUPS
    h = h.reshape(b, SSM_GROUPS, SSM_STATE, r, SSM_HEAD_DIM)
    return jnp.transpose(h, (0, 1, 3, 4, 2)).reshape(b, SSM_HEADS, SSM_HEAD_DIM, SSM_STATE)


def _pad_lanes(v):
    flat = v.reshape(1, 2 * SSM_HEADS).astype(F32)
    return jnp.pad(flat, ((0, 0), (0, LANES - 2 * SSM_HEADS)))


def _layer_weights(l, w_in, w_att_out, w_ssd_out, w_sc_out, w_o, w_gate_up, w_down):
    w = w_in[l]

    def cols(o, n):
        return w[:, o:o + n]

    w_main = jnp.concatenate(
        [cols(_O_Q, Q_W), cols(_O_Z, SSM_INNER), cols(_O_SCB, SC_WIDTH), cols(_O_SCC, SC_WIDTH),
         cols(_O_SCH, SC_WIDTH), cols(_O_G, N_BRANCH * D_MODEL), cols(_O_XS, SSM_INNER),
         cols(_O_BM, SSM_BC), cols(_O_CM, SSM_BC), cols(_O_K, KV_W), cols(_O_V, KV_W)],
        axis=1).astype(BF16)
    dt_w = jnp.pad(cols(_O_DT, 2 * SSM_HEADS), ((0, 0), (0, LANES - 2 * SSM_HEADS)))
    w_side_ctx = jnp.concatenate([cols(_O_K, KV_W), cols(_O_V, KV_W), dt_w], axis=1).astype(BF16)
    return dict(w_main=w_main, w_side_ctx=w_side_ctx, w_side_lat=dt_w.astype(BF16),
                w_att=w_att_out[l].astype(BF16), w_ssd=w_ssd_out[l].astype(BF16),
                w_sc=w_sc_out[l].astype(BF16), w_o=w_o[l].astype(BF16),
                w_gate_up=w_gate_up[l].astype(BF16), w_down=w_down[l].astype(BF16))


def _tile_plan(seq, side_cols):
    return dict(
        in_proj=(512 if side_cols > LANES else 1024, 1024),
        swiglu=(1024, 512),
        down=(512, D_FF // 4),
        w_o=(512, D_MODEL),
        merge=(512, 512),
        conv=min(seq, 512))


def _trunk_layer(x, mod, mod_row, lw, p, batch, seq, ctx_cache, rope):
    if ctx_cache is None:
        w_side = lw['w_side_ctx']
        dt_block = 2 * KV_W // LANES
    else:
        w_side = lw['w_side_lat']
        dt_block = 0
    plan = _tile_plan(seq, w_side.shape[1])
    proj, side = in_proj(x, p['g_pre_mix'], mod, mod_row, lw['w_main'], w_side, *plan['in_proj'])
    if ctx_cache is None:
        y_att = context_attention(proj, p['sink'], batch, seq)
        h0 = jnp.zeros((2, batch, SSM_GROUPS, SSM_STATE, SSM_INNER // SSM_GROUPS), F32)
    else:
        ctx_k, ctx_v, state = ctx_cache
        y_att = latent_attention(proj, ctx_k.reshape(batch, -1, KV_W).astype(BF16),
                                 ctx_v.reshape(batch, -1, KV_W).astype(BF16), p['sink'],
                                 rope[0], rope[1], batch, seq)
        h0 = jnp.stack([_state_to_kernel_layout(state[:, 0]), _state_to_kernel_layout(state[:, 1])])
    xbc = ssm_conv(proj, p['ssm_conv_w'], p['ssm_conv_b'], seq, plan['conv'])
    dt_bias = _pad_lanes(p['ssm_dt_bias'])
    a_log = _pad_lanes(p['ssm_a_log'])
    d_exp = jnp.repeat(p['ssm_d'].astype(F32), SSM_HEAD_DIM).reshape(1, SSM_INNER)
    y_f, h_f = ssd_scan(xbc, side, dt_block, dt_bias, a_log, h0[0], batch, seq)
    y_ssd, h_b = ssd_scan(xbc, side, dt_block, dt_bias, a_log, h0[1], batch, seq,
                          finish=(y_f, proj, d_exp, p['ssm_norm_g']))
    y_sc = short_conv(proj, p['sc_conv_w'], seq, plan['conv'])
    merged = merge_branches(y_att, y_ssd, y_sc, lw['w_att'], lw['w_ssd'], lw['w_sc'], proj,
                            *plan['merge'])
    x = matmul_norm_residual(merged, lw['w_o'], x, p['g_post_mix'], mod, mod_row, 2, *plan['w_o'])
    act = norm_swiglu(x, p['g_pre_ffn'], mod, mod_row, lw['w_gate_up'], *plan['swiglu'])
    x = matmul_norm_residual(act, lw['w_down'], x, p['g_post_ffn'], mod, mod_row, 5, *plan['down'])
    new_ctx = None
    if ctx_cache is None:
        k = side[:, :KV_W].reshape(batch, seq, N_KV_HEADS, HEAD_DIM)
        v = side[:, KV_W:2 * KV_W].reshape(batch, seq, N_KV_HEADS, HEAD_DIM)
        h_last = jnp.stack([_state_from_kernel_layout(h_f), _state_from_kernel_layout(h_b)], axis=1)
        new_ctx = (k, v, h_last)
    return x, new_ctx


def kernel(x_prompt, x_sample, cache_k, cache_v, state_ssm, c, c_ctx, w_mod, b_mod, g_pre_mix, w_in, sink, ssm_conv_w, ssm_conv_b, ssm_dt_bias, ssm_a_log, ssm_d, ssm_norm_g, sc_conv_w, w_att_out, w_ssd_out, w_sc_out, w_o, g_post_mix, g_pre_ffn, w_gate_up, w_down, g_post_ffn):
    batch, seq, _ = x_prompt.shape
    dec_batch, dec_seq, _ = x_sample.shape
    cond = jnp.concatenate([c_ctx[None, :], c, jnp.zeros((8 - 1 - dec_batch, D_MODEL), F32)], axis=0)
    mod = modulation(cond, w_mod, b_mod).reshape(DEPTH, 8, N_MOD, D_MODEL)
    rope = rope_tables(dec_seq)
    y_prompt = x_prompt.reshape(batch * seq, D_MODEL)
    y_sample = x_sample.reshape(dec_batch * dec_seq, D_MODEL)
    ks_out, vs_out, ss_out = [], [], []
    for l in range(DEPTH):
        lw = _layer_weights(l, w_in, w_att_out, w_ssd_out, w_sc_out, w_o, w_gate_up, w_down)
        p = {'g_pre_mix': g_pre_mix[l], 'sink': sink[l], 'ssm_conv_w': ssm_conv_w[l],
             'ssm_conv_b': ssm_conv_b[l], 'ssm_dt_bias': ssm_dt_bias[l], 'ssm_a_log': ssm_a_log[l],
             'ssm_d': ssm_d[l], 'ssm_norm_g': ssm_norm_g[l], 'sc_conv_w': sc_conv_w[l],
             'g_post_mix': g_post_mix[l], 'g_pre_ffn': g_pre_ffn[l], 'g_post_ffn': g_post_ffn[l]}
        y_prompt, (k_l, v_l, s_l) = _trunk_layer(
            y_prompt, mod[l], lambda r: 0, lw, p, batch, seq, None, None)
        ks_out.append(k_l)
        vs_out.append(v_l)
        ss_out.append(s_l)
        y_sample, _ = _trunk_layer(
            y_sample, mod[l], lambda r: 1 + r // dec_seq, lw, p, dec_batch, dec_seq,
            (cache_k[:, l], cache_v[:, l], state_ssm[:, l]), rope)
    return (y_prompt.reshape(batch, seq, D_MODEL), y_sample.reshape(dec_batch, dec_seq, D_MODEL),
            jnp.stack(ks_out, axis=1), jnp.stack(vs_out, axis=1), jnp.stack(ss_out, axis=1))
```

```python
import functools

import jax
import jax.numpy as jnp
from jax import lax
from jax.experimental import pallas as pl
from jax.experimental.pallas import tpu as pltpu

F32 = jnp.float32
BF16 = jnp.bfloat16

D_MODEL = 2048
DEPTH = 2
GRID_W = 64
N_HEADS = 16
N_KV_HEADS = 4
KV_REP = N_HEADS // N_KV_HEADS
HEAD_DIM = 128
Q_W = N_HEADS * HEAD_DIM
KV_W = N_KV_HEADS * HEAD_DIM
WINDOW = 128
ATT_BLOCK = 128
ATT_SCALE = HEAD_DIM ** -0.5
ROPE_THETA = 10000.0
SSM_HEADS = 32
SSM_HEAD_DIM = 64
SSM_INNER = SSM_HEADS * SSM_HEAD_DIM
SSM_GROUPS = 4
SSM_STATE = 128
SSM_BC = SSM_GROUPS * SSM_STATE
SSM_CONV_CH = SSM_INNER + 2 * SSM_BC
SSM_CONV = 5
SSM_CHUNK = 128
SC_WIDTH = D_MODEL
SC_CONV = 3
N_BRANCH = 3
D_FF = ((8 * D_MODEL + 3 * 256 - 1) // (3 * 256)) * 256
N_MOD = 6
EPS = 1e-6
LOG2E = 1.4426950408889634

_O_Q = 0
_O_K = _O_Q + Q_W
_O_V = _O_K + KV_W
_O_Z = _O_V + KV_W
_O_XS = _O_Z + SSM_INNER
_O_BM = _O_XS + SSM_INNER
_O_CM = _O_BM + SSM_BC
_O_DT = _O_CM + SSM_BC
_O_SCB = _O_DT + 2 * SSM_HEADS
_O_SCC = _O_SCB + SC_WIDTH
_O_SCH = _O_SCC + SC_WIDTH
_O_G = _O_SCH + SC_WIDTH

P_Q = 0
P_Z = 2048
P_SCB = 4096
P_SCC = 6144
P_SCH = 8192
P_G = 10240
P_XS = 16384
P_BM = 18432
P_CM = 18944
P_K = 19456
P_V = 19968
P_N = 20480

LANES = 128
HALO = 16
VMEM_LIMIT = 56 * 1024 * 1024


def _cparams(sem, **flags):
    return pltpu.CompilerParams(dimension_semantics=sem, vmem_limit_bytes=VMEM_LIMIT,
                                flags=flags or None)


def _rms(x, g):
    return x * lax.rsqrt(jnp.mean(x * x, axis=-1, keepdims=True) + EPS) * g


def _silu(x):
    return x * jax.nn.sigmoid(x)


def _mod_kernel(c_ref, w_ref, b_ref, o_ref):
    a = _silu(c_ref[...]).astype(BF16)
    o_ref[0] = jnp.dot(a, w_ref[0].astype(BF16), preferred_element_type=F32) + b_ref[0]


def modulation(cond, w_mod, b_mod):
    rows = cond.shape[0]
    n = N_MOD * D_MODEL
    tn = 1024
    return pl.pallas_call(
        _mod_kernel,
        grid=(DEPTH, n // tn),
        in_specs=[pl.BlockSpec((rows, D_MODEL), lambda l, j: (0, 0)),
                  pl.BlockSpec((1, D_MODEL, tn), lambda l, j: (l, 0, j)),
                  pl.BlockSpec((1, 1, tn), lambda l, j: (l, 0, j))],
        out_specs=pl.BlockSpec((1, rows, tn), lambda l, j: (l, 0, j)),
        out_shape=jax.ShapeDtypeStruct((DEPTH, rows, n), F32),
        compiler_params=_cparams(("parallel", "parallel")),
        name="modulation",
    )(cond, w_mod, b_mod.reshape(DEPTH, 1, n))


NORM_ROWS = 128


def _by_parity(i, fn, buf_a, buf_b):
    @pl.when(i % 2 == 0)
    def _():
        fn(buf_a, buf_b)

    @pl.when(i % 2 == 1)
    def _():
        fn(buf_b, buf_a)


def _norm_chunk(x_ref, g_ref, mod_ref, h_ref, j, tm, shift_i, scale_i):
    r0 = pl.multiple_of(jnp.minimum(j, tm // NORM_ROWS - 1) * NORM_ROWS, NORM_ROWS)
    y = _rms(x_ref[pl.ds(r0, NORM_ROWS), :], g_ref[...])
    h = y * (1.0 + mod_ref[0, scale_i:scale_i + 1, :]) + mod_ref[0, shift_i:shift_i + 1, :]
    h_ref[pl.ds(r0, NORM_ROWS), :] = h.astype(BF16)


def _in_proj_kernel(x_ref, g_ref, mod_ref, w_ref, ws_ref, o_ref, os_ref, ha_ref, hb_ref, *, tm):
    i = pl.program_id(0)
    j = pl.program_id(1)

    def row(h_next, h_cur):
        @pl.when(i == 0)
        def _():
            _norm_chunk(x_ref, g_ref, mod_ref, h_next, j, tm, 0, 1)

        @pl.when((i > 0) & (j == 0))
        def _():
            os_ref[...] = jnp.dot(h_cur[...], ws_ref[...], preferred_element_type=F32)

        @pl.when(i > 0)
        def _():
            _norm_chunk(x_ref, g_ref, mod_ref, h_next, j, tm, 0, 1)
            o_ref[...] = jnp.dot(h_cur[...], w_ref[...],
                                 preferred_element_type=F32).astype(o_ref.dtype)

    _by_parity(i, row, ha_ref, hb_ref)


def _pipelined_row_specs(m, tm, mod_row):
    nt = m // tm
    x_spec = pl.BlockSpec((tm, D_MODEL), lambda i, j: (jnp.minimum(i, nt - 1), 0))
    mod_spec = pl.BlockSpec((1, N_MOD, D_MODEL),
                            lambda i, j: (mod_row(jnp.minimum(i, nt - 1) * tm), 0, 0))
    return nt, x_spec, mod_spec


def in_proj(x, g, mod, mod_row, w, w_side, l, tm, tn):
    m = x.shape[0]
    n = w.shape[2]
    ns = w_side.shape[2]
    assert n // tn >= tm // NORM_ROWS
    nt, x_spec, mod_spec = _pipelined_row_specs(m, tm, mod_row)

    def col(i, j):
        return jnp.where(i == 0, 0, j)

    return pl.pallas_call(
        functools.partial(_in_proj_kernel, tm=tm),
        grid=(nt + 1, n // tn),
        in_specs=[x_spec,
                  pl.BlockSpec((1, D_MODEL), lambda i, j: (0, 0)),
                  mod_spec,
                  pl.BlockSpec((None, D_MODEL, tn), lambda i, j: (l, 0, col(i, j))),
                  pl.BlockSpec((None, D_MODEL, ns), lambda i, j: (l, 0, 0))],
        out_specs=[pl.BlockSpec((tm, tn), lambda i, j: (jnp.maximum(i - 1, 0), col(i, j))),
                   pl.BlockSpec((tm, ns), lambda i, j: (jnp.maximum(i - 1, 0), 0))],
        out_shape=[jax.ShapeDtypeStruct((m, n), BF16), jax.ShapeDtypeStruct((m, ns), F32)],
        scratch_shapes=[pltpu.VMEM((tm, D_MODEL), BF16), pltpu.VMEM((tm, D_MODEL), BF16)],
        compiler_params=_cparams(("arbitrary", "arbitrary")),
        name="in_proj",
    )(x, g.reshape(1, D_MODEL), mod, w, w_side)


def _nm_swiglu_kernel(x_ref, g_ref, mod_ref, wg_ref, wu_ref, o_ref, ha_ref, hb_ref, *, tm):
    i = pl.program_id(0)
    j = pl.program_id(1)

    def row(h_next, h_cur):
        @pl.when(i == 0)
        def _():
            _norm_chunk(x_ref, g_ref, mod_ref, h_next, j, tm, 3, 4)

        @pl.when(i > 0)
        def _():
            _norm_chunk(x_ref, g_ref, mod_ref, h_next, j, tm, 3, 4)
            h = h_cur[...]
            a = jnp.dot(h, wg_ref[...], preferred_element_type=F32)
            b = jnp.dot(h, wu_ref[...], preferred_element_type=F32)
            o_ref[...] = (_silu(a) * b).astype(o_ref.dtype)

    _by_parity(i, row, ha_ref, hb_ref)


def norm_swiglu(x, g, mod, mod_row, w_gate_up, l, tm, tn):
    m = x.shape[0]
    nb = D_FF // tn
    assert nb >= tm // NORM_ROWS
    nt, x_spec, mod_spec = _pipelined_row_specs(m, tm, mod_row)

    def col(i, j):
        return jnp.where(i == 0, 0, j)

    return pl.pallas_call(
        functools.partial(_nm_swiglu_kernel, tm=tm),
        grid=(nt + 1, nb),
        in_specs=[x_spec,
                  pl.BlockSpec((1, D_MODEL), lambda i, j: (0, 0)),
                  mod_spec,
                  pl.BlockSpec((None, D_MODEL, tn), lambda i, j: (l, 0, col(i, j))),
                  pl.BlockSpec((None, D_MODEL, tn), lambda i, j: (l, 0, col(i, j) + nb))],
        out_specs=pl.BlockSpec((tm, tn), lambda i, j: (jnp.maximum(i - 1, 0), col(i, j))),
        out_shape=jax.ShapeDtypeStruct((m, D_FF), BF16),
        scratch_shapes=[pltpu.VMEM((tm, D_MODEL), BF16), pltpu.VMEM((tm, D_MODEL), BF16)],
        compiler_params=_cparams(("arbitrary", "arbitrary")),
        name="norm_swiglu",
    )(x, g.reshape(1, D_MODEL), mod, w_gate_up, w_gate_up)


def _matmul_norm_res_kernel(a_ref, w_ref, x_ref, g_ref, mod_ref, o_ref, acca_ref, accb_ref,
                            *, gate_i, nt, ksteps, tm):
    i = pl.program_id(0)
    k = pl.program_id(1)
    rows = tm // ksteps
    r0 = pl.multiple_of(k * rows, rows)

    @pl.when((i == 0) & (k == 0))
    def _():
        acca_ref[...] = jnp.zeros_like(acca_ref)
        accb_ref[...] = jnp.zeros_like(accb_ref)

    def row(acc_cur, acc_prev):
        def epilogue():
            y = _rms(acc_prev[pl.ds(r0, rows), :], g_ref[...])
            o_ref[pl.ds(r0, rows), :] = (x_ref[pl.ds(r0, rows), :]
                                         + mod_ref[0, gate_i:gate_i + 1, :] * y)

        @pl.when(i < nt)
        def _():
            prod = jnp.dot(a_ref[...], w_ref[...], preferred_element_type=F32)
            if ksteps == 1:
                acc_cur[...] = prod
            else:
                acc_cur[...] = jnp.where(k == 0, prod, acc_cur[...] + prod)
            epilogue()

        @pl.when(i == nt)
        def _():
            epilogue()

    _by_parity(i, row, acca_ref, accb_ref)


def matmul_norm_residual(a, w, l, x, g, mod, mod_row, gate_i, tm, tk):
    m, kdim = a.shape
    nt = m // tm
    ksteps = kdim // tk

    def kk(i, k):
        return jnp.where(i == nt, ksteps - 1, k)

    def prev(i):
        return jnp.maximum(i - 1, 0)

    return pl.pallas_call(
        functools.partial(_matmul_norm_res_kernel, gate_i=gate_i, nt=nt, ksteps=ksteps, tm=tm),
        grid=(nt + 1, ksteps),
        in_specs=[pl.BlockSpec((tm, tk), lambda i, k: (jnp.minimum(i, nt - 1), kk(i, k))),
                  pl.BlockSpec((None, tk, D_MODEL), lambda i, k: (l, kk(i, k), 0)),
                  pl.BlockSpec((tm, D_MODEL), lambda i, k: (prev(i), 0)),
                  pl.BlockSpec((1, D_MODEL), lambda i, k: (0, 0)),
                  pl.BlockSpec((1, N_MOD, D_MODEL), lambda i, k: (mod_row(prev(i) * tm), 0, 0))],
        out_specs=pl.BlockSpec((tm, D_MODEL), lambda i, k: (prev(i), 0)),
        out_shape=jax.ShapeDtypeStruct((m, D_MODEL), F32),
        scratch_shapes=[pltpu.VMEM((tm, D_MODEL), F32), pltpu.VMEM((tm, D_MODEL), F32)],
        compiler_params=_cparams(("arbitrary", "arbitrary")),
        name="matmul_norm_residual",
    )(a, w, x, g.reshape(1, D_MODEL), mod)


def _merge_kernel(ya_ref, ys_ref, yc_ref, wa_ref, ws_ref, wc_ref, ga_ref, gs_ref, gc_ref, o_ref):
    def branch(y_ref, w_ref, gate_ref):
        return jax.nn.sigmoid(gate_ref[...].astype(F32)) * jnp.dot(
            y_ref[...], w_ref[...], preferred_element_type=F32)

    o_ref[...] = (branch(ya_ref, wa_ref, ga_ref) + branch(ys_ref, ws_ref, gs_ref)
                  + branch(yc_ref, wc_ref, gc_ref)).astype(o_ref.dtype)


def merge_branches(y_att, y_ssd, y_sc, w_att, w_ssd, w_sc, l, proj, tm, tn):
    m = y_att.shape[0]
    gb = P_G // tn
    nb = D_MODEL // tn
    y_spec = pl.BlockSpec((tm, D_MODEL), lambda i, j: (i, 0))
    w_spec = pl.BlockSpec((None, D_MODEL, tn), lambda i, j: (l, 0, j))

    def g_spec(b):
        return pl.BlockSpec((tm, tn), lambda i, j: (i, gb + b * nb + j))

    return pl.pallas_call(
        _merge_kernel,
        grid=(m // tm, nb),
        in_specs=[y_spec, y_spec, y_spec, w_spec, w_spec, w_spec, g_spec(0), g_spec(1), g_spec(2)],
        out_specs=pl.BlockSpec((tm, tn), lambda i, j: (i, j)),
        out_shape=jax.ShapeDtypeStruct((m, D_MODEL), BF16),
        compiler_params=_cparams(("parallel", "parallel")),
        name="merge_branches",
    )(y_att, y_ssd, y_sc, w_att, w_ssd, w_sc, proj, proj, proj)


def _softmax_pv(s, sink, v):
    m = jnp.maximum(jnp.max(s, axis=-1, keepdims=True), sink)
    p = jnp.exp(s - m)
    den = jnp.sum(p, axis=-1, keepdims=True) + jnp.exp(sink - m)
    return jnp.dot(p.astype(BF16), v, preferred_element_type=F32) / den


def _ctx_attn_kernel(sink_ref, q_ref, k_ref, v_ref, o_ref):
    for g in range(N_KV_HEADS):
        k_g = k_ref[:, g * HEAD_DIM:(g + 1) * HEAD_DIM]
        v_g = v_ref[:, g * HEAD_DIM:(g + 1) * HEAD_DIM]
        for r in range(KV_REP):
            h = g * KV_REP + r
            q_h = q_ref[:, h * HEAD_DIM:(h + 1) * HEAD_DIM]
            s = lax.dot_general(q_h, k_g, (((1,), (1,)), ((), ())),
                                preferred_element_type=F32) * ATT_SCALE
            o = _softmax_pv(s, sink_ref[h], v_g)
            o_ref[:, h * HEAD_DIM:(h + 1) * HEAD_DIM] = o.astype(o_ref.dtype)


def context_attention(proj, sink, batch, seq):
    m = batch * seq
    return pl.pallas_call(
        _ctx_attn_kernel,
        grid=(batch,),
        in_specs=[pl.BlockSpec(memory_space=pltpu.SMEM),
                  pl.BlockSpec((seq, Q_W), lambda b: (b, P_Q // Q_W)),
                  pl.BlockSpec((seq, KV_W), lambda b: (b, P_K // KV_W)),
                  pl.BlockSpec((seq, KV_W), lambda b: (b, P_V // KV_W))],
        out_specs=pl.BlockSpec((seq, Q_W), lambda b: (b, 0)),
        out_shape=jax.ShapeDtypeStruct((m, Q_W), BF16),
        compiler_params=_cparams(("parallel",)),
        name="context_attention",
    )(sink, proj, proj, proj)


def _rope_tile(x, cos, sin_signed, even_lane):
    nxt = pltpu.roll(x, LANES - 1, axis=1)
    prv = pltpu.roll(x, 1, axis=1)
    return x * cos + jnp.where(even_lane, nxt, prv) * sin_signed


def _lat_attn_kernel(sink_ref, q_ref, k_ref, v_ref, ck_ref, cv_ref, cosq_ref, sinq_ref,
                     cosk_ref, sink_tab_ref, o_ref, krot_ref, *, seq, past):
    i = pl.program_id(1)
    band = ATT_BLOCK + 2 * WINDOW
    rows = 256

    @pl.when(i == 0)
    def _():
        def body(c, carry):
            r0 = pl.multiple_of(c * rows, rows)
            cos = cosk_ref[pl.ds(r0, rows), :]
            sin = sink_tab_ref[pl.ds(r0, rows), :]
            even = lax.broadcasted_iota(jnp.int32, (rows, LANES), 1) % 2 == 0
            for g in range(N_KV_HEADS):
                x = k_ref[pl.ds(r0, rows), g * HEAD_DIM:(g + 1) * HEAD_DIM].astype(F32)
                krot_ref[pl.ds(r0, rows), g * HEAD_DIM:(g + 1) * HEAD_DIM] = _rope_tile(
                    x, cos, sin, even).astype(BF16)
            return carry

        lax.fori_loop(0, seq // rows, body, 0)

    start = pl.multiple_of(jnp.clip(i * ATT_BLOCK - WINDOW, 0, seq - band), ATT_BLOCK)
    kb = krot_ref[pl.ds(start, band), :]
    vb = v_ref[pl.ds(start, band), :]
    ck = ck_ref[0]
    cv = cv_ref[0]
    stacked = KV_REP * ATT_BLOCK
    row = lax.broadcasted_iota(jnp.int32, (stacked, band), 0)
    col = lax.broadcasted_iota(jnp.int32, (stacked, band), 1)
    keep = jnp.abs(start + col - (i * ATT_BLOCK + row % ATT_BLOCK)) <= WINDOW
    head_of_row = lax.broadcasted_iota(jnp.int32, (stacked, 1), 0) // ATT_BLOCK
    cos_q = cosq_ref[...] * (ATT_SCALE * LOG2E)
    sin_q = sinq_ref[...] * (ATT_SCALE * LOG2E)
    even_q = lax.broadcasted_iota(jnp.int32, (ATT_BLOCK, LANES), 1) % 2 == 0
    for g in range(N_KV_HEADS):
        sl = slice(g * HEAD_DIM, (g + 1) * HEAD_DIM)
        kcat = jnp.concatenate([kb[:, sl], ck[:, sl]], axis=0)
        vcat = jnp.concatenate([vb[:, sl], cv[:, sl]], axis=0)
        q_parts = []
        sink2 = jnp.zeros((stacked, 1), F32)
        for r in range(KV_REP):
            h = g * KV_REP + r
            hs = slice(h * HEAD_DIM, (h + 1) * HEAD_DIM)
            q_parts.append(_rope_tile(q_ref[:, hs].astype(F32), cos_q, sin_q, even_q).astype(BF16))
            sink2 = jnp.where(head_of_row == r, sink_ref[h] * LOG2E, sink2)
        q_g = jnp.concatenate(q_parts, axis=0)
        s = lax.dot_general(q_g, kcat, (((1,), (1,)), ((), ())), preferred_element_type=F32)
        s_loc = jnp.where(keep, s[:, :band], -jnp.inf)
        s_ctx = s[:, band:]
        m = jnp.maximum(jnp.maximum(jnp.max(s_loc, axis=-1, keepdims=True),
                                    jnp.max(s_ctx, axis=-1, keepdims=True)), sink2)
        p_loc = jnp.exp2(s_loc - m)
        p_ctx = jnp.exp2(s_ctx - m)
        den = (jnp.sum(p_loc, axis=-1, keepdims=True) + jnp.sum(p_ctx, axis=-1, keepdims=True)
               + jnp.exp2(sink2 - m))
        p = jnp.concatenate([p_loc.astype(BF16), p_ctx.astype(BF16)], axis=1)
        o = jnp.dot(p, vcat, preferred_element_type=F32) / den
        for r in range(KV_REP):
            h = g * KV_REP + r
            o_ref[:, h * HEAD_DIM:(h + 1) * HEAD_DIM] = o[r * ATT_BLOCK:(r + 1) * ATT_BLOCK].astype(
                o_ref.dtype)


def latent_attention(proj, ctx_k, ctx_v, sink, cos_t, sin_t, batch, seq):
    m = batch * seq
    nb = seq // ATT_BLOCK
    past = ctx_k.shape[1]
    return pl.pallas_call(
        functools.partial(_lat_attn_kernel, seq=seq, past=past),
        grid=(batch, nb),
        in_specs=[pl.BlockSpec(memory_space=pltpu.SMEM),
                  pl.BlockSpec((ATT_BLOCK, Q_W), lambda b, i: (b * nb + i, P_Q // Q_W)),
                  pl.BlockSpec((seq, KV_W), lambda b, i: (b, P_K // KV_W)),
                  pl.BlockSpec((seq, KV_W), lambda b, i: (b, P_V // KV_W)),
                  pl.BlockSpec((1, past, KV_W), lambda b, i: (b, 0, 0)),
                  pl.BlockSpec((1, past, KV_W), lambda b, i: (b, 0, 0)),
                  pl.BlockSpec((ATT_BLOCK, LANES), lambda b, i: (i, 0)),
                  pl.BlockSpec((ATT_BLOCK, LANES), lambda b, i: (i, 0)),
                  pl.BlockSpec((seq, LANES), lambda b, i: (0, 0)),
                  pl.BlockSpec((seq, LANES), lambda b, i: (0, 0))],
        out_specs=pl.BlockSpec((ATT_BLOCK, Q_W), lambda b, i: (b * nb + i, 0)),
        out_shape=jax.ShapeDtypeStruct((m, Q_W), BF16),
        scratch_shapes=[pltpu.VMEM((seq, KV_W), BF16)],
        compiler_params=_cparams(("parallel", "arbitrary")),
        name="latent_attention",
    )(sink, proj, proj, proj, ctx_k, ctx_v, cos_t, sin_t, cos_t, sin_t)


def rope_tables(seq):
    rows = seq // GRID_W
    row = jnp.repeat(jnp.arange(rows, dtype=F32), GRID_W)
    col = jnp.tile(jnp.arange(GRID_W, dtype=F32), rows)
    n_pairs = HEAD_DIM // 4
    inv = ROPE_THETA ** (-jnp.arange(n_pairs, dtype=F32) / n_pairs)
    ang = jnp.concatenate([row[:, None] * inv, col[:, None] * inv], axis=-1)
    ang = jnp.repeat(ang, 2, axis=-1)
    sign = jnp.where(jnp.arange(HEAD_DIM) % 2 == 0, -1.0, 1.0).astype(F32)
    return jnp.cos(ang), jnp.sin(ang) * sign


def _fill_halo_buffer(buf_ref, prev, main, nxt, t, nt, tile):
    buf_ref[0:HALO, :] = jnp.where(t == 0, 0.0, prev)
    buf_ref[HALO:HALO + tile, :] = main
    buf_ref[HALO + tile:HALO + tile + HALO, :] = jnp.where(t == nt - 1, 0.0, nxt)


def _conv_taps(buf_ref, w_ref, taps, tile):
    half = (taps - 1) // 2
    acc = None
    for k in range(taps):
        term = w_ref[k:k + 1, :] * buf_ref[pl.ds(HALO - half + k, tile), :]
        acc = term if acc is None else acc + term
    return acc


def _conv_rolled(prev, main, nxt, w_ref, taps, tile):
    half = (taps - 1) // 2
    sub = 8
    acc = w_ref[half:half + 1, :] * main
    for k in range(taps):
        if k != half:
            acc = acc + w_ref[k:k + 1, :] * pltpu.roll(main, (half - k) % tile, axis=0)

    def edge(window):
        out = None
        for k in range(taps):
            term = w_ref[k:k + 1, :] * window[sub - half + k:2 * sub - half + k]
            out = term if out is None else out + term
        return out

    top = edge(jnp.concatenate([prev, main[:2 * sub]], axis=0))
    bottom = edge(jnp.concatenate([main[tile - 2 * sub:], nxt], axis=0))
    return jnp.concatenate([top, acc[sub:tile - sub], bottom], axis=0)


def _ssm_conv_kernel(prev_ref, main_ref, next_ref, w_ref, b_ref, o_ref, *, nt, tile):
    t = pl.program_id(0) % nt
    prev = jnp.where(t == 0, 0.0, prev_ref[...].astype(F32)[HALO - 8:])
    nxt = jnp.where(t == nt - 1, 0.0, next_ref[...].astype(F32)[:8])
    y = _conv_rolled(prev, main_ref[...].astype(F32), nxt, w_ref, SSM_CONV, tile) + b_ref[...]
    o_ref[...] = _silu(y).astype(o_ref.dtype)


def _halo_specs(tile, width, col0, m):
    hb = tile // HALO
    last = m // HALO - 1
    prev = pl.BlockSpec((HALO, width), lambda i, j: (jnp.maximum(i * hb - 1, 0), col0 + j))
    main = pl.BlockSpec((tile, width), lambda i, j: (i, col0 + j))
    nxt = pl.BlockSpec((HALO, width), lambda i, j: (jnp.minimum((i + 1) * hb, last), col0 + j))
    return prev, main, nxt


def ssm_conv(proj, w, b, seq, tile):
    m = proj.shape[0]
    width = 1024
    nt = seq // tile
    prev, main, nxt = _halo_specs(tile, width, P_XS // width, m)
    return pl.pallas_call(
        functools.partial(_ssm_conv_kernel, nt=nt, tile=tile),
        grid=(m // tile, SSM_CONV_CH // width),
        in_specs=[prev, main, nxt,
                  pl.BlockSpec((SSM_CONV, width), lambda i, j: (0, j)),
                  pl.BlockSpec((1, width), lambda i, j: (0, j))],
        out_specs=pl.BlockSpec((tile, width), lambda i, j: (i, j)),
        out_shape=jax.ShapeDtypeStruct((m, SSM_CONV_CH), BF16),
        compiler_params=_cparams(("parallel", "parallel")),
        name="ssm_conv",
    )(proj, proj, proj, w, b.reshape(1, SSM_CONV_CH))


def _short_conv_kernel(b_ref, cp_ref, cm_ref, cn_ref, hp_ref, hm_ref, hn_ref, w_ref, o_ref,
                       *, nt, tile):
    t = pl.program_id(0) % nt

    def prod(c_ref, h_ref):
        return c_ref[...].astype(F32) * h_ref[...].astype(F32)

    prev = jnp.where(t == 0, 0.0, prod(cp_ref, hp_ref)[HALO - 8:])
    nxt = jnp.where(t == nt - 1, 0.0, prod(cn_ref, hn_ref)[:8])
    y = _conv_rolled(prev, prod(cm_ref, hm_ref), nxt, w_ref, SC_CONV, tile)
    o_ref[...] = (b_ref[...].astype(F32) * y).astype(o_ref.dtype)


def short_conv(proj, w, seq, tile):
    m = proj.shape[0]
    width = 1024
    nt = seq // tile
    cp, cm, cn = _halo_specs(tile, width, P_SCC // width, m)
    hp, hm, hn = _halo_specs(tile, width, P_SCH // width, m)
    return pl.pallas_call(
        functools.partial(_short_conv_kernel, nt=nt, tile=tile),
        grid=(m // tile, SC_WIDTH // width),
        in_specs=[pl.BlockSpec((tile, width), lambda i, j: (i, P_SCB // width + j)),
                  cp, cm, cn, hp, hm, hn,
                  pl.BlockSpec((SC_CONV, width), lambda i, j: (0, j))],
        out_specs=pl.BlockSpec((tile, width), lambda i, j: (i, j)),
        out_shape=jax.ShapeDtypeStruct((m, SC_WIDTH), BF16),
        compiler_params=_cparams(("parallel", "parallel")),
        name="short_conv",
    )(proj, proj, proj, proj, proj, proj, proj, w)


def _split3(a):
    a1 = a.astype(BF16)
    r1 = a - a1.astype(F32)
    a2 = r1.astype(BF16)
    a3 = (r1 - a2.astype(F32)).astype(BF16)
    return a1, a2, a3


def _softplus(x):
    return jnp.maximum(x, 0.0) + jnp.log1p(jnp.exp(-jnp.abs(x)))


def _ssd_kernel(*refs, backward, has_h0, emit_state):
    refs = list(refs)
    x_ref, b_ref, c_ref, dt_ref, dtb_ref, alog_ref, sel_ref = refs[:7]
    del refs[:7]
    h0_ref = refs.pop(0) if has_h0 else None
    if backward:
        yf_ref, z_ref, d_ref, gn_ref = refs[:4]
        del refs[:4]
    y_ref = refs.pop(0)
    hout_ref = refs.pop(0) if emit_state else None
    st_ref = refs.pop(0)
    acc_ref = refs.pop(0) if backward else y_ref
    q = SSM_CHUNK
    gw = (SSM_HEADS // SSM_GROUPS) * SSM_HEAD_DIM

    @pl.when(pl.program_id(1) == 0)
    def _():
        for g in range(SSM_GROUPS):
            if has_h0:
                st_ref[g] = h0_ref[0, 0, g * gw:(g + 1) * gw, :].T
            else:
                st_ref[g] = jnp.zeros((SSM_STATE, gw), F32)

    lo = SSM_HEADS if backward else 0
    dt = _softplus(dt_ref[...] + dtb_ref[...])
    a2 = dt * (-LOG2E * jnp.exp(alog_ref[...]))
    ii = lax.broadcasted_iota(jnp.int32, (q, q), 0)
    jj = lax.broadcasted_iota(jnp.int32, (q, q), 1)
    tri = (jj >= ii) if backward else (jj <= ii)
    tri_b = jnp.where(tri, 1.0, 0.0).astype(BF16)
    u2 = None
    for piece in _split3(a2):
        d = jnp.dot(tri_b, piece, preferred_element_type=F32)
        u2 = d if u2 is None else u2 + d
    v_t = (u2 - jnp.log2(dt)).T
    edge = 0 if backward else q - 1
    t_row = u2[edge:edge + 1, :]
    w_c = dt * jnp.exp2(t_row - u2)
    w_hi = w_c.astype(BF16)
    w_lo = (w_c - w_hi.astype(F32)).astype(BF16)
    w_exp = (jnp.dot(w_hi, sel_ref[...], preferred_element_type=F32)
             + jnp.dot(w_lo, sel_ref[...], preferred_element_type=F32))
    lane_lo = lax.broadcasted_iota(jnp.int32, (1, LANES), 1) < SSM_HEAD_DIM
    heads_per_group = SSM_HEADS // SSM_GROUPS
    pairs_per_group = heads_per_group // 2

    for g in range(SSM_GROUPS):
        b_g = b_ref[:, g * SSM_STATE:(g + 1) * SSM_STATE]
        c_g = c_ref[:, g * SSM_STATE:(g + 1) * SSM_STATE]
        cb = lax.dot_general(c_g, b_g, (((1,), (1,)), ((), ())), preferred_element_type=F32)
        h_t = st_ref[g]
        y_off = jnp.dot(c_g, h_t.astype(BF16), preferred_element_type=F32)
        xs_parts = []
        dec_parts = []
        for pr in range(pairs_per_group):
            h_a = g * heads_per_group + 2 * pr
            ca = lo + h_a
            cbk = ca + 1
            pc = (h_a // 2) * LANES
            x_b = x_ref[:, pc:pc + LANES]
            up = jnp.where(lane_lo, u2[:, ca:ca + 1], u2[:, cbk:cbk + 1])
            tp = jnp.where(lane_lo, t_row[:, ca:ca + 1], t_row[:, cbk:cbk + 1])
            ys = []
            for col in (ca, cbk):
                seg = u2[:, col:col + 1] - v_t[col:col + 1, :]
                decay_dt = jnp.exp2(jnp.where(tri, seg, -jnp.inf))
                ys.append(jnp.dot((cb * decay_dt).astype(BF16), x_b, preferred_element_type=F32))
            acc_ref[:, pc:pc + LANES] = (jnp.where(lane_lo, ys[0], ys[1])
                                         + jnp.exp2(up) * y_off[:, pr * LANES:(pr + 1) * LANES])
            xs_parts.append((x_b.astype(F32) * w_exp[:, pc:pc + LANES]).astype(BF16))
            dec_parts.append(jnp.exp2(tp))
        xs_g = jnp.concatenate(xs_parts, axis=1)
        dec_g = jnp.concatenate(dec_parts, axis=1)
        upd = lax.dot_general(b_g, xs_g, (((0,), (0,)), ((), ())), preferred_element_type=F32)
        st_ref[g] = dec_g * h_t + upd

    if emit_state:
        @pl.when(pl.program_id(1) == pl.num_programs(1) - 1)
        def _():
            for g in range(SSM_GROUPS):
                hout_ref[0, g * gw:(g + 1) * gw, :] = st_ref[g].T

    if backward:
        y = acc_ref[...] + yf_ref[...] + x_ref[...].astype(F32) * d_ref[...]
        y = y * _silu(z_ref[...].astype(F32))
        y_ref[...] = _rms(y, gn_ref[...]).astype(y_ref.dtype)


def _head_selector(backward):
    lane = jnp.arange(LANES)[:, None]
    head = jnp.arange(SSM_INNER)[None, :] // SSM_HEAD_DIM
    return (lane == head + (SSM_HEADS if backward else 0)).astype(BF16)


def ssd_scan(xbc, dt_src, dt_block, dt_bias, a_log, h0, batch, seq, emit_state, finish=None):
    backward = finish is not None
    m = batch * seq
    nc = seq // SSM_CHUNK
    gw = SSM_INNER // SSM_GROUPS
    hp = SSM_HEADS * SSM_HEAD_DIM

    def row(b, c):
        return b * nc + (nc - 1 - c if backward else c)

    rows = pl.BlockSpec((SSM_CHUNK, SSM_INNER), lambda b, c: (row(b, c), 0))
    vec = pl.BlockSpec((1, SSM_INNER), lambda b, c: (0, 0))
    in_specs = [rows,
                pl.BlockSpec((SSM_CHUNK, SSM_BC), lambda b, c: (row(b, c), SSM_INNER // SSM_BC)),
                pl.BlockSpec((SSM_CHUNK, SSM_BC), lambda b, c: (row(b, c), SSM_INNER // SSM_BC + 1)),
                pl.BlockSpec((SSM_CHUNK, LANES), lambda b, c: (row(b, c), dt_block)),
                pl.BlockSpec((1, LANES), lambda b, c: (0, 0)),
                pl.BlockSpec((1, LANES), lambda b, c: (0, 0)),
                pl.BlockSpec((LANES, SSM_INNER), lambda b, c: (0, 0))]
    args = [xbc, xbc, xbc, dt_src, dt_bias, a_log, _head_selector(backward)]
    if h0 is not None:
        state, slab = h0
        in_specs.append(pl.BlockSpec((1, 1, hp, SSM_STATE), lambda b, c: (b, slab, 0, 0)))
        args.append(state)
    scratch = [pltpu.VMEM((SSM_GROUPS, SSM_STATE, gw), F32)]
    if backward:
        y_fwd, proj, d_exp, norm_g = finish
        in_specs += [rows,
                     pl.BlockSpec((SSM_CHUNK, SSM_INNER), lambda b, c: (row(b, c), P_Z // SSM_INNER)),
                     vec, vec]
        args += [y_fwd, proj, d_exp, norm_g.reshape(1, SSM_INNER)]
        scratch.append(pltpu.VMEM((SSM_CHUNK, SSM_INNER), F32))
    out_specs = [rows]
    out_shape = [jax.ShapeDtypeStruct((m, SSM_INNER), BF16 if backward else F32)]
    if emit_state:
        out_specs.append(pl.BlockSpec((1, hp, SSM_STATE), lambda b, c: (b, 0, 0)))
        out_shape.append(jax.ShapeDtypeStruct((batch, hp, SSM_STATE), F32))
    out = pl.pallas_call(
        functools.partial(_ssd_kernel, backward=backward, has_h0=h0 is not None, emit_state=emit_state),
        grid=(batch, nc),
        in_specs=in_specs,
        out_specs=out_specs,
        out_shape=out_shape,
        scratch_shapes=scratch,
        compiler_params=_cparams(("parallel", "arbitrary")),
        name="ssd_scan_bwd" if backward else "ssd_scan_fwd",
    )(*args)
    return (out[0], out[1]) if emit_state else (out[0], None)


def _pad_lanes(v):
    flat = v.reshape(1, 2 * SSM_HEADS).astype(F32)
    return jnp.pad(flat, ((0, 0), (0, LANES - 2 * SSM_HEADS)))


def _prepare_weights(w_in, w_att_out, w_ssd_out, w_sc_out, w_o, w_gate_up, w_down):
    def cols(o, n):
        return w_in[:, :, o:o + n]

    w_main = jnp.concatenate(
        [cols(_O_Q, Q_W), cols(_O_Z, SSM_INNER), cols(_O_SCB, SC_WIDTH), cols(_O_SCC, SC_WIDTH),
         cols(_O_SCH, SC_WIDTH), cols(_O_G, N_BRANCH * D_MODEL), cols(_O_XS, SSM_INNER),
         cols(_O_BM, SSM_BC), cols(_O_CM, SSM_BC), cols(_O_K, KV_W), cols(_O_V, KV_W)],
        axis=2).astype(BF16)
    dt_w = jnp.pad(cols(_O_DT, 2 * SSM_HEADS), ((0, 0), (0, 0), (0, LANES - 2 * SSM_HEADS)))
    w_side_ctx = jnp.concatenate([cols(_O_K, KV_W), cols(_O_V, KV_W), dt_w], axis=2).astype(BF16)
    return dict(w_main=w_main, w_side_ctx=w_side_ctx, w_side_lat=dt_w.astype(BF16),
                w_att=w_att_out.astype(BF16), w_ssd=w_ssd_out.astype(BF16),
                w_sc=w_sc_out.astype(BF16), w_o=w_o.astype(BF16),
                w_gate_up=w_gate_up.astype(BF16), w_down=w_down.astype(BF16))


def _tile_plan(seq, side_cols):
    return dict(
        in_proj=(512 if side_cols > LANES else 1024, 1024),
        swiglu=(1024, 512),
        down=(512, D_FF // 4),
        w_o=(512, D_MODEL),
        merge=(512, 512),
        conv=min(seq, 512))


def _trunk_layer(x, mod, mod_row, l, lw, p, batch, seq, latent):
    if latent is None:
        w_side = lw['w_side_ctx']
        dt_block = 2 * KV_W // LANES
    else:
        w_side = lw['w_side_lat']
        dt_block = 0
    plan = _tile_plan(seq, w_side.shape[2])
    proj, side = in_proj(x, p['g_pre_mix'], mod, mod_row, lw['w_main'], w_side, l, *plan['in_proj'])
    if latent is None:
        y_att = context_attention(proj, p['sink'], batch, seq)
        h0_f = h0_b = None
    else:
        ctx_k, ctx_v, state, rope = latent
        y_att = latent_attention(proj, ctx_k.reshape(batch, -1, KV_W).astype(BF16),
                                 ctx_v.reshape(batch, -1, KV_W).astype(BF16), p['sink'],
                                 rope[0], rope[1], batch, seq)
        h0_f, h0_b = (state, 2 * l), (state, 2 * l + 1)
    xbc = ssm_conv(proj, p['ssm_conv_w'], p['ssm_conv_b'], seq, plan['conv'])
    dt_bias = _pad_lanes(p['ssm_dt_bias'])
    a_log = _pad_lanes(p['ssm_a_log'])
    d_exp = jnp.repeat(p['ssm_d'].astype(F32), SSM_HEAD_DIM).reshape(1, SSM_INNER)
    emit_state = latent is None
    y_f, h_f = ssd_scan(xbc, side, dt_block, dt_bias, a_log, h0_f, batch, seq, emit_state)
    y_ssd, h_b = ssd_scan(xbc, side, dt_block, dt_bias, a_log, h0_b, batch, seq, emit_state,
                          finish=(y_f, proj, d_exp, p['ssm_norm_g']))
    y_sc = short_conv(proj, p['sc_conv_w'], seq, plan['conv'])
    merged = merge_branches(y_att, y_ssd, y_sc, lw['w_att'], lw['w_ssd'], lw['w_sc'], l, proj,
                            *plan['merge'])
    x = matmul_norm_residual(merged, lw['w_o'], l, x, p['g_post_mix'], mod, mod_row, 2, *plan['w_o'])
    act = norm_swiglu(x, p['g_pre_ffn'], mod, mod_row, lw['w_gate_up'], l, *plan['swiglu'])
    x = matmul_norm_residual(act, lw['w_down'], l, x, p['g_post_ffn'], mod, mod_row, 5, *plan['down'])
    new_ctx = None
    if latent is None:
        k = side[:, :KV_W].reshape(batch, seq, N_KV_HEADS, HEAD_DIM)
        v = side[:, KV_W:2 * KV_W].reshape(batch, seq, N_KV_HEADS, HEAD_DIM)
        new_ctx = (k, v, h_f, h_b)
    return x, new_ctx


def kernel(x_prompt, x_sample, cache_k, cache_v, state_ssm, c, c_ctx, w_mod, b_mod, g_pre_mix, w_in, sink, ssm_conv_w, ssm_conv_b, ssm_dt_bias, ssm_a_log, ssm_d, ssm_norm_g, sc_conv_w, w_att_out, w_ssd_out, w_sc_out, w_o, g_post_mix, g_pre_ffn, w_gate_up, w_down, g_post_ffn):
    batch, seq, _ = x_prompt.shape
    dec_batch, dec_seq, _ = x_sample.shape
    cond = jnp.concatenate([c_ctx[None, :], c, jnp.zeros((8 - 1 - dec_batch, D_MODEL), F32)], axis=0)
    mod = modulation(cond, w_mod, b_mod).reshape(DEPTH, 8, N_MOD, D_MODEL)
    rope = rope_tables(dec_seq)
    y_prompt = x_prompt.reshape(batch * seq, D_MODEL)
    y_sample = x_sample.reshape(dec_batch * dec_seq, D_MODEL)
    lw = _prepare_weights(w_in, w_att_out, w_ssd_out, w_sc_out, w_o, w_gate_up, w_down)
    state_in = state_ssm.reshape(dec_batch, 2 * DEPTH, SSM_HEADS * SSM_HEAD_DIM, SSM_STATE)
    ks_out, vs_out, ss_out = [], [], []
    for l in range(DEPTH):
        p = {'g_pre_mix': g_pre_mix[l], 'sink': sink[l], 'ssm_conv_w': ssm_conv_w[l],
             'ssm_conv_b': ssm_conv_b[l], 'ssm_dt_bias': ssm_dt_bias[l], 'ssm_a_log': ssm_a_log[l],
             'ssm_d': ssm_d[l], 'ssm_norm_g': ssm_norm_g[l], 'sc_conv_w': sc_conv_w[l],
             'g_post_mix': g_post_mix[l], 'g_pre_ffn': g_pre_ffn[l], 'g_post_ffn': g_post_ffn[l]}
        y_prompt, (k_l, v_l, h_f, h_b) = _trunk_layer(
            y_prompt, mod[l], lambda r: 0, l, lw, p, batch, seq, None)
        ks_out.append(k_l)
        vs_out.append(v_l)
        ss_out += [h_f, h_b]
        y_sample, _ = _trunk_layer(
            y_sample, mod[l], lambda r: 1 + r // dec_seq, l, lw, p, dec_batch, dec_seq,
            (cache_k[:, l], cache_v[:, l], state_in, rope))
    new_state = jnp.stack(ss_out, axis=1).reshape(
        batch, DEPTH, 2, SSM_HEADS, SSM_HEAD_DIM, SSM_STATE)
    return (y_prompt.reshape(batch, seq, D_MODEL), y_sample.reshape(dec_batch, dec_seq, D_MODEL),
            jnp.stack(ks_out, axis=1), jnp.stack(vs_out, axis=1), new_state)
```

```python
import functools

import jax
import jax.numpy as jnp
from jax import lax
from jax.experimental import pallas as pl
from jax.experimental.pallas import tpu as pltpu

F32 = jnp.float32
BF16 = jnp.bfloat16

D_MODEL = 2048
DEPTH = 2
GRID_W = 64
N_HEADS = 16
N_KV_HEADS = 4
KV_REP = N_HEADS // N_KV_HEADS
HEAD_DIM = 128
Q_W = N_HEADS * HEAD_DIM
KV_W = N_KV_HEADS * HEAD_DIM
WINDOW = 128
ATT_BLOCK = 128
ATT_SCALE = HEAD_DIM ** -0.5
ROPE_THETA = 10000.0
SSM_HEADS = 32
SSM_HEAD_DIM = 64
SSM_INNER = SSM_HEADS * SSM_HEAD_DIM
SSM_GROUPS = 4
SSM_STATE = 128
SSM_BC = SSM_GROUPS * SSM_STATE
SSM_CONV_CH = SSM_INNER + 2 * SSM_BC
SSM_CONV = 5
SSM_CHUNK = 128
SC_WIDTH = D_MODEL
SC_CONV = 3
N_BRANCH = 3
D_FF = ((8 * D_MODEL + 3 * 256 - 1) // (3 * 256)) * 256
N_MOD = 6
EPS = 1e-6
LOG2E = 1.4426950408889634

_O_Q = 0
_O_K = _O_Q + Q_W
_O_V = _O_K + KV_W
_O_Z = _O_V + KV_W
_O_XS = _O_Z + SSM_INNER
_O_BM = _O_XS + SSM_INNER
_O_CM = _O_BM + SSM_BC
_O_DT = _O_CM + SSM_BC
_O_SCB = _O_DT + 2 * SSM_HEADS
_O_SCC = _O_SCB + SC_WIDTH
_O_SCH = _O_SCC + SC_WIDTH
_O_G = _O_SCH + SC_WIDTH

P_Q = 0
P_Z = 2048
P_SCB = 4096
P_SCC = 6144
P_SCH = 8192
P_G = 10240
P_XS = 16384
P_BM = 18432
P_CM = 18944
P_K = 19456
P_V = 19968
P_N = 20480

ATT_STACK = KV_REP
LANES = 128
HALO = 16
VMEM_LIMIT = 56 * 1024 * 1024


def _cparams(sem, **flags):
    return pltpu.CompilerParams(dimension_semantics=sem, vmem_limit_bytes=VMEM_LIMIT,
                                flags=flags or None)


def _rms(x, g):
    return x * lax.rsqrt(jnp.mean(x * x, axis=-1, keepdims=True) + EPS) * g


def _silu(x):
    return x * jax.nn.sigmoid(x)


def _mod_kernel(c_ref, w_ref, b_ref, o_ref):
    a = _silu(c_ref[...]).astype(BF16)
    o_ref[0] = jnp.dot(a, w_ref[0].astype(BF16), preferred_element_type=F32) + b_ref[0]


def modulation(cond, w_mod, b_mod):
    rows = cond.shape[0]
    n = N_MOD * D_MODEL
    tn = 1024
    return pl.pallas_call(
        _mod_kernel,
        grid=(DEPTH, n // tn),
        in_specs=[pl.BlockSpec((rows, D_MODEL), lambda l, j: (0, 0)),
                  pl.BlockSpec((1, D_MODEL, tn), lambda l, j: (l, 0, j)),
                  pl.BlockSpec((1, 1, tn), lambda l, j: (l, 0, j))],
        out_specs=pl.BlockSpec((1, rows, tn), lambda l, j: (l, 0, j)),
        out_shape=jax.ShapeDtypeStruct((DEPTH, rows, n), F32),
        compiler_params=_cparams(("parallel", "parallel")),
        name="modulation",
    )(cond, w_mod, b_mod.reshape(DEPTH, 1, n))


def _norm_mod(x_ref, g_ref, mod_ref, shift_i, scale_i):
    y = _rms(x_ref[...], g_ref[...])
    return y * (1.0 + mod_ref[0, scale_i:scale_i + 1, :]) + mod_ref[0, shift_i:shift_i + 1, :]


def _by_parity(i, fn, buf_a, buf_b):
    @pl.when(i % 2 == 0)
    def _():
        fn(buf_a, buf_b)

    @pl.when(i % 2 == 1)
    def _():
        fn(buf_b, buf_a)


def _in_proj_kernel(x_ref, g_ref, mod_ref, w_ref, wdt_ref, o_ref, odt_ref, *rest, emit_kv):
    okv_ref, h_ref = rest if emit_kv else (None, rest[0])

    @pl.when(pl.program_id(1) == 0)
    def _():
        h = _norm_mod(x_ref, g_ref, mod_ref, 0, 1).astype(BF16)
        h_ref[...] = h
        odt_ref[...] = jnp.dot(h, wdt_ref[...], preferred_element_type=F32)

    res = jnp.dot(h_ref[...], w_ref[...], preferred_element_type=F32)
    o_ref[...] = res.astype(o_ref.dtype)
    if emit_kv:
        @pl.when(pl.program_id(1) == pl.num_programs(1) - 1)
        def _():
            okv_ref[...] = res


def in_proj(x, g, mod, mod_row, w, w_dt, l, tm, tn, emit_kv):
    m = x.shape[0]
    n = w.shape[2]
    assert P_N - P_K == tn and 2 * KV_W == tn
    out_specs = [pl.BlockSpec((tm, tn), lambda i, j: (i, j)),
                 pl.BlockSpec((tm, LANES), lambda i, j: (i, 0))]
    out_shape = [jax.ShapeDtypeStruct((m, n), BF16), jax.ShapeDtypeStruct((m, LANES), F32)]
    if emit_kv:
        out_specs.append(pl.BlockSpec((tm, tn), lambda i, j: (i, 0)))
        out_shape.append(jax.ShapeDtypeStruct((m, tn), F32))
    return pl.pallas_call(
        functools.partial(_in_proj_kernel, emit_kv=emit_kv),
        grid=(m // tm, n // tn),
        in_specs=[pl.BlockSpec((tm, D_MODEL), lambda i, j: (i, 0)),
                  pl.BlockSpec((1, D_MODEL), lambda i, j: (0, 0)),
                  pl.BlockSpec((1, N_MOD, D_MODEL), lambda i, j: (mod_row(i * tm), 0, 0)),
                  pl.BlockSpec((None, D_MODEL, tn), lambda i, j: (l, 0, j)),
                  pl.BlockSpec((None, D_MODEL, LANES), lambda i, j: (l, 0, 0))],
        out_specs=out_specs,
        out_shape=out_shape,
        scratch_shapes=[pltpu.VMEM((tm, D_MODEL), BF16)],
        compiler_params=_cparams(("parallel", "arbitrary")),
        name="in_proj",
    )(x, g.reshape(1, D_MODEL), mod, w, w_dt)


def _nm_swiglu_kernel(x_ref, g_ref, mod_ref, wg_ref, wu_ref, o_ref, h_ref):
    @pl.when(pl.program_id(1) == 0)
    def _():
        h_ref[...] = _norm_mod(x_ref, g_ref, mod_ref, 3, 4).astype(BF16)

    h = h_ref[...]
    a = jnp.dot(h, wg_ref[...], preferred_element_type=F32)
    b = jnp.dot(h, wu_ref[...], preferred_element_type=F32)
    o_ref[...] = (_silu(a) * b).astype(o_ref.dtype)


def norm_swiglu(x, g, mod, mod_row, w_gate_up, l, tm, tn):
    m = x.shape[0]
    nb = D_FF // tn
    return pl.pallas_call(
        _nm_swiglu_kernel,
        grid=(m // tm, nb),
        in_specs=[pl.BlockSpec((tm, D_MODEL), lambda i, j: (i, 0)),
                  pl.BlockSpec((1, D_MODEL), lambda i, j: (0, 0)),
                  pl.BlockSpec((1, N_MOD, D_MODEL), lambda i, j: (mod_row(i * tm), 0, 0)),
                  pl.BlockSpec((None, D_MODEL, tn), lambda i, j: (l, 0, j)),
                  pl.BlockSpec((None, D_MODEL, tn), lambda i, j: (l, 0, j + nb))],
        out_specs=pl.BlockSpec((tm, tn), lambda i, j: (i, j)),
        out_shape=jax.ShapeDtypeStruct((m, D_FF), BF16),
        scratch_shapes=[pltpu.VMEM((tm, D_MODEL), BF16)],
        compiler_params=_cparams(("parallel", "arbitrary")),
        name="norm_swiglu",
    )(x, g.reshape(1, D_MODEL), mod, w_gate_up, w_gate_up)


def _matmul_norm_res_kernel(a_ref, w_ref, x_ref, g_ref, mod_ref, o_ref, acca_ref, accb_ref,
                            *, gate_i, nt, ksteps, tm):
    i = pl.program_id(0)
    k = pl.program_id(1)
    rows = tm // ksteps
    r0 = pl.multiple_of(k * rows, rows)

    @pl.when((i == 0) & (k == 0))
    def _():
        acca_ref[...] = jnp.zeros_like(acca_ref)
        accb_ref[...] = jnp.zeros_like(accb_ref)

    def row(acc_cur, acc_prev):
        def epilogue():
            y = _rms(acc_prev[pl.ds(r0, rows), :], g_ref[...])
            o_ref[pl.ds(r0, rows), :] = (x_ref[pl.ds(r0, rows), :]
                                         + mod_ref[0, gate_i:gate_i + 1, :] * y)

        @pl.when(i < nt)
        def _():
            prod = jnp.dot(a_ref[...], w_ref[...], preferred_element_type=F32)
            acc_cur[...] = jnp.where(k == 0, prod, acc_cur[...] + prod)
            epilogue()

        @pl.when(i == nt)
        def _():
            epilogue()

    _by_parity(i, row, acca_ref, accb_ref)


def _matmul_norm_res_single_kernel(a_ref, w_ref, x_ref, g_ref, mod_ref, o_ref, *, gate_i):
    y = _rms(jnp.dot(a_ref[...], w_ref[...], preferred_element_type=F32), g_ref[...])
    o_ref[...] = x_ref[...] + mod_ref[0, gate_i:gate_i + 1, :] * y


def matmul_norm_residual(a, w, l, x, g, mod, mod_row, gate_i, tm, tk):
    m, kdim = a.shape
    nt = m // tm
    ksteps = kdim // tk
    if ksteps == 1:
        return pl.pallas_call(
            functools.partial(_matmul_norm_res_single_kernel, gate_i=gate_i),
            grid=(nt,),
            in_specs=[pl.BlockSpec((tm, kdim), lambda i: (i, 0)),
                      pl.BlockSpec((None, kdim, D_MODEL), lambda i: (l, 0, 0)),
                      pl.BlockSpec((tm, D_MODEL), lambda i: (i, 0)),
                      pl.BlockSpec((1, D_MODEL), lambda i: (0, 0)),
                      pl.BlockSpec((1, N_MOD, D_MODEL), lambda i: (mod_row(i * tm), 0, 0))],
            out_specs=pl.BlockSpec((tm, D_MODEL), lambda i: (i, 0)),
            out_shape=jax.ShapeDtypeStruct((m, D_MODEL), F32),
            compiler_params=_cparams(("parallel",)),
            name="matmul_norm_residual_1k",
        )(a, w, x, g.reshape(1, D_MODEL), mod)

    def kk(i, k):
        return jnp.where(i == nt, ksteps - 1, k)

    def prev(i):
        return jnp.maximum(i - 1, 0)

    return pl.pallas_call(
        functools.partial(_matmul_norm_res_kernel, gate_i=gate_i, nt=nt, ksteps=ksteps, tm=tm),
        grid=(nt + 1, ksteps),
        in_specs=[pl.BlockSpec((tm, tk), lambda i, k: (jnp.minimum(i, nt - 1), kk(i, k))),
                  pl.BlockSpec((None, tk, D_MODEL), lambda i, k: (l, kk(i, k), 0)),
                  pl.BlockSpec((tm, D_MODEL), lambda i, k: (prev(i), 0)),
                  pl.BlockSpec((1, D_MODEL), lambda i, k: (0, 0)),
                  pl.BlockSpec((1, N_MOD, D_MODEL), lambda i, k: (mod_row(prev(i) * tm), 0, 0))],
        out_specs=pl.BlockSpec((tm, D_MODEL), lambda i, k: (prev(i), 0)),
        out_shape=jax.ShapeDtypeStruct((m, D_MODEL), F32),
        scratch_shapes=[pltpu.VMEM((tm, D_MODEL), F32), pltpu.VMEM((tm, D_MODEL), F32)],
        compiler_params=_cparams(("arbitrary", "arbitrary")),
        name="matmul_norm_residual",
    )(a, w, x, g.reshape(1, D_MODEL), mod)


def _merge_kernel(ya_ref, ys_ref, yc_ref, wa_ref, ws_ref, wc_ref, ga_ref, gs_ref, gc_ref, o_ref):
    def branch(y_ref, w_ref, gate_ref):
        return jax.nn.sigmoid(gate_ref[...].astype(F32)) * jnp.dot(
            y_ref[...], w_ref[...], preferred_element_type=F32)

    o_ref[...] = (branch(ya_ref, wa_ref, ga_ref) + branch(ys_ref, ws_ref, gs_ref)
                  + branch(yc_ref, wc_ref, gc_ref)).astype(o_ref.dtype)


def merge_branches(y_att, y_ssd, y_sc, w_att, w_ssd, w_sc, l, proj, tm, tn):
    m = y_att.shape[0]
    gb = P_G // tn
    nb = D_MODEL // tn
    y_spec = pl.BlockSpec((tm, D_MODEL), lambda i, j: (i, 0))
    w_spec = pl.BlockSpec((None, D_MODEL, tn), lambda i, j: (l, 0, j))

    def g_spec(b):
        return pl.BlockSpec((tm, tn), lambda i, j: (i, gb + b * nb + j))

    return pl.pallas_call(
        _merge_kernel,
        grid=(m // tm, nb),
        in_specs=[y_spec, y_spec, y_spec, w_spec, w_spec, w_spec, g_spec(0), g_spec(1), g_spec(2)],
        out_specs=pl.BlockSpec((tm, tn), lambda i, j: (i, j)),
        out_shape=jax.ShapeDtypeStruct((m, D_MODEL), BF16),
        compiler_params=_cparams(("parallel", "parallel")),
        name="merge_branches",
    )(y_att, y_ssd, y_sc, w_att, w_ssd, w_sc, proj, proj, proj)


def _softmax_pv(s, sink, v):
    m = jnp.maximum(jnp.max(s, axis=-1, keepdims=True), sink)
    p = jnp.exp(s - m)
    den = jnp.sum(p, axis=-1, keepdims=True) + jnp.exp(sink - m)
    return jnp.dot(p.astype(BF16), v, preferred_element_type=F32) / den


def _ctx_attn_kernel(sink_ref, q_ref, k_ref, v_ref, o_ref):
    for g in range(N_KV_HEADS):
        k_g = k_ref[:, g * HEAD_DIM:(g + 1) * HEAD_DIM]
        v_g = v_ref[:, g * HEAD_DIM:(g + 1) * HEAD_DIM]
        for r in range(KV_REP):
            h = g * KV_REP + r
            q_h = q_ref[:, h * HEAD_DIM:(h + 1) * HEAD_DIM]
            s = lax.dot_general(q_h, k_g, (((1,), (1,)), ((), ())),
                                preferred_element_type=F32) * ATT_SCALE
            o = _softmax_pv(s, sink_ref[h], v_g)
            o_ref[:, h * HEAD_DIM:(h + 1) * HEAD_DIM] = o.astype(o_ref.dtype)


def context_attention(proj, sink, batch, seq):
    m = batch * seq
    return pl.pallas_call(
        _ctx_attn_kernel,
        grid=(batch,),
        in_specs=[pl.BlockSpec(memory_space=pltpu.SMEM),
                  pl.BlockSpec((seq, Q_W), lambda b: (b, P_Q // Q_W)),
                  pl.BlockSpec((seq, KV_W), lambda b: (b, P_K // KV_W)),
                  pl.BlockSpec((seq, KV_W), lambda b: (b, P_V // KV_W))],
        out_specs=pl.BlockSpec((seq, Q_W), lambda b: (b, 0)),
        out_shape=jax.ShapeDtypeStruct((m, Q_W), BF16),
        compiler_params=_cparams(("parallel",)),
        name="context_attention",
    )(sink, proj, proj, proj)


def _rope_tile(x, cos, sin_signed, even_lane):
    nxt = pltpu.roll(x, LANES - 1, axis=1)
    prv = pltpu.roll(x, 1, axis=1)
    return x * cos + jnp.where(even_lane, nxt, prv) * sin_signed


def _lat_attn_kernel(sink_ref, q_ref, k_ref, v_ref, ck_ref, cv_ref, cosq_ref, sinq_ref,
                     cosk_ref, sink_tab_ref, o_ref, krot_ref, *, seq, past):
    i = pl.program_id(1)
    band = ATT_BLOCK + 2 * WINDOW
    rows = 256

    @pl.when(i == 0)
    def _():
        def body(c, carry):
            r0 = pl.multiple_of(c * rows, rows)
            cos = cosk_ref[pl.ds(r0, rows), :]
            sin = sink_tab_ref[pl.ds(r0, rows), :]
            even = lax.broadcasted_iota(jnp.int32, (rows, LANES), 1) % 2 == 0
            for g in range(N_KV_HEADS):
                x = k_ref[pl.ds(r0, rows), g * HEAD_DIM:(g + 1) * HEAD_DIM].astype(F32)
                krot_ref[pl.ds(r0, rows), g * HEAD_DIM:(g + 1) * HEAD_DIM] = _rope_tile(
                    x, cos, sin, even).astype(BF16)
            return carry

        lax.fori_loop(0, seq // rows, body, 0)

    start = pl.multiple_of(jnp.clip(i * ATT_BLOCK - WINDOW, 0, seq - band), ATT_BLOCK)
    kb = krot_ref[pl.ds(start, band), :]
    vb = v_ref[pl.ds(start, band), :]
    ck = ck_ref[0]
    cv = cv_ref[0]
    stacked = ATT_STACK * ATT_BLOCK
    row = lax.broadcasted_iota(jnp.int32, (stacked, band), 0)
    col = lax.broadcasted_iota(jnp.int32, (stacked, band), 1)
    keep = jnp.abs(start + col - (i * ATT_BLOCK + row % ATT_BLOCK)) <= WINDOW
    head_of_row = lax.broadcasted_iota(jnp.int32, (stacked, 1), 0) // ATT_BLOCK
    cos_q = cosq_ref[...] * (ATT_SCALE * LOG2E)
    sin_q = sinq_ref[...] * (ATT_SCALE * LOG2E)
    even_q = lax.broadcasted_iota(jnp.int32, (ATT_BLOCK, LANES), 1) % 2 == 0
    for g in range(N_KV_HEADS):
        sl = slice(g * HEAD_DIM, (g + 1) * HEAD_DIM)
        kcat = jnp.concatenate([kb[:, sl], ck[:, sl]], axis=0)
        vcat = jnp.concatenate([vb[:, sl], cv[:, sl]], axis=0)
        for h0 in range(g * KV_REP, (g + 1) * KV_REP, ATT_STACK):
            q_parts = []
            sink2 = jnp.zeros((stacked, 1), F32)
            for r in range(ATT_STACK):
                hs = slice((h0 + r) * HEAD_DIM, (h0 + r + 1) * HEAD_DIM)
                q_parts.append(_rope_tile(q_ref[:, hs].astype(F32), cos_q, sin_q, even_q).astype(BF16))
                sink2 = jnp.where(head_of_row == r, sink_ref[h0 + r] * LOG2E, sink2)
            q_g = jnp.concatenate(q_parts, axis=0)
            s = lax.dot_general(q_g, kcat, (((1,), (1,)), ((), ())), preferred_element_type=F32)
            s_loc = jnp.where(keep, s[:, :band], -jnp.inf)
            s_ctx = s[:, band:]
            m = jnp.maximum(jnp.maximum(jnp.max(s_loc, axis=-1, keepdims=True),
                                        jnp.max(s_ctx, axis=-1, keepdims=True)), sink2)
            p_loc = jnp.exp2(s_loc - m)
            p_ctx = jnp.exp2(s_ctx - m)
            den = (jnp.sum(p_loc, axis=-1, keepdims=True) + jnp.sum(p_ctx, axis=-1, keepdims=True)
                   + jnp.exp2(sink2 - m))
            p = jnp.concatenate([p_loc.astype(BF16), p_ctx.astype(BF16)], axis=1)
            o = jnp.dot(p, vcat, preferred_element_type=F32) / den
            for r in range(ATT_STACK):
                o_ref[:, (h0 + r) * HEAD_DIM:(h0 + r + 1) * HEAD_DIM] = o[
                    r * ATT_BLOCK:(r + 1) * ATT_BLOCK].astype(o_ref.dtype)


def latent_attention(proj, ctx_k, ctx_v, sink, cos_t, sin_t, batch, seq):
    m = batch * seq
    nb = seq // ATT_BLOCK
    past = ctx_k.shape[1]
    return pl.pallas_call(
        functools.partial(_lat_attn_kernel, seq=seq, past=past),
        grid=(batch, nb),
        in_specs=[pl.BlockSpec(memory_space=pltpu.SMEM),
                  pl.BlockSpec((ATT_BLOCK, Q_W), lambda b, i: (b * nb + i, P_Q // Q_W)),
                  pl.BlockSpec((seq, KV_W), lambda b, i: (b, P_K // KV_W)),
                  pl.BlockSpec((seq, KV_W), lambda b, i: (b, P_V // KV_W)),
                  pl.BlockSpec((1, past, KV_W), lambda b, i: (b, 0, 0)),
                  pl.BlockSpec((1, past, KV_W), lambda b, i: (b, 0, 0)),
                  pl.BlockSpec((ATT_BLOCK, LANES), lambda b, i: (i, 0)),
                  pl.BlockSpec((ATT_BLOCK, LANES), lambda b, i: (i, 0)),
                  pl.BlockSpec((seq, LANES), lambda b, i: (0, 0)),
                  pl.BlockSpec((seq, LANES), lambda b, i: (0, 0))],
        out_specs=pl.BlockSpec((ATT_BLOCK, Q_W), lambda b, i: (b * nb + i, 0)),
        out_shape=jax.ShapeDtypeStruct((m, Q_W), BF16),
        scratch_shapes=[pltpu.VMEM((seq, KV_W), BF16)],
        compiler_params=_cparams(("parallel", "arbitrary")),
        name="latent_attention",
    )(sink, proj, proj, proj, ctx_k, ctx_v, cos_t, sin_t, cos_t, sin_t)


def rope_tables(seq):
    rows = seq // GRID_W
    row = jnp.repeat(jnp.arange(rows, dtype=F32), GRID_W)
    col = jnp.tile(jnp.arange(GRID_W, dtype=F32), rows)
    n_pairs = HEAD_DIM // 4
    inv = ROPE_THETA ** (-jnp.arange(n_pairs, dtype=F32) / n_pairs)
    ang = jnp.concatenate([row[:, None] * inv, col[:, None] * inv], axis=-1)
    ang = jnp.repeat(ang, 2, axis=-1)
    sign = jnp.where(jnp.arange(HEAD_DIM) % 2 == 0, -1.0, 1.0).astype(F32)
    return jnp.cos(ang), jnp.sin(ang) * sign


def _conv_rolled(prev, main, nxt, w_ref, taps, tile):
    half = (taps - 1) // 2
    sub = 8
    acc = w_ref[half:half + 1, :] * main
    for k in range(taps):
        if k != half:
            acc = acc + w_ref[k:k + 1, :] * pltpu.roll(main, (half - k) % tile, axis=0)

    def edge(window):
        out = None
        for k in range(taps):
            term = w_ref[k:k + 1, :] * window[sub - half + k:2 * sub - half + k]
            out = term if out is None else out + term
        return out

    top = edge(jnp.concatenate([prev, main[:2 * sub]], axis=0))
    bottom = edge(jnp.concatenate([main[tile - 2 * sub:], nxt], axis=0))
    return jnp.concatenate([top, acc[sub:tile - sub], bottom], axis=0)


def _ssm_conv_kernel(prev_ref, main_ref, next_ref, w_ref, b_ref, o_ref, *, nt, tile):
    t = pl.program_id(0) % nt
    prev = jnp.where(t == 0, 0.0, prev_ref[...].astype(F32)[HALO - 8:])
    nxt = jnp.where(t == nt - 1, 0.0, next_ref[...].astype(F32)[:8])
    y = _conv_rolled(prev, main_ref[...].astype(F32), nxt, w_ref, SSM_CONV, tile) + b_ref[...]
    o_ref[...] = _silu(y).astype(o_ref.dtype)


def _halo_specs(tile, width, col0, m):
    hb = tile // HALO
    last = m // HALO - 1
    prev = pl.BlockSpec((HALO, width), lambda i, j: (jnp.maximum(i * hb - 1, 0), col0 + j))
    main = pl.BlockSpec((tile, width), lambda i, j: (i, col0 + j))
    nxt = pl.BlockSpec((HALO, width), lambda i, j: (jnp.minimum((i + 1) * hb, last), col0 + j))
    return prev, main, nxt


def ssm_conv(proj, w, b, seq, tile):
    m = proj.shape[0]
    width = 1024
    nt = seq // tile
    prev, main, nxt = _halo_specs(tile, width, P_XS // width, m)
    return pl.pallas_call(
        functools.partial(_ssm_conv_kernel, nt=nt, tile=tile),
        grid=(m // tile, SSM_CONV_CH // width),
        in_specs=[prev, main, nxt,
                  pl.BlockSpec((SSM_CONV, width), lambda i, j: (0, j)),
                  pl.BlockSpec((1, width), lambda i, j: (0, j))],
        out_specs=pl.BlockSpec((tile, width), lambda i, j: (i, j)),
        out_shape=jax.ShapeDtypeStruct((m, SSM_CONV_CH), BF16),
        compiler_params=_cparams(("parallel", "parallel")),
        name="ssm_conv",
    )(proj, proj, proj, w, b.reshape(1, SSM_CONV_CH))


def _short_conv_kernel(b_ref, cp_ref, cm_ref, cn_ref, hp_ref, hm_ref, hn_ref, w_ref, o_ref,
                       *, nt, tile):
    t = pl.program_id(0) % nt

    def prod(c_ref, h_ref):
        return c_ref[...].astype(F32) * h_ref[...].astype(F32)

    prev = jnp.where(t == 0, 0.0, prod(cp_ref, hp_ref)[HALO - 8:])
    nxt = jnp.where(t == nt - 1, 0.0, prod(cn_ref, hn_ref)[:8])
    y = _conv_rolled(prev, prod(cm_ref, hm_ref), nxt, w_ref, SC_CONV, tile)
    o_ref[...] = (b_ref[...].astype(F32) * y).astype(o_ref.dtype)


def short_conv(proj, w, seq, tile):
    m = proj.shape[0]
    width = 1024
    nt = seq // tile
    cp, cm, cn = _halo_specs(tile, width, P_SCC // width, m)
    hp, hm, hn = _halo_specs(tile, width, P_SCH // width, m)
    return pl.pallas_call(
        functools.partial(_short_conv_kernel, nt=nt, tile=tile),
        grid=(m // tile, SC_WIDTH // width),
        in_specs=[pl.BlockSpec((tile, width), lambda i, j: (i, P_SCB // width + j)),
                  cp, cm, cn, hp, hm, hn,
                  pl.BlockSpec((SC_CONV, width), lambda i, j: (0, j))],
        out_specs=pl.BlockSpec((tile, width), lambda i, j: (i, j)),
        out_shape=jax.ShapeDtypeStruct((m, SC_WIDTH), BF16),
        compiler_params=_cparams(("parallel", "parallel")),
        name="short_conv",
    )(proj, proj, proj, proj, proj, proj, proj, w)


def _split3(a):
    a1 = a.astype(BF16)
    r1 = a - a1.astype(F32)
    a2 = r1.astype(BF16)
    a3 = (r1 - a2.astype(F32)).astype(BF16)
    return a1, a2, a3


def _softplus(x):
    return jnp.maximum(x, 0.0) + jnp.log1p(jnp.exp(-jnp.abs(x)))


def _ssd_kernel(*refs, backward, has_h0, emit_state):
    refs = list(refs)
    x_ref, b_ref, c_ref, dt_ref, dtb_ref, alog_ref, sel_ref = refs[:7]
    del refs[:7]
    h0_ref = refs.pop(0) if has_h0 else None
    if backward:
        yf_ref, z_ref, d_ref, gn_ref = refs[:4]
        del refs[:4]
    y_ref = refs.pop(0)
    hout_ref = refs.pop(0) if emit_state else None
    st_ref = refs.pop(0)
    acc_ref = refs.pop(0) if backward else y_ref
    q = SSM_CHUNK
    gw = (SSM_HEADS // SSM_GROUPS) * SSM_HEAD_DIM

    @pl.when(pl.program_id(1) == 0)
    def _():
        for g in range(SSM_GROUPS):
            if has_h0:
                st_ref[g] = h0_ref[0, 0, g * gw:(g + 1) * gw, :].T
            else:
                st_ref[g] = jnp.zeros((SSM_STATE, gw), F32)

    lo = SSM_HEADS if backward else 0
    dt = _softplus(dt_ref[...] + dtb_ref[...])
    a2 = dt * (-LOG2E * jnp.exp(alog_ref[...]))
    ii = lax.broadcasted_iota(jnp.int32, (q, q), 0)
    jj = lax.broadcasted_iota(jnp.int32, (q, q), 1)
    tri = (jj >= ii) if backward else (jj <= ii)
    tri_b = jnp.where(tri, 1.0, 0.0).astype(BF16)
    u2 = None
    for piece in _split3(a2):
        d = jnp.dot(tri_b, piece, preferred_element_type=F32)
        u2 = d if u2 is None else u2 + d
    v_t = (u2 - jnp.log2(dt)).T
    edge = 0 if backward else q - 1
    t_row = u2[edge:edge + 1, :]
    w_c = dt * jnp.exp2(t_row - u2)
    w_hi = w_c.astype(BF16)
    w_lo = (w_c - w_hi.astype(F32)).astype(BF16)
    w_exp = (jnp.dot(w_hi, sel_ref[...], preferred_element_type=F32)
             + jnp.dot(w_lo, sel_ref[...], preferred_element_type=F32))
    lane_lo = lax.broadcasted_iota(jnp.int32, (1, LANES), 1) < SSM_HEAD_DIM
    heads_per_group = SSM_HEADS // SSM_GROUPS
    pairs_per_group = heads_per_group // 2

    for g in range(SSM_GROUPS):
        b_g = b_ref[:, g * SSM_STATE:(g + 1) * SSM_STATE]
        c_g = c_ref[:, g * SSM_STATE:(g + 1) * SSM_STATE]
        cb = lax.dot_general(c_g, b_g, (((1,), (1,)), ((), ())), preferred_element_type=F32)
        h_t = st_ref[g]
        y_off = jnp.dot(c_g, h_t.astype(BF16), preferred_element_type=F32)
        xs_parts = []
        dec_parts = []
        for pr in range(pairs_per_group):
            h_a = g * heads_per_group + 2 * pr
            ca = lo + h_a
            cbk = ca + 1
            pc = (h_a // 2) * LANES
            x_b = x_ref[:, pc:pc + LANES]
            up = jnp.where(lane_lo, u2[:, ca:ca + 1], u2[:, cbk:cbk + 1])
            tp = jnp.where(lane_lo, t_row[:, ca:ca + 1], t_row[:, cbk:cbk + 1])
            ys = []
            for col in (ca, cbk):
                seg = u2[:, col:col + 1] - v_t[col:col + 1, :]
                decay_dt = jnp.exp2(jnp.where(tri, seg, -jnp.inf))
                ys.append(jnp.dot((cb * decay_dt).astype(BF16), x_b, preferred_element_type=F32))
            acc_ref[:, pc:pc + LANES] = (jnp.where(lane_lo, ys[0], ys[1])
                                         + jnp.exp2(up) * y_off[:, pr * LANES:(pr + 1) * LANES])
            xs_parts.append((x_b.astype(F32) * w_exp[:, pc:pc + LANES]).astype(BF16))
            dec_parts.append(jnp.exp2(tp))
        xs_g = jnp.concatenate(xs_parts, axis=1)
        dec_g = jnp.concatenate(dec_parts, axis=1)
        upd = lax.dot_general(b_g, xs_g, (((0,), (0,)), ((), ())), preferred_element_type=F32)
        st_ref[g] = dec_g * h_t + upd

    if emit_state:
        @pl.when(pl.program_id(1) == pl.num_programs(1) - 1)
        def _():
            for g in range(SSM_GROUPS):
                hout_ref[0, g * gw:(g + 1) * gw, :] = st_ref[g].T

    if backward:
        y = acc_ref[...] + yf_ref[...] + x_ref[...].astype(F32) * d_ref[...]
        y = y * _silu(z_ref[...].astype(F32))
        y_ref[...] = _rms(y, gn_ref[...]).astype(y_ref.dtype)


def _head_selector(backward):
    lane = jnp.arange(LANES)[:, None]
    head = jnp.arange(SSM_INNER)[None, :] // SSM_HEAD_DIM
    return (lane == head + (SSM_HEADS if backward else 0)).astype(BF16)


def ssd_scan(xbc, dt_raw, dt_bias, a_log, h0, batch, seq, emit_state, finish=None):
    backward = finish is not None
    m = batch * seq
    nc = seq // SSM_CHUNK
    gw = SSM_INNER // SSM_GROUPS
    hp = SSM_HEADS * SSM_HEAD_DIM

    def row(b, c):
        return b * nc + (nc - 1 - c if backward else c)

    rows = pl.BlockSpec((SSM_CHUNK, SSM_INNER), lambda b, c: (row(b, c), 0))
    vec = pl.BlockSpec((1, SSM_INNER), lambda b, c: (0, 0))
    in_specs = [rows,
                pl.BlockSpec((SSM_CHUNK, SSM_BC), lambda b, c: (row(b, c), SSM_INNER // SSM_BC)),
                pl.BlockSpec((SSM_CHUNK, SSM_BC), lambda b, c: (row(b, c), SSM_INNER // SSM_BC + 1)),
                pl.BlockSpec((SSM_CHUNK, LANES), lambda b, c: (row(b, c), 0)),
                pl.BlockSpec((1, LANES), lambda b, c: (0, 0)),
                pl.BlockSpec((1, LANES), lambda b, c: (0, 0)),
                pl.BlockSpec((LANES, SSM_INNER), lambda b, c: (0, 0))]
    args = [xbc, xbc, xbc, dt_raw, dt_bias, a_log, _head_selector(backward)]
    if h0 is not None:
        state, slab = h0
        in_specs.append(pl.BlockSpec((1, 1, hp, SSM_STATE), lambda b, c: (b, slab, 0, 0)))
        args.append(state)
    scratch = [pltpu.VMEM((SSM_GROUPS, SSM_STATE, gw), F32)]
    if backward:
        y_fwd, proj, d_exp, norm_g = finish
        in_specs += [rows,
                     pl.BlockSpec((SSM_CHUNK, SSM_INNER), lambda b, c: (row(b, c), P_Z // SSM_INNER)),
                     vec, vec]
        args += [y_fwd, proj, d_exp, norm_g.reshape(1, SSM_INNER)]
        scratch.append(pltpu.VMEM((SSM_CHUNK, SSM_INNER), F32))
    out_specs = [rows]
    out_shape = [jax.ShapeDtypeStruct((m, SSM_INNER), BF16 if backward else F32)]
    if emit_state:
        out_specs.append(pl.BlockSpec((1, hp, SSM_STATE), lambda b, c: (b, 0, 0)))
        out_shape.append(jax.ShapeDtypeStruct((batch, hp, SSM_STATE), F32))
    out = pl.pallas_call(
        functools.partial(_ssd_kernel, backward=backward, has_h0=h0 is not None, emit_state=emit_state),
        grid=(batch, nc),
        in_specs=in_specs,
        out_specs=out_specs,
        out_shape=out_shape,
        scratch_shapes=scratch,
        compiler_params=_cparams(("parallel", "arbitrary")),
        name="ssd_scan_bwd" if backward else "ssd_scan_fwd",
    )(*args)
    return (out[0], out[1]) if emit_state else (out[0], None)


def _pad_lanes(v):
    flat = v.reshape(1, 2 * SSM_HEADS).astype(F32)
    return jnp.pad(flat, ((0, 0), (0, LANES - 2 * SSM_HEADS)))


W_BLOCK = 512


def _w_main_tables():
    main, after, shifted = [], [], []

    def aligned(src, width):
        assert src % W_BLOCK == 0 and width % W_BLOCK == 0
        for b in range(width // W_BLOCK):
            main.append(src // W_BLOCK + b)
            after.append(0)
            shifted.append(0)

    def after_dt(width):
        assert _O_DT % W_BLOCK == 0 and _O_SCB - _O_DT == 2 * SSM_HEADS and width % W_BLOCK == 0
        for b in range(width // W_BLOCK):
            main.append(_O_DT // W_BLOCK + b)
            after.append((_O_DT + (b + 1) * W_BLOCK) // LANES)
            shifted.append(1)

    aligned(_O_Q, Q_W)
    aligned(_O_Z, SSM_INNER)
    after_dt(3 * SC_WIDTH + N_BRANCH * D_MODEL)
    aligned(_O_XS, SSM_INNER)
    aligned(_O_BM, SSM_BC)
    aligned(_O_CM, SSM_BC)
    aligned(_O_K, KV_W)
    aligned(_O_V, KV_W)
    assert len(main) == P_N // W_BLOCK
    as_i32 = lambda v: jnp.asarray(v, jnp.int32)
    return as_i32(main), as_i32(after), as_i32(shifted)


def _w_main_kernel(main_tbl, after_tbl, shift_tbl, main_ref, after_ref, o_ref):
    j = pl.program_id(1)
    off = 2 * SSM_HEADS

    @pl.when(shift_tbl[j] == 0)
    def _():
        o_ref[...] = main_ref[...].astype(BF16)

    @pl.when(shift_tbl[j] == 1)
    def _():
        both = jnp.concatenate([main_ref[...], after_ref[...]], axis=1)
        o_ref[...] = both[:, off:off + W_BLOCK].astype(BF16)


def reorder_w_in(w_in):
    main_tbl, after_tbl, shift_tbl = _w_main_tables()
    grid_spec = pltpu.PrefetchScalarGridSpec(
        num_scalar_prefetch=3,
        grid=(DEPTH, P_N // W_BLOCK),
        in_specs=[pl.BlockSpec((None, D_MODEL, W_BLOCK), lambda l, j, mt, at, st: (l, 0, mt[j])),
                  pl.BlockSpec((None, D_MODEL, LANES), lambda l, j, mt, at, st: (l, 0, at[j]))],
        out_specs=pl.BlockSpec((None, D_MODEL, W_BLOCK), lambda l, j, mt, at, st: (l, 0, j)))
    return pl.pallas_call(
        _w_main_kernel,
        grid_spec=grid_spec,
        out_shape=jax.ShapeDtypeStruct((DEPTH, D_MODEL, P_N), BF16),
        compiler_params=_cparams(("parallel", "parallel")),
        name="reorder_w_in",
    )(main_tbl, after_tbl, shift_tbl, w_in, w_in)


def _prepare_weights(w_in, w_att_out, w_ssd_out, w_sc_out, w_o, w_gate_up, w_down):
    def cols(o, n):
        return w_in[:, :, o:o + n]

    w_main = reorder_w_in(w_in)
    w_dt = jnp.pad(cols(_O_DT, 2 * SSM_HEADS), ((0, 0), (0, 0), (0, LANES - 2 * SSM_HEADS))).astype(BF16)
    return dict(w_main=w_main, w_dt=w_dt,
                w_att=w_att_out.astype(BF16), w_ssd=w_ssd_out.astype(BF16),
                w_sc=w_sc_out.astype(BF16), w_o=w_o.astype(BF16),
                w_gate_up=w_gate_up.astype(BF16), w_down=w_down.astype(BF16))


def _tile_plan(seq):
    return dict(
        in_proj=(1024, 1024),
        swiglu=(1024, 512),
        down=(512, D_FF // 4),
        w_o=(512, D_MODEL),
        merge=(512, 512),
        conv=min(seq, 512))


def _trunk_layer(x, mod, mod_row, l, lw, p, batch, seq, latent):
    plan = _tile_plan(seq)
    outs = in_proj(x, p['g_pre_mix'], mod, mod_row, lw['w_main'], lw['w_dt'], l, *plan['in_proj'],
                   emit_kv=latent is None)
    proj, dt_raw = outs[0], outs[1]
    if latent is None:
        y_att = context_attention(proj, p['sink'], batch, seq)
        h0_f = h0_b = None
    else:
        ctx_k, ctx_v, state, rope = latent
        y_att = latent_attention(proj, ctx_k.reshape(batch, -1, KV_W).astype(BF16),
                                 ctx_v.reshape(batch, -1, KV_W).astype(BF16), p['sink'],
                                 rope[0], rope[1], batch, seq)
        h0_f, h0_b = (state, 2 * l), (state, 2 * l + 1)
    xbc = ssm_conv(proj, p['ssm_conv_w'], p['ssm_conv_b'], seq, plan['conv'])
    dt_bias = _pad_lanes(p['ssm_dt_bias'])
    a_log = _pad_lanes(p['ssm_a_log'])
    d_exp = jnp.repeat(p['ssm_d'].astype(F32), SSM_HEAD_DIM).reshape(1, SSM_INNER)
    emit_state = latent is None
    y_f, h_f = ssd_scan(xbc, dt_raw, dt_bias, a_log, h0_f, batch, seq, emit_state)
    y_ssd, h_b = ssd_scan(xbc, dt_raw, dt_bias, a_log, h0_b, batch, seq, emit_state,
                          finish=(y_f, proj, d_exp, p['ssm_norm_g']))
    y_sc = short_conv(proj, p['sc_conv_w'], seq, plan['conv'])
    merged = merge_branches(y_att, y_ssd, y_sc, lw['w_att'], lw['w_ssd'], lw['w_sc'], l, proj,
                            *plan['merge'])
    x = matmul_norm_residual(merged, lw['w_o'], l, x, p['g_post_mix'], mod, mod_row, 2, *plan['w_o'])
    act = norm_swiglu(x, p['g_pre_ffn'], mod, mod_row, lw['w_gate_up'], l, *plan['swiglu'])
    x = matmul_norm_residual(act, lw['w_down'], l, x, p['g_post_ffn'], mod, mod_row, 5, *plan['down'])
    new_ctx = None
    if latent is None:
        kv = outs[2]
        k = kv[:, :KV_W].reshape(batch, seq, N_KV_HEADS, HEAD_DIM)
        v = kv[:, KV_W:].reshape(batch, seq, N_KV_HEADS, HEAD_DIM)
        new_ctx = (k, v, h_f, h_b)
    return x, new_ctx


def kernel(x_prompt, x_sample, cache_k, cache_v, state_ssm, c, c_ctx, w_mod, b_mod, g_pre_mix, w_in, sink, ssm_conv_w, ssm_conv_b, ssm_dt_bias, ssm_a_log, ssm_d, ssm_norm_g, sc_conv_w, w_att_out, w_ssd_out, w_sc_out, w_o, g_post_mix, g_pre_ffn, w_gate_up, w_down, g_post_ffn):
    batch, seq, _ = x_prompt.shape
    dec_batch, dec_seq, _ = x_sample.shape
    cond = jnp.concatenate([c_ctx[None, :], c, jnp.zeros((8 - 1 - dec_batch, D_MODEL), F32)], axis=0)
    mod = modulation(cond, w_mod, b_mod).reshape(DEPTH, 8, N_MOD, D_MODEL)
    rope = rope_tables(dec_seq)
    y_prompt = x_prompt.reshape(batch * seq, D_MODEL)
    y_sample = x_sample.reshape(dec_batch * dec_seq, D_MODEL)
    lw = _prepare_weights(w_in, w_att_out, w_ssd_out, w_sc_out, w_o, w_gate_up, w_down)
    state_in = state_ssm.reshape(dec_batch, 2 * DEPTH, SSM_HEADS * SSM_HEAD_DIM, SSM_STATE)
    ks_out, vs_out, ss_out = [], [], []
    for l in range(DEPTH):
        p = {'g_pre_mix': g_pre_mix[l], 'sink': sink[l], 'ssm_conv_w': ssm_conv_w[l],
             'ssm_conv_b': ssm_conv_b[l], 'ssm_dt_bias': ssm_dt_bias[l], 'ssm_a_log': ssm_a_log[l],
             'ssm_d': ssm_d[l], 'ssm_norm_g': ssm_norm_g[l], 'sc_conv_w': sc_conv_w[l],
             'g_post_mix': g_post_mix[l], 'g_pre_ffn': g_pre_ffn[l], 'g_post_ffn': g_post_ffn[l]}
        y_prompt, (k_l, v_l, h_f, h_b) = _trunk_layer(
            y_prompt, mod[l], lambda r: 0, l, lw, p, batch, seq, None)
        ks_out.append(k_l)
        vs_out.append(v_l)
        ss_out += [h_f, h_b]
        y_sample, _ = _trunk_layer(
            y_sample, mod[l], lambda r: 1 + r // dec_seq, l, lw, p, dec_batch, dec_seq,
            (cache_k[:, l], cache_v[:, l], state_in, rope))
    new_state = jnp.stack(ss_out, axis=1).reshape(
        batch, DEPTH, 2, SSM_HEADS, SSM_HEAD_DIM, SSM_STATE)
    return (y_prompt.reshape(batch, seq, D_MODEL), y_sample.reshape(dec_batch, dec_seq, D_MODEL),
            jnp.stack(ks_out, axis=1), jnp.stack(vs_out, axis=1), new_state)
```

```python
import functools

import jax
import jax.numpy as jnp
from jax import lax
from jax.experimental import pallas as pl
from jax.experimental.pallas import tpu as pltpu

F32 = jnp.float32
BF16 = jnp.bfloat16

D_MODEL = 2048
DEPTH = 2
GRID_W = 64
N_HEADS = 16
N_KV_HEADS = 4
KV_REP = N_HEADS // N_KV_HEADS
HEAD_DIM = 128
Q_W = N_HEADS * HEAD_DIM
KV_W = N_KV_HEADS * HEAD_DIM
WINDOW = 128
ATT_BLOCK = 128
ATT_SCALE = HEAD_DIM ** -0.5
ROPE_THETA = 10000.0
SSM_HEADS = 32
SSM_HEAD_DIM = 64
SSM_INNER = SSM_HEADS * SSM_HEAD_DIM
SSM_GROUPS = 4
SSM_STATE = 128
SSM_BC = SSM_GROUPS * SSM_STATE
SSM_CONV_CH = SSM_INNER + 2 * SSM_BC
SSM_CONV = 5
SSM_CHUNK = 128
SC_WIDTH = D_MODEL
SC_CONV = 3
N_BRANCH = 3
D_FF = ((8 * D_MODEL + 3 * 256 - 1) // (3 * 256)) * 256
N_MOD = 6
EPS = 1e-6
LOG2E = 1.4426950408889634

_O_Q = 0
_O_K = _O_Q + Q_W
_O_V = _O_K + KV_W
_O_Z = _O_V + KV_W
_O_XS = _O_Z + SSM_INNER
_O_BM = _O_XS + SSM_INNER
_O_CM = _O_BM + SSM_BC
_O_DT = _O_CM + SSM_BC
_O_SCB = _O_DT + 2 * SSM_HEADS
_O_SCC = _O_SCB + SC_WIDTH
_O_SCH = _O_SCC + SC_WIDTH
_O_G = _O_SCH + SC_WIDTH
D_IN = _O_G + N_BRANCH * D_MODEL

P_Q = 0
P_Z = 2048
P_SCB = 4096
P_SCC = 6144
P_SCH = 8192
P_G = 10240
P_XS = 16384
P_BM = 18432
P_CM = 18944
P_K = 19456
P_V = 19968
P_N = 20480

ATT_STACK = KV_REP
LANES = 128
HALO = 16
VMEM_LIMIT = 56 * 1024 * 1024


def _cparams(sem, **flags):
    return pltpu.CompilerParams(dimension_semantics=sem, vmem_limit_bytes=VMEM_LIMIT,
                                flags=flags or None)


def _rms(x, g):
    return x * lax.rsqrt(jnp.mean(x * x, axis=-1, keepdims=True) + EPS) * g


def _silu(x):
    return x * jax.nn.sigmoid(x)


def _mod_kernel(c_ref, w_ref, b_ref, o_ref):
    a = _silu(c_ref[...]).astype(BF16)
    o_ref[0] = jnp.dot(a, w_ref[0].astype(BF16), preferred_element_type=F32) + b_ref[0]


def modulation(cond, w_mod, b_mod):
    rows = cond.shape[0]
    n = N_MOD * D_MODEL
    tn = 1024
    return pl.pallas_call(
        _mod_kernel,
        grid=(DEPTH, n // tn),
        in_specs=[pl.BlockSpec((rows, D_MODEL), lambda l, j: (0, 0)),
                  pl.BlockSpec((1, D_MODEL, tn), lambda l, j: (l, 0, j)),
                  pl.BlockSpec((1, 1, tn), lambda l, j: (l, 0, j))],
        out_specs=pl.BlockSpec((1, rows, tn), lambda l, j: (l, 0, j)),
        out_shape=jax.ShapeDtypeStruct((DEPTH, rows, n), F32),
        compiler_params=_cparams(("parallel", "parallel")),
        name="modulation",
    )(cond, w_mod, b_mod.reshape(DEPTH, 1, n))


def _norm_mod(x_ref, g_ref, mod_ref, shift_i, scale_i):
    y = _rms(x_ref[...], g_ref[...])
    return y * (1.0 + mod_ref[0, scale_i:scale_i + 1, :]) + mod_ref[0, shift_i:shift_i + 1, :]


def _by_parity(i, fn, buf_a, buf_b):
    @pl.when(i % 2 == 0)
    def _():
        fn(buf_a, buf_b)

    @pl.when(i % 2 == 1)
    def _():
        fn(buf_b, buf_a)


def _in_proj_kernel(x_ref, g_ref, mod_ref, w_ref, wdt_ref, o_ref, odt_ref, *rest, emit_kv):
    okv_ref, h_ref = rest if emit_kv else (None, rest[0])

    @pl.when(pl.program_id(1) == 0)
    def _():
        h = _norm_mod(x_ref, g_ref, mod_ref, 0, 1).astype(BF16)
        h_ref[...] = h
        odt_ref[...] = jnp.dot(h, wdt_ref[...], preferred_element_type=F32)

    res = jnp.dot(h_ref[...], w_ref[...], preferred_element_type=F32)
    o_ref[...] = res.astype(o_ref.dtype)
    if emit_kv:
        @pl.when(pl.program_id(1) == pl.num_programs(1) - 1)
        def _():
            okv_ref[...] = res


def in_proj(x, g, mod, mod_row, w, w_dt, l, tm, tn, emit_kv):
    m = x.shape[0]
    n = w.shape[2]
    assert w.shape[0] == 1 and P_N - P_K == tn and 2 * KV_W == tn
    out_specs = [pl.BlockSpec((tm, tn), lambda i, j: (i, j)),
                 pl.BlockSpec((tm, LANES), lambda i, j: (i, 0))]
    out_shape = [jax.ShapeDtypeStruct((m, n), BF16), jax.ShapeDtypeStruct((m, LANES), F32)]
    if emit_kv:
        out_specs.append(pl.BlockSpec((tm, tn), lambda i, j: (i, 0)))
        out_shape.append(jax.ShapeDtypeStruct((m, tn), F32))
    return pl.pallas_call(
        functools.partial(_in_proj_kernel, emit_kv=emit_kv),
        grid=(m // tm, n // tn),
        in_specs=[pl.BlockSpec((tm, D_MODEL), lambda i, j: (i, 0)),
                  pl.BlockSpec((1, D_MODEL), lambda i, j: (0, 0)),
                  pl.BlockSpec((1, N_MOD, D_MODEL), lambda i, j: (mod_row(i * tm), 0, 0)),
                  pl.BlockSpec((None, D_MODEL, tn), lambda i, j: (0, 0, j)),
                  pl.BlockSpec((None, D_MODEL, LANES), lambda i, j: (l, 0, 0))],
        out_specs=out_specs,
        out_shape=out_shape,
        scratch_shapes=[pltpu.VMEM((tm, D_MODEL), BF16)],
        compiler_params=_cparams(("parallel", "arbitrary")),
        name="in_proj",
    )(x, g.reshape(1, D_MODEL), mod, w, w_dt)


def _nm_swiglu_kernel(x_ref, g_ref, mod_ref, wg_ref, wu_ref, o_ref, h_ref):
    @pl.when(pl.program_id(1) == 0)
    def _():
        h_ref[...] = _norm_mod(x_ref, g_ref, mod_ref, 3, 4).astype(BF16)

    h = h_ref[...]
    a = jnp.dot(h, wg_ref[...], preferred_element_type=F32)
    b = jnp.dot(h, wu_ref[...], preferred_element_type=F32)
    o_ref[...] = (_silu(a) * b).astype(o_ref.dtype)


def norm_swiglu(x, g, mod, mod_row, w_gate_up, l, tm, tn):
    m = x.shape[0]
    nb = D_FF // tn
    return pl.pallas_call(
        _nm_swiglu_kernel,
        grid=(m // tm, nb),
        in_specs=[pl.BlockSpec((tm, D_MODEL), lambda i, j: (i, 0)),
                  pl.BlockSpec((1, D_MODEL), lambda i, j: (0, 0)),
                  pl.BlockSpec((1, N_MOD, D_MODEL), lambda i, j: (mod_row(i * tm), 0, 0)),
                  pl.BlockSpec((None, D_MODEL, tn), lambda i, j: (l, 0, j)),
                  pl.BlockSpec((None, D_MODEL, tn), lambda i, j: (l, 0, j + nb))],
        out_specs=pl.BlockSpec((tm, tn), lambda i, j: (i, j)),
        out_shape=jax.ShapeDtypeStruct((m, D_FF), BF16),
        scratch_shapes=[pltpu.VMEM((tm, D_MODEL), BF16)],
        compiler_params=_cparams(("parallel", "arbitrary")),
        name="norm_swiglu",
    )(x, g.reshape(1, D_MODEL), mod, w_gate_up, w_gate_up)


def _matmul_norm_res_kernel(a_ref, w_ref, x_ref, g_ref, mod_ref, o_ref, acca_ref, accb_ref,
                            *, gate_i, nt, ksteps, tm):
    i = pl.program_id(0)
    k = pl.program_id(1)
    rows = tm // ksteps
    r0 = pl.multiple_of(k * rows, rows)

    @pl.when((i == 0) & (k == 0))
    def _():
        acca_ref[...] = jnp.zeros_like(acca_ref)
        accb_ref[...] = jnp.zeros_like(accb_ref)

    def row(acc_cur, acc_prev):
        def epilogue():
            y = _rms(acc_prev[pl.ds(r0, rows), :], g_ref[...])
            o_ref[pl.ds(r0, rows), :] = (x_ref[pl.ds(r0, rows), :]
                                         + mod_ref[0, gate_i:gate_i + 1, :] * y)

        @pl.when(i < nt)
        def _():
            prod = jnp.dot(a_ref[...], w_ref[...], preferred_element_type=F32)
            acc_cur[...] = jnp.where(k == 0, prod, acc_cur[...] + prod)
            epilogue()

        @pl.when(i == nt)
        def _():
            epilogue()

    _by_parity(i, row, acca_ref, accb_ref)


def _matmul_norm_res_single_kernel(a_ref, w_ref, x_ref, g_ref, mod_ref, o_ref, *, gate_i):
    y = _rms(jnp.dot(a_ref[...], w_ref[...], preferred_element_type=F32), g_ref[...])
    o_ref[...] = x_ref[...] + mod_ref[0, gate_i:gate_i + 1, :] * y


def matmul_norm_residual(a, w, l, x, g, mod, mod_row, gate_i, tm, tk):
    m, kdim = a.shape
    nt = m // tm
    ksteps = kdim // tk
    if ksteps == 1:
        return pl.pallas_call(
            functools.partial(_matmul_norm_res_single_kernel, gate_i=gate_i),
            grid=(nt,),
            in_specs=[pl.BlockSpec((tm, kdim), lambda i: (i, 0)),
                      pl.BlockSpec((None, kdim, D_MODEL), lambda i: (l, 0, 0)),
                      pl.BlockSpec((tm, D_MODEL), lambda i: (i, 0)),
                      pl.BlockSpec((1, D_MODEL), lambda i: (0, 0)),
                      pl.BlockSpec((1, N_MOD, D_MODEL), lambda i: (mod_row(i * tm), 0, 0))],
            out_specs=pl.BlockSpec((tm, D_MODEL), lambda i: (i, 0)),
            out_shape=jax.ShapeDtypeStruct((m, D_MODEL), F32),
            compiler_params=_cparams(("parallel",)),
            name="matmul_norm_residual_1k",
        )(a, w, x, g.reshape(1, D_MODEL), mod)

    def kk(i, k):
        return jnp.where(i == nt, ksteps - 1, k)

    def prev(i):
        return jnp.maximum(i - 1, 0)

    return pl.pallas_call(
        functools.partial(_matmul_norm_res_kernel, gate_i=gate_i, nt=nt, ksteps=ksteps, tm=tm),
        grid=(nt + 1, ksteps),
        in_specs=[pl.BlockSpec((tm, tk), lambda i, k: (jnp.minimum(i, nt - 1), kk(i, k))),
                  pl.BlockSpec((None, tk, D_MODEL), lambda i, k: (l, kk(i, k), 0)),
                  pl.BlockSpec((tm, D_MODEL), lambda i, k: (prev(i), 0)),
                  pl.BlockSpec((1, D_MODEL), lambda i, k: (0, 0)),
                  pl.BlockSpec((1, N_MOD, D_MODEL), lambda i, k: (mod_row(prev(i) * tm), 0, 0))],
        out_specs=pl.BlockSpec((tm, D_MODEL), lambda i, k: (prev(i), 0)),
        out_shape=jax.ShapeDtypeStruct((m, D_MODEL), F32),
        scratch_shapes=[pltpu.VMEM((tm, D_MODEL), F32), pltpu.VMEM((tm, D_MODEL), F32)],
        compiler_params=_cparams(("arbitrary", "arbitrary")),
        name="matmul_norm_residual",
    )(a, w, x, g.reshape(1, D_MODEL), mod)


def _merge_kernel(ya_ref, ys_ref, yc_ref, wa_ref, ws_ref, wc_ref, ga_ref, gs_ref, gc_ref, o_ref):
    def branch(y_ref, w_ref, gate_ref):
        return jax.nn.sigmoid(gate_ref[...].astype(F32)) * jnp.dot(
            y_ref[...], w_ref[...], preferred_element_type=F32)

    o_ref[...] = (branch(ya_ref, wa_ref, ga_ref) + branch(ys_ref, ws_ref, gs_ref)
                  + branch(yc_ref, wc_ref, gc_ref)).astype(o_ref.dtype)


def merge_branches(y_att, y_ssd, y_sc, w_att, w_ssd, w_sc, l, proj, tm, tn):
    m = y_att.shape[0]
    gb = P_G // tn
    nb = D_MODEL // tn
    y_spec = pl.BlockSpec((tm, D_MODEL), lambda i, j: (i, 0))
    w_spec = pl.BlockSpec((None, D_MODEL, tn), lambda i, j: (l, 0, j))

    def g_spec(b):
        return pl.BlockSpec((tm, tn), lambda i, j: (i, gb + b * nb + j))

    return pl.pallas_call(
        _merge_kernel,
        grid=(m // tm, nb),
        in_specs=[y_spec, y_spec, y_spec, w_spec, w_spec, w_spec, g_spec(0), g_spec(1), g_spec(2)],
        out_specs=pl.BlockSpec((tm, tn), lambda i, j: (i, j)),
        out_shape=jax.ShapeDtypeStruct((m, D_MODEL), BF16),
        compiler_params=_cparams(("parallel", "parallel")),
        name="merge_branches",
    )(y_att, y_ssd, y_sc, w_att, w_ssd, w_sc, proj, proj, proj)


def _softmax_pv(s, sink, v):
    m = jnp.maximum(jnp.max(s, axis=-1, keepdims=True), sink)
    p = jnp.exp(s - m)
    den = jnp.sum(p, axis=-1, keepdims=True) + jnp.exp(sink - m)
    return jnp.dot(p.astype(BF16), v, preferred_element_type=F32) / den


def _ctx_attn_kernel(sink_ref, q_ref, k_ref, v_ref, o_ref):
    for g in range(N_KV_HEADS):
        k_g = k_ref[:, g * HEAD_DIM:(g + 1) * HEAD_DIM]
        v_g = v_ref[:, g * HEAD_DIM:(g + 1) * HEAD_DIM]
        for r in range(KV_REP):
            h = g * KV_REP + r
            q_h = q_ref[:, h * HEAD_DIM:(h + 1) * HEAD_DIM]
            s = lax.dot_general(q_h, k_g, (((1,), (1,)), ((), ())),
                                preferred_element_type=F32) * ATT_SCALE
            o = _softmax_pv(s, sink_ref[h], v_g)
            o_ref[:, h * HEAD_DIM:(h + 1) * HEAD_DIM] = o.astype(o_ref.dtype)


def context_attention(proj, sink, batch, seq):
    m = batch * seq
    return pl.pallas_call(
        _ctx_attn_kernel,
        grid=(batch,),
        in_specs=[pl.BlockSpec(memory_space=pltpu.SMEM),
                  pl.BlockSpec((seq, Q_W), lambda b: (b, P_Q // Q_W)),
                  pl.BlockSpec((seq, KV_W), lambda b: (b, P_K // KV_W)),
                  pl.BlockSpec((seq, KV_W), lambda b: (b, P_V // KV_W))],
        out_specs=pl.BlockSpec((seq, Q_W), lambda b: (b, 0)),
        out_shape=jax.ShapeDtypeStruct((m, Q_W), BF16),
        compiler_params=_cparams(("parallel",)),
        name="context_attention",
    )(sink, proj, proj, proj)


def _rope_tile(x, cos, sin_signed, even_lane):
    nxt = pltpu.roll(x, LANES - 1, axis=1)
    prv = pltpu.roll(x, 1, axis=1)
    return x * cos + jnp.where(even_lane, nxt, prv) * sin_signed


def _lat_attn_kernel(sink_ref, q_ref, k_ref, v_ref, ck_ref, cv_ref, cosq_ref, sinq_ref,
                     cosk_ref, sink_tab_ref, o_ref, krot_ref, *, seq, past):
    i = pl.program_id(1)
    band = ATT_BLOCK + 2 * WINDOW
    rows = 256

    @pl.when(i == 0)
    def _():
        def body(c, carry):
            r0 = pl.multiple_of(c * rows, rows)
            cos = cosk_ref[pl.ds(r0, rows), :]
            sin = sink_tab_ref[pl.ds(r0, rows), :]
            even = lax.broadcasted_iota(jnp.int32, (rows, LANES), 1) % 2 == 0
            for g in range(N_KV_HEADS):
                x = k_ref[pl.ds(r0, rows), g * HEAD_DIM:(g + 1) * HEAD_DIM].astype(F32)
                krot_ref[pl.ds(r0, rows), g * HEAD_DIM:(g + 1) * HEAD_DIM] = _rope_tile(
                    x, cos, sin, even).astype(BF16)
            return carry

        lax.fori_loop(0, seq // rows, body, 0)

    start = pl.multiple_of(jnp.clip(i * ATT_BLOCK - WINDOW, 0, seq - band), ATT_BLOCK)
    kb = krot_ref[pl.ds(start, band), :]
    vb = v_ref[pl.ds(start, band), :]
    ck = ck_ref[0]
    cv = cv_ref[0]
    stacked = ATT_STACK * ATT_BLOCK
    row = lax.broadcasted_iota(jnp.int32, (stacked, band), 0)
    col = lax.broadcasted_iota(jnp.int32, (stacked, band), 1)
    keep = jnp.abs(start + col - (i * ATT_BLOCK + row % ATT_BLOCK)) <= WINDOW
    head_of_row = lax.broadcasted_iota(jnp.int32, (stacked, 1), 0) // ATT_BLOCK
    cos_q = cosq_ref[...] * (ATT_SCALE * LOG2E)
    sin_q = sinq_ref[...] * (ATT_SCALE * LOG2E)
    even_q = lax.broadcasted_iota(jnp.int32, (ATT_BLOCK, LANES), 1) % 2 == 0
    for g in range(N_KV_HEADS):
        sl = slice(g * HEAD_DIM, (g + 1) * HEAD_DIM)
        kcat = jnp.concatenate([kb[:, sl], ck[:, sl]], axis=0)
        vcat = jnp.concatenate([vb[:, sl], cv[:, sl]], axis=0)
        for h0 in range(g * KV_REP, (g + 1) * KV_REP, ATT_STACK):
            q_parts = []
            sink2 = jnp.zeros((stacked, 1), F32)
            for r in range(ATT_STACK):
                hs = slice((h0 + r) * HEAD_DIM, (h0 + r + 1) * HEAD_DIM)
                q_parts.append(_rope_tile(q_ref[:, hs].astype(F32), cos_q, sin_q, even_q).astype(BF16))
                sink2 = jnp.where(head_of_row == r, sink_ref[h0 + r] * LOG2E, sink2)
            q_g = jnp.concatenate(q_parts, axis=0)
            s = lax.dot_general(q_g, kcat, (((1,), (1,)), ((), ())), preferred_element_type=F32)
            s_loc = jnp.where(keep, s[:, :band], -jnp.inf)
            s_ctx = s[:, band:]
            m = jnp.maximum(jnp.maximum(jnp.max(s_loc, axis=-1, keepdims=True),
                                        jnp.max(s_ctx, axis=-1, keepdims=True)), sink2)
            p_loc = jnp.exp2(s_loc - m)
            p_ctx = jnp.exp2(s_ctx - m)
            den = (jnp.sum(p_loc, axis=-1, keepdims=True) + jnp.sum(p_ctx, axis=-1, keepdims=True)
                   + jnp.exp2(sink2 - m))
            p = jnp.concatenate([p_loc.astype(BF16), p_ctx.astype(BF16)], axis=1)
            o = jnp.dot(p, vcat, preferred_element_type=F32) / den
            for r in range(ATT_STACK):
                o_ref[:, (h0 + r) * HEAD_DIM:(h0 + r + 1) * HEAD_DIM] = o[
                    r * ATT_BLOCK:(r + 1) * ATT_BLOCK].astype(o_ref.dtype)


def latent_attention(proj, ctx_k, ctx_v, sink, cos_t, sin_t, batch, seq):
    m = batch * seq
    nb = seq // ATT_BLOCK
    past = ctx_k.shape[1]
    return pl.pallas_call(
        functools.partial(_lat_attn_kernel, seq=seq, past=past),
        grid=(batch, nb),
        in_specs=[pl.BlockSpec(memory_space=pltpu.SMEM),
                  pl.BlockSpec((ATT_BLOCK, Q_W), lambda b, i: (b * nb + i, P_Q // Q_W)),
                  pl.BlockSpec((seq, KV_W), lambda b, i: (b, P_K // KV_W)),
                  pl.BlockSpec((seq, KV_W), lambda b, i: (b, P_V // KV_W)),
                  pl.BlockSpec((1, past, KV_W), lambda b, i: (b, 0, 0)),
                  pl.BlockSpec((1, past, KV_W), lambda b, i: (b, 0, 0)),
                  pl.BlockSpec((ATT_BLOCK, LANES), lambda b, i: (i, 0)),
                  pl.BlockSpec((ATT_BLOCK, LANES), lambda b, i: (i, 0)),
                  pl.BlockSpec((seq, LANES), lambda b, i: (0, 0)),
                  pl.BlockSpec((seq, LANES), lambda b, i: (0, 0))],
        out_specs=pl.BlockSpec((ATT_BLOCK, Q_W), lambda b, i: (b * nb + i, 0)),
        out_shape=jax.ShapeDtypeStruct((m, Q_W), BF16),
        scratch_shapes=[pltpu.VMEM((seq, KV_W), BF16)],
        compiler_params=_cparams(("parallel", "arbitrary")),
        name="latent_attention",
    )(sink, proj, proj, proj, ctx_k, ctx_v, cos_t, sin_t, cos_t, sin_t)


def rope_tables(seq):
    rows = seq // GRID_W
    row = jnp.repeat(jnp.arange(rows, dtype=F32), GRID_W)
    col = jnp.tile(jnp.arange(GRID_W, dtype=F32), rows)
    n_pairs = HEAD_DIM // 4
    inv = ROPE_THETA ** (-jnp.arange(n_pairs, dtype=F32) / n_pairs)
    ang = jnp.concatenate([row[:, None] * inv, col[:, None] * inv], axis=-1)
    ang = jnp.repeat(ang, 2, axis=-1)
    sign = jnp.where(jnp.arange(HEAD_DIM) % 2 == 0, -1.0, 1.0).astype(F32)
    return jnp.cos(ang), jnp.sin(ang) * sign


def _conv_rolled(prev, main, nxt, w_ref, taps, tile):
    half = (taps - 1) // 2
    sub = 8
    acc = w_ref[half:half + 1, :] * main
    for k in range(taps):
        if k != half:
            acc = acc + w_ref[k:k + 1, :] * pltpu.roll(main, (half - k) % tile, axis=0)

    def edge(window):
        out = None
        for k in range(taps):
            term = w_ref[k:k + 1, :] * window[sub - half + k:2 * sub - half + k]
            out = term if out is None else out + term
        return out

    top = edge(jnp.concatenate([prev, main[:2 * sub]], axis=0))
    bottom = edge(jnp.concatenate([main[tile - 2 * sub:], nxt], axis=0))
    return jnp.concatenate([top, acc[sub:tile - sub], bottom], axis=0)


def _ssm_conv_kernel(prev_ref, main_ref, next_ref, w_ref, b_ref, o_ref, *, nt, tile):
    t = pl.program_id(0) % nt
    prev = jnp.where(t == 0, 0.0, prev_ref[...].astype(F32)[HALO - 8:])
    nxt = jnp.where(t == nt - 1, 0.0, next_ref[...].astype(F32)[:8])
    y = _conv_rolled(prev, main_ref[...].astype(F32), nxt, w_ref, SSM_CONV, tile) + b_ref[...]
    o_ref[...] = _silu(y).astype(o_ref.dtype)


def _halo_specs(tile, width, col0, m):
    hb = tile // HALO
    last = m // HALO - 1
    prev = pl.BlockSpec((HALO, width), lambda i, j: (jnp.maximum(i * hb - 1, 0), col0 + j))
    main = pl.BlockSpec((tile, width), lambda i, j: (i, col0 + j))
    nxt = pl.BlockSpec((HALO, width), lambda i, j: (jnp.minimum((i + 1) * hb, last), col0 + j))
    return prev, main, nxt


def ssm_conv(proj, w, b, seq, tile):
    m = proj.shape[0]
    width = 1024
    nt = seq // tile
    prev, main, nxt = _halo_specs(tile, width, P_XS // width, m)
    return pl.pallas_call(
        functools.partial(_ssm_conv_kernel, nt=nt, tile=tile),
        grid=(m // tile, SSM_CONV_CH // width),
        in_specs=[prev, main, nxt,
                  pl.BlockSpec((SSM_CONV, width), lambda i, j: (0, j)),
                  pl.BlockSpec((1, width), lambda i, j: (0, j))],
        out_specs=pl.BlockSpec((tile, width), lambda i, j: (i, j)),
        out_shape=jax.ShapeDtypeStruct((m, SSM_CONV_CH), BF16),
        compiler_params=_cparams(("parallel", "parallel")),
        name="ssm_conv",
    )(proj, proj, proj, w, b.reshape(1, SSM_CONV_CH))


def _short_conv_kernel(b_ref, cp_ref, cm_ref, cn_ref, hp_ref, hm_ref, hn_ref, w_ref, o_ref,
                       *, nt, tile):
    t = pl.program_id(0) % nt

    def prod(c_ref, h_ref):
        return c_ref[...].astype(F32) * h_ref[...].astype(F32)

    prev = jnp.where(t == 0, 0.0, prod(cp_ref, hp_ref)[HALO - 8:])
    nxt = jnp.where(t == nt - 1, 0.0, prod(cn_ref, hn_ref)[:8])
    y = _conv_rolled(prev, prod(cm_ref, hm_ref), nxt, w_ref, SC_CONV, tile)
    o_ref[...] = (b_ref[...].astype(F32) * y).astype(o_ref.dtype)


def short_conv(proj, w, seq, tile):
    m = proj.shape[0]
    width = 1024
    nt = seq // tile
    cp, cm, cn = _halo_specs(tile, width, P_SCC // width, m)
    hp, hm, hn = _halo_specs(tile, width, P_SCH // width, m)
    return pl.pallas_call(
        functools.partial(_short_conv_kernel, nt=nt, tile=tile),
        grid=(m // tile, SC_WIDTH // width),
        in_specs=[pl.BlockSpec((tile, width), lambda i, j: (i, P_SCB // width + j)),
                  cp, cm, cn, hp, hm, hn,
                  pl.BlockSpec((SC_CONV, width), lambda i, j: (0, j))],
        out_specs=pl.BlockSpec((tile, width), lambda i, j: (i, j)),
        out_shape=jax.ShapeDtypeStruct((m, SC_WIDTH), BF16),
        compiler_params=_cparams(("parallel", "parallel")),
        name="short_conv",
    )(proj, proj, proj, proj, proj, proj, proj, w)


def _split3(a):
    a1 = a.astype(BF16)
    r1 = a - a1.astype(F32)
    a2 = r1.astype(BF16)
    a3 = (r1 - a2.astype(F32)).astype(BF16)
    return a1, a2, a3


def _softplus(x):
    return jnp.maximum(x, 0.0) + jnp.log1p(jnp.exp(-jnp.abs(x)))


def _ssd_kernel(*refs, backward, has_h0, emit_state):
    refs = list(refs)
    x_ref, b_ref, c_ref, dt_ref, dtb_ref, alog_ref, sel_ref = refs[:7]
    del refs[:7]
    h0_ref = refs.pop(0) if has_h0 else None
    if backward:
        yf_ref, z_ref, d_ref, gn_ref = refs[:4]
        del refs[:4]
    y_ref = refs.pop(0)
    hout_ref = refs.pop(0) if emit_state else None
    st_ref = refs.pop(0)
    acc_ref = refs.pop(0) if backward else y_ref
    q = SSM_CHUNK
    gw = (SSM_HEADS // SSM_GROUPS) * SSM_HEAD_DIM

    @pl.when(pl.program_id(1) == 0)
    def _():
        for g in range(SSM_GROUPS):
            if has_h0:
                st_ref[g] = h0_ref[0, 0, g * gw:(g + 1) * gw, :].T
            else:
                st_ref[g] = jnp.zeros((SSM_STATE, gw), F32)

    lo = SSM_HEADS if backward else 0
    dt = _softplus(dt_ref[...] + dtb_ref[...])
    a2 = dt * (-LOG2E * jnp.exp(alog_ref[...]))
    ii = lax.broadcasted_iota(jnp.int32, (q, q), 0)
    jj = lax.broadcasted_iota(jnp.int32, (q, q), 1)
    tri = (jj >= ii) if backward else (jj <= ii)
    tri_b = jnp.where(tri, 1.0, 0.0).astype(BF16)
    u2 = None
    for piece in _split3(a2):
        d = jnp.dot(tri_b, piece, preferred_element_type=F32)
        u2 = d if u2 is None else u2 + d
    v_t = (u2 - jnp.log2(dt)).T
    edge = 0 if backward else q - 1
    t_row = u2[edge:edge + 1, :]
    w_c = dt * jnp.exp2(t_row - u2)
    w_hi = w_c.astype(BF16)
    w_lo = (w_c - w_hi.astype(F32)).astype(BF16)
    w_exp = (jnp.dot(w_hi, sel_ref[...], preferred_element_type=F32)
             + jnp.dot(w_lo, sel_ref[...], preferred_element_type=F32))
    lane_lo = lax.broadcasted_iota(jnp.int32, (1, LANES), 1) < SSM_HEAD_DIM
    heads_per_group = SSM_HEADS // SSM_GROUPS
    pairs_per_group = heads_per_group // 2

    for g in range(SSM_GROUPS):
        b_g = b_ref[:, g * SSM_STATE:(g + 1) * SSM_STATE]
        c_g = c_ref[:, g * SSM_STATE:(g + 1) * SSM_STATE]
        cb = lax.dot_general(c_g, b_g, (((1,), (1,)), ((), ())), preferred_element_type=F32)
        h_t = st_ref[g]
        y_off = jnp.dot(c_g, h_t.astype(BF16), preferred_element_type=F32)
        xs_parts = []
        dec_parts = []
        for pr in range(pairs_per_group):
            h_a = g * heads_per_group + 2 * pr
            ca = lo + h_a
            cbk = ca + 1
            pc = (h_a // 2) * LANES
            x_b = x_ref[:, pc:pc + LANES]
            up = jnp.where(lane_lo, u2[:, ca:ca + 1], u2[:, cbk:cbk + 1])
            tp = jnp.where(lane_lo, t_row[:, ca:ca + 1], t_row[:, cbk:cbk + 1])
            ys = []
            for col in (ca, cbk):
                seg = u2[:, col:col + 1] - v_t[col:col + 1, :]
                decay_dt = jnp.exp2(jnp.where(tri, seg, -jnp.inf))
                ys.append(jnp.dot((cb * decay_dt).astype(BF16), x_b, preferred_element_type=F32))
            acc_ref[:, pc:pc + LANES] = (jnp.where(lane_lo, ys[0], ys[1])
                                         + jnp.exp2(up) * y_off[:, pr * LANES:(pr + 1) * LANES])
            xs_parts.append((x_b.astype(F32) * w_exp[:, pc:pc + LANES]).astype(BF16))
            dec_parts.append(jnp.exp2(tp))
        xs_g = jnp.concatenate(xs_parts, axis=1)
        dec_g = jnp.concatenate(dec_parts, axis=1)
        upd = lax.dot_general(b_g, xs_g, (((0,), (0,)), ((), ())), preferred_element_type=F32)
        st_ref[g] = dec_g * h_t + upd

    if emit_state:
        @pl.when(pl.program_id(1) == pl.num_programs(1) - 1)
        def _():
            for g in range(SSM_GROUPS):
                hout_ref[0, g * gw:(g + 1) * gw, :] = st_ref[g].T

    if backward:
        y = acc_ref[...] + yf_ref[...] + x_ref[...].astype(F32) * d_ref[...]
        y = y * _silu(z_ref[...].astype(F32))
        y_ref[...] = _rms(y, gn_ref[...]).astype(y_ref.dtype)


def _head_selector(backward):
    lane = jnp.arange(LANES)[:, None]
    head = jnp.arange(SSM_INNER)[None, :] // SSM_HEAD_DIM
    return (lane == head + (SSM_HEADS if backward else 0)).astype(BF16)


def ssd_scan(xbc, dt_raw, dt_bias, a_log, h0, batch, seq, emit_state, finish=None):
    backward = finish is not None
    m = batch * seq
    nc = seq // SSM_CHUNK
    gw = SSM_INNER // SSM_GROUPS
    hp = SSM_HEADS * SSM_HEAD_DIM

    def row(b, c):
        return b * nc + (nc - 1 - c if backward else c)

    rows = pl.BlockSpec((SSM_CHUNK, SSM_INNER), lambda b, c: (row(b, c), 0))
    vec = pl.BlockSpec((1, SSM_INNER), lambda b, c: (0, 0))
    in_specs = [rows,
                pl.BlockSpec((SSM_CHUNK, SSM_BC), lambda b, c: (row(b, c), SSM_INNER // SSM_BC)),
                pl.BlockSpec((SSM_CHUNK, SSM_BC), lambda b, c: (row(b, c), SSM_INNER // SSM_BC + 1)),
                pl.BlockSpec((SSM_CHUNK, LANES), lambda b, c: (row(b, c), 0)),
                pl.BlockSpec((1, LANES), lambda b, c: (0, 0)),
                pl.BlockSpec((1, LANES), lambda b, c: (0, 0)),
                pl.BlockSpec((LANES, SSM_INNER), lambda b, c: (0, 0))]
    args = [xbc, xbc, xbc, dt_raw, dt_bias, a_log, _head_selector(backward)]
    if h0 is not None:
        state, slab = h0
        in_specs.append(pl.BlockSpec((1, 1, hp, SSM_STATE), lambda b, c: (b, slab, 0, 0)))
        args.append(state)
    scratch = [pltpu.VMEM((SSM_GROUPS, SSM_STATE, gw), F32)]
    if backward:
        y_fwd, proj, d_exp, norm_g = finish
        in_specs += [rows,
                     pl.BlockSpec((SSM_CHUNK, SSM_INNER), lambda b, c: (row(b, c), P_Z // SSM_INNER)),
                     vec, vec]
        args += [y_fwd, proj, d_exp, norm_g.reshape(1, SSM_INNER)]
        scratch.append(pltpu.VMEM((SSM_CHUNK, SSM_INNER), F32))
    out_specs = [rows]
    out_shape = [jax.ShapeDtypeStruct((m, SSM_INNER), BF16 if backward else F32)]
    if emit_state:
        out_specs.append(pl.BlockSpec((1, hp, SSM_STATE), lambda b, c: (b, 0, 0)))
        out_shape.append(jax.ShapeDtypeStruct((batch, hp, SSM_STATE), F32))
    out = pl.pallas_call(
        functools.partial(_ssd_kernel, backward=backward, has_h0=h0 is not None, emit_state=emit_state),
        grid=(batch, nc),
        in_specs=in_specs,
        out_specs=out_specs,
        out_shape=out_shape,
        scratch_shapes=scratch,
        compiler_params=_cparams(("parallel", "arbitrary")),
        name="ssd_scan_bwd" if backward else "ssd_scan_fwd",
    )(*args)
    return (out[0], out[1]) if emit_state else (out[0], None)


def _pad_lanes(v):
    flat = v.reshape(1, 2 * SSM_HEADS).astype(F32)
    return jnp.pad(flat, ((0, 0), (0, LANES - 2 * SSM_HEADS)))


def _prepare_weights(w_in, w_att_out, w_ssd_out, w_sc_out, w_o, w_gate_up, w_down):
    def cols(o, n):
        return w_in[:, :, o:o + n]

    def reordered(l):
        w = w_in[l]
        parts = [(_O_Q, Q_W), (_O_Z, SSM_INNER), (_O_SCB, SC_WIDTH), (_O_SCC, SC_WIDTH),
                 (_O_SCH, SC_WIDTH), (_O_G, N_BRANCH * D_MODEL), (_O_XS, SSM_INNER),
                 (_O_BM, SSM_BC), (_O_CM, SSM_BC), (_O_K, KV_W), (_O_V, KV_W)]
        return jnp.concatenate([w[:, o:o + n] for o, n in parts], axis=1).astype(BF16)[None]

    w_main = [reordered(l) for l in range(DEPTH)]
    w_dt = jnp.pad(cols(_O_DT, 2 * SSM_HEADS), ((0, 0), (0, 0), (0, LANES - 2 * SSM_HEADS))).astype(BF16)
    return dict(w_main=w_main, w_dt=w_dt,
                w_att=w_att_out.astype(BF16), w_ssd=w_ssd_out.astype(BF16),
                w_sc=w_sc_out.astype(BF16), w_o=w_o.astype(BF16),
                w_gate_up=w_gate_up.astype(BF16), w_down=w_down.astype(BF16))


def _tile_plan(seq):
    return dict(
        in_proj=(1024, 1024),
        swiglu=(1024, 512),
        down=(512, D_FF // 4),
        w_o=(512, D_MODEL),
        merge=(512, 512),
        conv=min(seq, 512))


def _trunk_layer(x, mod, mod_row, l, lw, p, batch, seq, latent):
    plan = _tile_plan(seq)
    outs = in_proj(x, p['g_pre_mix'], mod, mod_row, lw['w_main'][l], lw['w_dt'], l, *plan['in_proj'],
                   emit_kv=latent is None)
    proj, dt_raw = outs[0], outs[1]
    if latent is None:
        y_att = context_attention(proj, p['sink'], batch, seq)
        h0_f = h0_b = None
    else:
        ctx_k, ctx_v, state, rope = latent
        y_att = latent_attention(proj, ctx_k.reshape(batch, -1, KV_W).astype(BF16),
                                 ctx_v.reshape(batch, -1, KV_W).astype(BF16), p['sink'],
                                 rope[0], rope[1], batch, seq)
        h0_f, h0_b = (state, 2 * l), (state, 2 * l + 1)
    xbc = ssm_conv(proj, p['ssm_conv_w'], p['ssm_conv_b'], seq, plan['conv'])
    dt_bias = _pad_lanes(p['ssm_dt_bias'])
    a_log = _pad_lanes(p['ssm_a_log'])
    d_exp = jnp.repeat(p['ssm_d'].astype(F32), SSM_HEAD_DIM).reshape(1, SSM_INNER)
    emit_state = latent is None
    y_f, h_f = ssd_scan(xbc, dt_raw, dt_bias, a_log, h0_f, batch, seq, emit_state)
    y_ssd, h_b = ssd_scan(xbc, dt_raw, dt_bias, a_log, h0_b, batch, seq, emit_state,
                          finish=(y_f, proj, d_exp, p['ssm_norm_g']))
    y_sc = short_conv(proj, p['sc_conv_w'], seq, plan['conv'])
    merged = merge_branches(y_att, y_ssd, y_sc, lw['w_att'], lw['w_ssd'], lw['w_sc'], l, proj,
                            *plan['merge'])
    x = matmul_norm_residual(merged, lw['w_o'], l, x, p['g_post_mix'], mod, mod_row, 2, *plan['w_o'])
    act = norm_swiglu(x, p['g_pre_ffn'], mod, mod_row, lw['w_gate_up'], l, *plan['swiglu'])
    x = matmul_norm_residual(act, lw['w_down'], l, x, p['g_post_ffn'], mod, mod_row, 5, *plan['down'])
    new_ctx = None
    if latent is None:
        kv = outs[2]
        k = kv[:, :KV_W].reshape(batch, seq, N_KV_HEADS, HEAD_DIM)
        v = kv[:, KV_W:].reshape(batch, seq, N_KV_HEADS, HEAD_DIM)
        new_ctx = (k, v, h_f, h_b)
    return x, new_ctx


def kernel(x_prompt, x_sample, cache_k, cache_v, state_ssm, c, c_ctx, w_mod, b_mod, g_pre_mix, w_in, sink, ssm_conv_w, ssm_conv_b, ssm_dt_bias, ssm_a_log, ssm_d, ssm_norm_g, sc_conv_w, w_att_out, w_ssd_out, w_sc_out, w_o, g_post_mix, g_pre_ffn, w_gate_up, w_down, g_post_ffn):
    batch, seq, _ = x_prompt.shape
    dec_batch, dec_seq, _ = x_sample.shape
    cond = jnp.concatenate([c_ctx[None, :], c, jnp.zeros((8 - 1 - dec_batch, D_MODEL), F32)], axis=0)
    mod = modulation(cond, w_mod, b_mod).reshape(DEPTH, 8, N_MOD, D_MODEL)
    rope = rope_tables(dec_seq)
    y_prompt = x_prompt.reshape(batch * seq, D_MODEL)
    y_sample = x_sample.reshape(dec_batch * dec_seq, D_MODEL)
    lw = _prepare_weights(w_in, w_att_out, w_ssd_out, w_sc_out, w_o, w_gate_up, w_down)
    state_in = state_ssm.reshape(dec_batch, 2 * DEPTH, SSM_HEADS * SSM_HEAD_DIM, SSM_STATE)
    ks_out, vs_out, ss_out = [], [], []
    for l in range(DEPTH):
        p = {'g_pre_mix': g_pre_mix[l], 'sink': sink[l], 'ssm_conv_w': ssm_conv_w[l],
             'ssm_conv_b': ssm_conv_b[l], 'ssm_dt_bias': ssm_dt_bias[l], 'ssm_a_log': ssm_a_log[l],
             'ssm_d': ssm_d[l], 'ssm_norm_g': ssm_norm_g[l], 'sc_conv_w': sc_conv_w[l],
             'g_post_mix': g_post_mix[l], 'g_pre_ffn': g_pre_ffn[l], 'g_post_ffn': g_post_ffn[l]}
        y_prompt, (k_l, v_l, h_f, h_b) = _trunk_layer(
            y_prompt, mod[l], lambda r: 0, l, lw, p, batch, seq, None)
        ks_out.append(k_l)
        vs_out.append(v_l)
        ss_out += [h_f, h_b]
        y_sample, _ = _trunk_layer(
            y_sample, mod[l], lambda r: 1 + r // dec_seq, l, lw, p, dec_batch, dec_seq,
            (cache_k[:, l], cache_v[:, l], state_in, rope))
    new_state = jnp.stack(ss_out, axis=1).reshape(
        batch, DEPTH, 2, SSM_HEADS, SSM_HEAD_DIM, SSM_STATE)
    return (y_prompt.reshape(batch, seq, D_MODEL), y_sample.reshape(dec_batch, dec_seq, D_MODEL),
            jnp.stack(ks_out, axis=1), jnp.stack(vs_out, axis=1), new_state)
```

```python
import functools

import jax
import jax.numpy as jnp
from jax import lax
from jax.experimental import pallas as pl
from jax.experimental.pallas import tpu as pltpu

F32 = jnp.float32
BF16 = jnp.bfloat16

D_MODEL = 2048
DEPTH = 2
GRID_W = 64
N_HEADS = 16
N_KV_HEADS = 4
KV_REP = N_HEADS // N_KV_HEADS
HEAD_DIM = 128
Q_W = N_HEADS * HEAD_DIM
KV_W = N_KV_HEADS * HEAD_DIM
WINDOW = 128
ATT_BLOCK = 128
ATT_SCALE = HEAD_DIM ** -0.5
ROPE_THETA = 10000.0
SSM_HEADS = 32
SSM_HEAD_DIM = 64
SSM_INNER = SSM_HEADS * SSM_HEAD_DIM
SSM_GROUPS = 4
SSM_STATE = 128
SSM_BC = SSM_GROUPS * SSM_STATE
SSM_CONV_CH = SSM_INNER + 2 * SSM_BC
SSM_CONV = 5
SSM_CHUNK = 128
SC_WIDTH = D_MODEL
SC_CONV = 3
N_BRANCH = 3
D_FF = ((8 * D_MODEL + 3 * 256 - 1) // (3 * 256)) * 256
N_MOD = 6
EPS = 1e-6
LOG2E = 1.4426950408889634

_O_Q = 0
_O_K = _O_Q + Q_W
_O_V = _O_K + KV_W
_O_Z = _O_V + KV_W
_O_XS = _O_Z + SSM_INNER
_O_BM = _O_XS + SSM_INNER
_O_CM = _O_BM + SSM_BC
_O_DT = _O_CM + SSM_BC
_O_SCB = _O_DT + 2 * SSM_HEADS
_O_SCC = _O_SCB + SC_WIDTH
_O_SCH = _O_SCC + SC_WIDTH
_O_G = _O_SCH + SC_WIDTH
D_IN = _O_G + N_BRANCH * D_MODEL

P_Q = 0
P_Z = 2048
P_SCB = 4096
P_SCC = 6144
P_SCH = 8192
P_G = 10240
P_XS = 16384
P_BM = 18432
P_CM = 18944
P_K = 19456
P_V = 19968
P_N = 20480

SSD_PROBLEMS = 2
ATT_STACK = KV_REP
LANES = 128
HALO = 16
VMEM_LIMIT = 56 * 1024 * 1024


def _cparams(sem, **flags):
    return pltpu.CompilerParams(dimension_semantics=sem, vmem_limit_bytes=VMEM_LIMIT,
                                flags=flags or None)


def _rms(x, g):
    return x * lax.rsqrt(jnp.mean(x * x, axis=-1, keepdims=True) + EPS) * g


def _silu(x):
    return x * jax.nn.sigmoid(x)


def _mod_kernel(c_ref, w_ref, b_ref, o_ref):
    a = _silu(c_ref[...]).astype(BF16)
    o_ref[0] = jnp.dot(a, w_ref[0].astype(BF16), preferred_element_type=F32) + b_ref[0]


def modulation(cond, w_mod, b_mod):
    rows = cond.shape[0]
    n = N_MOD * D_MODEL
    tn = 1024
    return pl.pallas_call(
        _mod_kernel,
        grid=(DEPTH, n // tn),
        in_specs=[pl.BlockSpec((rows, D_MODEL), lambda l, j: (0, 0)),
                  pl.BlockSpec((1, D_MODEL, tn), lambda l, j: (l, 0, j)),
                  pl.BlockSpec((1, 1, tn), lambda l, j: (l, 0, j))],
        out_specs=pl.BlockSpec((1, rows, tn), lambda l, j: (l, 0, j)),
        out_shape=jax.ShapeDtypeStruct((DEPTH, rows, n), F32),
        compiler_params=_cparams(("parallel", "parallel")),
        name="modulation",
    )(cond, w_mod, b_mod.reshape(DEPTH, 1, n))


def _norm_mod(x_ref, g_ref, mod_ref, shift_i, scale_i):
    y = _rms(x_ref[...], g_ref[...])
    return y * (1.0 + mod_ref[0, scale_i:scale_i + 1, :]) + mod_ref[0, shift_i:shift_i + 1, :]


def _by_parity(i, fn, buf_a, buf_b):
    @pl.when(i % 2 == 0)
    def _():
        fn(buf_a, buf_b)

    @pl.when(i % 2 == 1)
    def _():
        fn(buf_b, buf_a)


def _in_proj_kernel(x_ref, g_ref, mod_ref, w_ref, wdt_ref, o_ref, odt_ref, *rest, emit_kv):
    okv_ref, h_ref = rest if emit_kv else (None, rest[0])

    @pl.when(pl.program_id(1) == 0)
    def _():
        h = _norm_mod(x_ref, g_ref, mod_ref, 0, 1).astype(BF16)
        h_ref[...] = h
        odt_ref[...] = jnp.dot(h, wdt_ref[...], preferred_element_type=F32)

    res = jnp.dot(h_ref[...], w_ref[...], preferred_element_type=F32)
    o_ref[...] = res.astype(o_ref.dtype)
    if emit_kv:
        @pl.when(pl.program_id(1) == pl.num_programs(1) - 1)
        def _():
            okv_ref[...] = res


def in_proj(x, g, mod, mod_row, w, w_dt, l, tm, tn, emit_kv):
    m = x.shape[0]
    n = w.shape[2]
    assert w.shape[0] == 1 and P_N - P_K == tn and 2 * KV_W == tn
    out_specs = [pl.BlockSpec((tm, tn), lambda i, j: (i, j)),
                 pl.BlockSpec((tm, LANES), lambda i, j: (i, 0))]
    out_shape = [jax.ShapeDtypeStruct((m, n), BF16), jax.ShapeDtypeStruct((m, LANES), F32)]
    if emit_kv:
        out_specs.append(pl.BlockSpec((tm, tn), lambda i, j: (i, 0)))
        out_shape.append(jax.ShapeDtypeStruct((m, tn), F32))
    return pl.pallas_call(
        functools.partial(_in_proj_kernel, emit_kv=emit_kv),
        grid=(m // tm, n // tn),
        in_specs=[pl.BlockSpec((tm, D_MODEL), lambda i, j: (i, 0)),
                  pl.BlockSpec((1, D_MODEL), lambda i, j: (0, 0)),
                  pl.BlockSpec((1, N_MOD, D_MODEL), lambda i, j: (mod_row(i * tm), 0, 0)),
                  pl.BlockSpec((None, D_MODEL, tn), lambda i, j: (0, 0, j)),
                  pl.BlockSpec((None, D_MODEL, LANES), lambda i, j: (l, 0, 0))],
        out_specs=out_specs,
        out_shape=out_shape,
        scratch_shapes=[pltpu.VMEM((tm, D_MODEL), BF16)],
        compiler_params=_cparams(("parallel", "arbitrary")),
        name="in_proj",
    )(x, g.reshape(1, D_MODEL), mod, w, w_dt)


def _nm_swiglu_kernel(x_ref, g_ref, mod_ref, wg_ref, wu_ref, o_ref, h_ref):
    @pl.when(pl.program_id(1) == 0)
    def _():
        h_ref[...] = _norm_mod(x_ref, g_ref, mod_ref, 3, 4).astype(BF16)

    h = h_ref[...]
    a = jnp.dot(h, wg_ref[...], preferred_element_type=F32)
    b = jnp.dot(h, wu_ref[...], preferred_element_type=F32)
    o_ref[...] = (_silu(a) * b).astype(o_ref.dtype)


def norm_swiglu(x, g, mod, mod_row, w_gate_up, l, tm, tn):
    m = x.shape[0]
    nb = D_FF // tn
    return pl.pallas_call(
        _nm_swiglu_kernel,
        grid=(m // tm, nb),
        in_specs=[pl.BlockSpec((tm, D_MODEL), lambda i, j: (i, 0)),
                  pl.BlockSpec((1, D_MODEL), lambda i, j: (0, 0)),
                  pl.BlockSpec((1, N_MOD, D_MODEL), lambda i, j: (mod_row(i * tm), 0, 0)),
                  pl.BlockSpec((None, D_MODEL, tn), lambda i, j: (l, 0, j)),
                  pl.BlockSpec((None, D_MODEL, tn), lambda i, j: (l, 0, j + nb))],
        out_specs=pl.BlockSpec((tm, tn), lambda i, j: (i, j)),
        out_shape=jax.ShapeDtypeStruct((m, D_FF), BF16),
        scratch_shapes=[pltpu.VMEM((tm, D_MODEL), BF16)],
        compiler_params=_cparams(("parallel", "arbitrary")),
        name="norm_swiglu",
    )(x, g.reshape(1, D_MODEL), mod, w_gate_up, w_gate_up)


def _matmul_norm_res_kernel(a_ref, w_ref, x_ref, g_ref, mod_ref, o_ref, acca_ref, accb_ref,
                            *, gate_i, nt, ksteps, tm):
    i = pl.program_id(0)
    k = pl.program_id(1)
    rows = tm // ksteps
    r0 = pl.multiple_of(k * rows, rows)

    @pl.when((i == 0) & (k == 0))
    def _():
        acca_ref[...] = jnp.zeros_like(acca_ref)
        accb_ref[...] = jnp.zeros_like(accb_ref)

    def row(acc_cur, acc_prev):
        def epilogue():
            y = _rms(acc_prev[pl.ds(r0, rows), :], g_ref[...])
            o_ref[pl.ds(r0, rows), :] = (x_ref[pl.ds(r0, rows), :]
                                         + mod_ref[0, gate_i:gate_i + 1, :] * y)

        @pl.when(i < nt)
        def _():
            prod = jnp.dot(a_ref[...], w_ref[...], preferred_element_type=F32)
            acc_cur[...] = jnp.where(k == 0, prod, acc_cur[...] + prod)
            epilogue()

        @pl.when(i == nt)
        def _():
            epilogue()

    _by_parity(i, row, acca_ref, accb_ref)


def _matmul_norm_res_single_kernel(a_ref, w_ref, x_ref, g_ref, mod_ref, o_ref, *, gate_i):
    y = _rms(jnp.dot(a_ref[...], w_ref[...], preferred_element_type=F32), g_ref[...])
    o_ref[...] = x_ref[...] + mod_ref[0, gate_i:gate_i + 1, :] * y


def matmul_norm_residual(a, w, l, x, g, mod, mod_row, gate_i, tm, tk):
    m, kdim = a.shape
    nt = m // tm
    ksteps = kdim // tk
    if ksteps == 1:
        return pl.pallas_call(
            functools.partial(_matmul_norm_res_single_kernel, gate_i=gate_i),
            grid=(nt,),
            in_specs=[pl.BlockSpec((tm, kdim), lambda i: (i, 0)),
                      pl.BlockSpec((None, kdim, D_MODEL), lambda i: (l, 0, 0)),
                      pl.BlockSpec((tm, D_MODEL), lambda i: (i, 0)),
                      pl.BlockSpec((1, D_MODEL), lambda i: (0, 0)),
                      pl.BlockSpec((1, N_MOD, D_MODEL), lambda i: (mod_row(i * tm), 0, 0))],
            out_specs=pl.BlockSpec((tm, D_MODEL), lambda i: (i, 0)),
            out_shape=jax.ShapeDtypeStruct((m, D_MODEL), F32),
            compiler_params=_cparams(("parallel",)),
            name="matmul_norm_residual_1k",
        )(a, w, x, g.reshape(1, D_MODEL), mod)

    def kk(i, k):
        return jnp.where(i == nt, ksteps - 1, k)

    def prev(i):
        return jnp.maximum(i - 1, 0)

    return pl.pallas_call(
        functools.partial(_matmul_norm_res_kernel, gate_i=gate_i, nt=nt, ksteps=ksteps, tm=tm),
        grid=(nt + 1, ksteps),
        in_specs=[pl.BlockSpec((tm, tk), lambda i, k: (jnp.minimum(i, nt - 1), kk(i, k))),
                  pl.BlockSpec((None, tk, D_MODEL), lambda i, k: (l, kk(i, k), 0)),
                  pl.BlockSpec((tm, D_MODEL), lambda i, k: (prev(i), 0)),
                  pl.BlockSpec((1, D_MODEL), lambda i, k: (0, 0)),
                  pl.BlockSpec((1, N_MOD, D_MODEL), lambda i, k: (mod_row(prev(i) * tm), 0, 0))],
        out_specs=pl.BlockSpec((tm, D_MODEL), lambda i, k: (prev(i), 0)),
        out_shape=jax.ShapeDtypeStruct((m, D_MODEL), F32),
        scratch_shapes=[pltpu.VMEM((tm, D_MODEL), F32), pltpu.VMEM((tm, D_MODEL), F32)],
        compiler_params=_cparams(("arbitrary", "arbitrary")),
        name="matmul_norm_residual",
    )(a, w, x, g.reshape(1, D_MODEL), mod)


def _merge_kernel(ya_ref, ys_ref, yc_ref, wa_ref, ws_ref, wc_ref, ga_ref, gs_ref, gc_ref, o_ref):
    def branch(y_ref, w_ref, gate_ref):
        return jax.nn.sigmoid(gate_ref[...].astype(F32)) * jnp.dot(
            y_ref[...], w_ref[...], preferred_element_type=F32)

    o_ref[...] = (branch(ya_ref, wa_ref, ga_ref) + branch(ys_ref, ws_ref, gs_ref)
                  + branch(yc_ref, wc_ref, gc_ref)).astype(o_ref.dtype)


def merge_branches(y_att, y_ssd, y_sc, w_att, w_ssd, w_sc, l, proj, tm, tn):
    m = y_att.shape[0]
    gb = P_G // tn
    nb = D_MODEL // tn
    y_spec = pl.BlockSpec((tm, D_MODEL), lambda i, j: (i, 0))
    w_spec = pl.BlockSpec((None, D_MODEL, tn), lambda i, j: (l, 0, j))

    def g_spec(b):
        return pl.BlockSpec((tm, tn), lambda i, j: (i, gb + b * nb + j))

    return pl.pallas_call(
        _merge_kernel,
        grid=(m // tm, nb),
        in_specs=[y_spec, y_spec, y_spec, w_spec, w_spec, w_spec, g_spec(0), g_spec(1), g_spec(2)],
        out_specs=pl.BlockSpec((tm, tn), lambda i, j: (i, j)),
        out_shape=jax.ShapeDtypeStruct((m, D_MODEL), BF16),
        compiler_params=_cparams(("parallel", "parallel")),
        name="merge_branches",
    )(y_att, y_ssd, y_sc, w_att, w_ssd, w_sc, proj, proj, proj)


def _softmax_pv(s, sink, v):
    m = jnp.maximum(jnp.max(s, axis=-1, keepdims=True), sink)
    p = jnp.exp(s - m)
    den = jnp.sum(p, axis=-1, keepdims=True) + jnp.exp(sink - m)
    return jnp.dot(p.astype(BF16), v, preferred_element_type=F32) / den


def _ctx_attn_kernel(sink_ref, q_ref, k_ref, v_ref, o_ref):
    for g in range(N_KV_HEADS):
        k_g = k_ref[:, g * HEAD_DIM:(g + 1) * HEAD_DIM]
        v_g = v_ref[:, g * HEAD_DIM:(g + 1) * HEAD_DIM]
        for r in range(KV_REP):
            h = g * KV_REP + r
            q_h = q_ref[:, h * HEAD_DIM:(h + 1) * HEAD_DIM]
            s = lax.dot_general(q_h, k_g, (((1,), (1,)), ((), ())),
                                preferred_element_type=F32) * ATT_SCALE
            o = _softmax_pv(s, sink_ref[h], v_g)
            o_ref[:, h * HEAD_DIM:(h + 1) * HEAD_DIM] = o.astype(o_ref.dtype)


def context_attention(proj, sink, batch, seq):
    m = batch * seq
    return pl.pallas_call(
        _ctx_attn_kernel,
        grid=(batch,),
        in_specs=[pl.BlockSpec(memory_space=pltpu.SMEM),
                  pl.BlockSpec((seq, Q_W), lambda b: (b, P_Q // Q_W)),
                  pl.BlockSpec((seq, KV_W), lambda b: (b, P_K // KV_W)),
                  pl.BlockSpec((seq, KV_W), lambda b: (b, P_V // KV_W))],
        out_specs=pl.BlockSpec((seq, Q_W), lambda b: (b, 0)),
        out_shape=jax.ShapeDtypeStruct((m, Q_W), BF16),
        compiler_params=_cparams(("parallel",)),
        name="context_attention",
    )(sink, proj, proj, proj)


def _rope_tile(x, cos, sin_signed, even_lane):
    nxt = pltpu.roll(x, LANES - 1, axis=1)
    prv = pltpu.roll(x, 1, axis=1)
    return x * cos + jnp.where(even_lane, nxt, prv) * sin_signed


def _lat_attn_kernel(sink_ref, q_ref, k_ref, v_ref, ck_ref, cv_ref, cosq_ref, sinq_ref,
                     cosk_ref, sink_tab_ref, o_ref, krot_ref, *, seq, past):
    i = pl.program_id(1)
    band = ATT_BLOCK + 2 * WINDOW
    rows = 256

    @pl.when(i == 0)
    def _():
        def body(c, carry):
            r0 = pl.multiple_of(c * rows, rows)
            cos = cosk_ref[pl.ds(r0, rows), :]
            sin = sink_tab_ref[pl.ds(r0, rows), :]
            even = lax.broadcasted_iota(jnp.int32, (rows, LANES), 1) % 2 == 0
            for g in range(N_KV_HEADS):
                x = k_ref[pl.ds(r0, rows), g * HEAD_DIM:(g + 1) * HEAD_DIM].astype(F32)
                krot_ref[pl.ds(r0, rows), g * HEAD_DIM:(g + 1) * HEAD_DIM] = _rope_tile(
                    x, cos, sin, even).astype(BF16)
            return carry

        lax.fori_loop(0, seq // rows, body, 0)

    start = pl.multiple_of(jnp.clip(i * ATT_BLOCK - WINDOW, 0, seq - band), ATT_BLOCK)
    kb = krot_ref[pl.ds(start, band), :]
    vb = v_ref[pl.ds(start, band), :]
    ck = ck_ref[0]
    cv = cv_ref[0]
    stacked = ATT_STACK * ATT_BLOCK
    row = lax.broadcasted_iota(jnp.int32, (stacked, band), 0)
    col = lax.broadcasted_iota(jnp.int32, (stacked, band), 1)
    keep = jnp.abs(start + col - (i * ATT_BLOCK + row % ATT_BLOCK)) <= WINDOW
    head_of_row = lax.broadcasted_iota(jnp.int32, (stacked, 1), 0) // ATT_BLOCK
    cos_q = cosq_ref[...] * (ATT_SCALE * LOG2E)
    sin_q = sinq_ref[...] * (ATT_SCALE * LOG2E)
    even_q = lax.broadcasted_iota(jnp.int32, (ATT_BLOCK, LANES), 1) % 2 == 0
    for g in range(N_KV_HEADS):
        sl = slice(g * HEAD_DIM, (g + 1) * HEAD_DIM)
        kcat = jnp.concatenate([kb[:, sl], ck[:, sl]], axis=0)
        vcat = jnp.concatenate([vb[:, sl], cv[:, sl]], axis=0)
        for h0 in range(g * KV_REP, (g + 1) * KV_REP, ATT_STACK):
            q_parts = []
            sink2 = jnp.zeros((stacked, 1), F32)
            for r in range(ATT_STACK):
                hs = slice((h0 + r) * HEAD_DIM, (h0 + r + 1) * HEAD_DIM)
                q_parts.append(_rope_tile(q_ref[:, hs].astype(F32), cos_q, sin_q, even_q).astype(BF16))
                sink2 = jnp.where(head_of_row == r, sink_ref[h0 + r] * LOG2E, sink2)
            q_g = jnp.concatenate(q_parts, axis=0)
            s = lax.dot_general(q_g, kcat, (((1,), (1,)), ((), ())), preferred_element_type=F32)
            s_loc = jnp.where(keep, s[:, :band], -jnp.inf)
            s_ctx = s[:, band:]
            m = jnp.maximum(jnp.maximum(jnp.max(s_loc, axis=-1, keepdims=True),
                                        jnp.max(s_ctx, axis=-1, keepdims=True)), sink2)
            p_loc = jnp.exp2(s_loc - m)
            p_ctx = jnp.exp2(s_ctx - m)
            den = (jnp.sum(p_loc, axis=-1, keepdims=True) + jnp.sum(p_ctx, axis=-1, keepdims=True)
                   + jnp.exp2(sink2 - m))
            p = jnp.concatenate([p_loc.astype(BF16), p_ctx.astype(BF16)], axis=1)
            o = jnp.dot(p, vcat, preferred_element_type=F32) / den
            for r in range(ATT_STACK):
                o_ref[:, (h0 + r) * HEAD_DIM:(h0 + r + 1) * HEAD_DIM] = o[
                    r * ATT_BLOCK:(r + 1) * ATT_BLOCK].astype(o_ref.dtype)


def latent_attention(proj, ctx_k, ctx_v, sink, cos_t, sin_t, batch, seq):
    m = batch * seq
    nb = seq // ATT_BLOCK
    past = ctx_k.shape[1]
    return pl.pallas_call(
        functools.partial(_lat_attn_kernel, seq=seq, past=past),
        grid=(batch, nb),
        in_specs=[pl.BlockSpec(memory_space=pltpu.SMEM),
                  pl.BlockSpec((ATT_BLOCK, Q_W), lambda b, i: (b * nb + i, P_Q // Q_W)),
                  pl.BlockSpec((seq, KV_W), lambda b, i: (b, P_K // KV_W)),
                  pl.BlockSpec((seq, KV_W), lambda b, i: (b, P_V // KV_W)),
                  pl.BlockSpec((1, past, KV_W), lambda b, i: (b, 0, 0)),
                  pl.BlockSpec((1, past, KV_W), lambda b, i: (b, 0, 0)),
                  pl.BlockSpec((ATT_BLOCK, LANES), lambda b, i: (i, 0)),
                  pl.BlockSpec((ATT_BLOCK, LANES), lambda b, i: (i, 0)),
                  pl.BlockSpec((seq, LANES), lambda b, i: (0, 0)),
                  pl.BlockSpec((seq, LANES), lambda b, i: (0, 0))],
        out_specs=pl.BlockSpec((ATT_BLOCK, Q_W), lambda b, i: (b * nb + i, 0)),
        out_shape=jax.ShapeDtypeStruct((m, Q_W), BF16),
        scratch_shapes=[pltpu.VMEM((seq, KV_W), BF16)],
        compiler_params=_cparams(("parallel", "arbitrary")),
        name="latent_attention",
    )(sink, proj, proj, proj, ctx_k, ctx_v, cos_t, sin_t, cos_t, sin_t)


def rope_tables(seq):
    rows = seq // GRID_W
    row = jnp.repeat(jnp.arange(rows, dtype=F32), GRID_W)
    col = jnp.tile(jnp.arange(GRID_W, dtype=F32), rows)
    n_pairs = HEAD_DIM // 4
    inv = ROPE_THETA ** (-jnp.arange(n_pairs, dtype=F32) / n_pairs)
    ang = jnp.concatenate([row[:, None] * inv, col[:, None] * inv], axis=-1)
    ang = jnp.repeat(ang, 2, axis=-1)
    sign = jnp.where(jnp.arange(HEAD_DIM) % 2 == 0, -1.0, 1.0).astype(F32)
    return jnp.cos(ang), jnp.sin(ang) * sign


def _conv_rolled(prev, main, nxt, w_ref, taps, tile):
    half = (taps - 1) // 2
    sub = 8
    acc = w_ref[half:half + 1, :] * main
    for k in range(taps):
        if k != half:
            acc = acc + w_ref[k:k + 1, :] * pltpu.roll(main, (half - k) % tile, axis=0)

    def edge(window):
        out = None
        for k in range(taps):
            term = w_ref[k:k + 1, :] * window[sub - half + k:2 * sub - half + k]
            out = term if out is None else out + term
        return out

    top = edge(jnp.concatenate([prev, main[:2 * sub]], axis=0))
    bottom = edge(jnp.concatenate([main[tile - 2 * sub:], nxt], axis=0))
    return jnp.concatenate([top, acc[sub:tile - sub], bottom], axis=0)


def _ssm_conv_kernel(prev_ref, main_ref, next_ref, w_ref, b_ref, o_ref, *, nt, tile):
    t = pl.program_id(0) % nt
    prev = jnp.where(t == 0, 0.0, prev_ref[...].astype(F32)[HALO - 8:])
    nxt = jnp.where(t == nt - 1, 0.0, next_ref[...].astype(F32)[:8])
    y = _conv_rolled(prev, main_ref[...].astype(F32), nxt, w_ref, SSM_CONV, tile) + b_ref[...]
    o_ref[...] = _silu(y).astype(o_ref.dtype)


def _halo_specs(tile, width, col0, m):
    hb = tile // HALO
    last = m // HALO - 1
    prev = pl.BlockSpec((HALO, width), lambda i, j: (jnp.maximum(i * hb - 1, 0), col0 + j))
    main = pl.BlockSpec((tile, width), lambda i, j: (i, col0 + j))
    nxt = pl.BlockSpec((HALO, width), lambda i, j: (jnp.minimum((i + 1) * hb, last), col0 + j))
    return prev, main, nxt


def ssm_conv(proj, w, b, seq, tile):
    m = proj.shape[0]
    width = 1024
    nt = seq // tile
    prev, main, nxt = _halo_specs(tile, width, P_XS // width, m)
    return pl.pallas_call(
        functools.partial(_ssm_conv_kernel, nt=nt, tile=tile),
        grid=(m // tile, SSM_CONV_CH // width),
        in_specs=[prev, main, nxt,
                  pl.BlockSpec((SSM_CONV, width), lambda i, j: (0, j)),
                  pl.BlockSpec((1, width), lambda i, j: (0, j))],
        out_specs=pl.BlockSpec((tile, width), lambda i, j: (i, j)),
        out_shape=jax.ShapeDtypeStruct((m, SSM_CONV_CH), BF16),
        compiler_params=_cparams(("parallel", "parallel")),
        name="ssm_conv",
    )(proj, proj, proj, w, b.reshape(1, SSM_CONV_CH))


def _short_conv_kernel(b_ref, cp_ref, cm_ref, cn_ref, hp_ref, hm_ref, hn_ref, w_ref, o_ref,
                       *, nt, tile):
    t = pl.program_id(0) % nt

    def prod(c_ref, h_ref):
        return c_ref[...].astype(F32) * h_ref[...].astype(F32)

    prev = jnp.where(t == 0, 0.0, prod(cp_ref, hp_ref)[HALO - 8:])
    nxt = jnp.where(t == nt - 1, 0.0, prod(cn_ref, hn_ref)[:8])
    y = _conv_rolled(prev, prod(cm_ref, hm_ref), nxt, w_ref, SC_CONV, tile)
    o_ref[...] = (b_ref[...].astype(F32) * y).astype(o_ref.dtype)


def short_conv(proj, w, seq, tile):
    m = proj.shape[0]
    width = 1024
    nt = seq // tile
    cp, cm, cn = _halo_specs(tile, width, P_SCC // width, m)
    hp, hm, hn = _halo_specs(tile, width, P_SCH // width, m)
    return pl.pallas_call(
        functools.partial(_short_conv_kernel, nt=nt, tile=tile),
        grid=(m // tile, SC_WIDTH // width),
        in_specs=[pl.BlockSpec((tile, width), lambda i, j: (i, P_SCB // width + j)),
                  cp, cm, cn, hp, hm, hn,
                  pl.BlockSpec((SC_CONV, width), lambda i, j: (0, j))],
        out_specs=pl.BlockSpec((tile, width), lambda i, j: (i, j)),
        out_shape=jax.ShapeDtypeStruct((m, SC_WIDTH), BF16),
        compiler_params=_cparams(("parallel", "parallel")),
        name="short_conv",
    )(proj, proj, proj, proj, proj, proj, proj, w)


def _split3(a):
    a1 = a.astype(BF16)
    r1 = a - a1.astype(F32)
    a2 = r1.astype(BF16)
    a3 = (r1 - a2.astype(F32)).astype(BF16)
    return a1, a2, a3


def _softplus(x):
    return jnp.maximum(x, 0.0) + jnp.log1p(jnp.exp(-jnp.abs(x)))


def _ssd_kernel(*refs, backward, has_h0, emit_state, nprob):
    refs = list(refs)
    x_ref, b_ref, c_ref, dt_ref, dtb_ref, alog_ref, sel_ref = refs[:7]
    del refs[:7]
    h0_ref = refs.pop(0) if has_h0 else None
    if backward:
        yf_ref, z_ref, d_ref, gn_ref = refs[:4]
        del refs[:4]
    y_ref = refs.pop(0)
    hout_ref = refs.pop(0) if emit_state else None
    st_ref = refs.pop(0)
    acc_ref = refs.pop(0) if backward else y_ref
    q = SSM_CHUNK
    gw = (SSM_HEADS // SSM_GROUPS) * SSM_HEAD_DIM

    @pl.when(pl.program_id(1) == 0)
    def _():
        for p in range(nprob):
            for g in range(SSM_GROUPS):
                if has_h0:
                    st_ref[p, g] = h0_ref[p, 0, g * gw:(g + 1) * gw, :].T
                else:
                    st_ref[p, g] = jnp.zeros((SSM_STATE, gw), F32)

    lo = SSM_HEADS if backward else 0
    neg_a = -LOG2E * jnp.exp(alog_ref[...])
    ii = lax.broadcasted_iota(jnp.int32, (q, q), 0)
    jj = lax.broadcasted_iota(jnp.int32, (q, q), 1)
    tri = (jj >= ii) if backward else (jj <= ii)
    tri_b = jnp.where(tri, 1.0, 0.0).astype(BF16)
    lane_lo = lax.broadcasted_iota(jnp.int32, (1, LANES), 1) < SSM_HEAD_DIM
    heads_per_group = SSM_HEADS // SSM_GROUPS
    pairs_per_group = heads_per_group // 2
    edge = 0 if backward else q - 1
    probs = range(nprob)

    pre = []
    for p in probs:
        dt = _softplus(dt_ref[p] + dtb_ref[...])
        a2 = dt * neg_a
        u2 = None
        for piece in _split3(a2):
            d = jnp.dot(tri_b, piece, preferred_element_type=F32)
            u2 = d if u2 is None else u2 + d
        v_t = (u2 - jnp.log2(dt)).T
        t_row = u2[edge:edge + 1, :]
        w_c = dt * jnp.exp2(t_row - u2)
        w_hi = w_c.astype(BF16)
        w_lo = (w_c - w_hi.astype(F32)).astype(BF16)
        w_exp = (jnp.dot(w_hi, sel_ref[...], preferred_element_type=F32)
                 + jnp.dot(w_lo, sel_ref[...], preferred_element_type=F32))
        pre.append((u2, v_t, t_row, w_exp))

    for g in range(SSM_GROUPS):
        grp = []
        for p in probs:
            b_g = b_ref[p, :, g * SSM_STATE:(g + 1) * SSM_STATE]
            c_g = c_ref[p, :, g * SSM_STATE:(g + 1) * SSM_STATE]
            cb = lax.dot_general(c_g, b_g, (((1,), (1,)), ((), ())), preferred_element_type=F32)
            h_t = st_ref[p, g]
            y_off = jnp.dot(c_g, h_t.astype(BF16), preferred_element_type=F32)
            grp.append((b_g, cb, h_t, y_off, [], []))
        for pr in range(pairs_per_group):
            h_a = g * heads_per_group + 2 * pr
            ca = lo + h_a
            cbk = ca + 1
            pc = (h_a // 2) * LANES
            for p in probs:
                u2, v_t, t_row, w_exp = pre[p]
                b_g, cb, h_t, y_off, xs_parts, dec_parts = grp[p]
                x_b = x_ref[p, :, pc:pc + LANES]
                up = jnp.where(lane_lo, u2[:, ca:ca + 1], u2[:, cbk:cbk + 1])
                tp = jnp.where(lane_lo, t_row[:, ca:ca + 1], t_row[:, cbk:cbk + 1])
                ys = []
                for col in (ca, cbk):
                    seg = u2[:, col:col + 1] - v_t[col:col + 1, :]
                    decay_dt = jnp.exp2(jnp.where(tri, seg, -jnp.inf))
                    ys.append(jnp.dot((cb * decay_dt).astype(BF16), x_b, preferred_element_type=F32))
                acc_ref[p, :, pc:pc + LANES] = (jnp.where(lane_lo, ys[0], ys[1])
                                                + jnp.exp2(up) * y_off[:, pr * LANES:(pr + 1) * LANES])
                xs_parts.append((x_b.astype(F32) * w_exp[:, pc:pc + LANES]).astype(BF16))
                dec_parts.append(jnp.exp2(tp))
        for p in probs:
            b_g, cb, h_t, y_off, xs_parts, dec_parts = grp[p]
            xs_g = jnp.concatenate(xs_parts, axis=1)
            dec_g = jnp.concatenate(dec_parts, axis=1)
            upd = lax.dot_general(b_g, xs_g, (((0,), (0,)), ((), ())), preferred_element_type=F32)
            st_ref[p, g] = dec_g * h_t + upd

    if backward:
        for p in probs:
            y = acc_ref[p] + yf_ref[p] + x_ref[p].astype(F32) * d_ref[...]
            y = y * _silu(z_ref[p].astype(F32))
            y_ref[p] = _rms(y, gn_ref[...]).astype(y_ref.dtype)

    if emit_state:
        @pl.when(pl.program_id(1) == pl.num_programs(1) - 1)
        def _():
            for p in range(nprob):
                for g in range(SSM_GROUPS):
                    hout_ref[p, g * gw:(g + 1) * gw, :] = st_ref[p, g].T


def _head_selector(backward):
    lane = jnp.arange(LANES)[:, None]
    head = jnp.arange(SSM_INNER)[None, :] // SSM_HEAD_DIM
    return (lane == head + (SSM_HEADS if backward else 0)).astype(BF16)


def ssd_scan(xbc, dt_raw, dt_bias, a_log, h0, batch, seq, emit_state, finish=None):
    backward = finish is not None
    m = batch * seq
    nc = seq // SSM_CHUNK
    gw = SSM_INNER // SSM_GROUPS
    hp = SSM_HEADS * SSM_HEAD_DIM

    nprob = SSD_PROBLEMS
    assert batch % nprob == 0

    def chunk(c):
        return nc - 1 - c if backward else c

    def seq_spec(width, col):
        return pl.BlockSpec((nprob, SSM_CHUNK, width), lambda b, c: (b, chunk(c), col))

    def by_seq(a):
        return a.reshape(batch, seq, a.shape[-1])

    vec = pl.BlockSpec((1, SSM_INNER), lambda b, c: (0, 0))
    lane_vec = pl.BlockSpec((1, LANES), lambda b, c: (0, 0))
    in_specs = [seq_spec(SSM_INNER, 0),
                seq_spec(SSM_BC, SSM_INNER // SSM_BC),
                seq_spec(SSM_BC, SSM_INNER // SSM_BC + 1),
                seq_spec(LANES, 0),
                lane_vec, lane_vec,
                pl.BlockSpec((LANES, SSM_INNER), lambda b, c: (0, 0))]
    xbc3 = by_seq(xbc)
    args = [xbc3, xbc3, xbc3, by_seq(dt_raw), dt_bias, a_log, _head_selector(backward)]
    if h0 is not None:
        state, slab = h0
        in_specs.append(pl.BlockSpec((nprob, 1, hp, SSM_STATE), lambda b, c: (b, slab, 0, 0)))
        args.append(state)
    scratch = [pltpu.VMEM((nprob, SSM_GROUPS, SSM_STATE, gw), F32)]
    if backward:
        y_fwd, proj, d_exp, norm_g = finish
        in_specs += [seq_spec(SSM_INNER, 0), seq_spec(SSM_INNER, P_Z // SSM_INNER), vec, vec]
        args += [by_seq(y_fwd), by_seq(proj), d_exp, norm_g.reshape(1, SSM_INNER)]
        scratch.append(pltpu.VMEM((nprob, SSM_CHUNK, SSM_INNER), F32))
    out_specs = [seq_spec(SSM_INNER, 0)]
    out_shape = [jax.ShapeDtypeStruct((batch, seq, SSM_INNER), BF16 if backward else F32)]
    if emit_state:
        out_specs.append(pl.BlockSpec((nprob, hp, SSM_STATE), lambda b, c: (b, 0, 0)))
        out_shape.append(jax.ShapeDtypeStruct((batch, hp, SSM_STATE), F32))
    out = pl.pallas_call(
        functools.partial(_ssd_kernel, backward=backward, has_h0=h0 is not None,
                          emit_state=emit_state, nprob=nprob),
        grid=(batch // nprob, nc),
        in_specs=in_specs,
        out_specs=out_specs,
        out_shape=out_shape,
        scratch_shapes=scratch,
        compiler_params=_cparams(("parallel", "arbitrary")),
        name="ssd_scan_bwd" if backward else "ssd_scan_fwd",
    )(*args)
    y = out[0].reshape(m, SSM_INNER)
    return (y, out[1]) if emit_state else (y, None)


def _pad_lanes(v):
    flat = v.reshape(1, 2 * SSM_HEADS).astype(F32)
    return jnp.pad(flat, ((0, 0), (0, LANES - 2 * SSM_HEADS)))


def _prepare_weights(w_in, w_att_out, w_ssd_out, w_sc_out, w_o, w_gate_up, w_down):
    def cols(o, n):
        return w_in[:, :, o:o + n]

    def reordered(l):
        w = w_in[l]
        parts = [(_O_Q, Q_W), (_O_Z, SSM_INNER), (_O_SCB, SC_WIDTH), (_O_SCC, SC_WIDTH),
                 (_O_SCH, SC_WIDTH), (_O_G, N_BRANCH * D_MODEL), (_O_XS, SSM_INNER),
                 (_O_BM, SSM_BC), (_O_CM, SSM_BC), (_O_K, KV_W), (_O_V, KV_W)]
        return jnp.concatenate([w[:, o:o + n] for o, n in parts], axis=1).astype(BF16)[None]

    w_main = [reordered(l) for l in range(DEPTH)]
    w_dt = jnp.pad(cols(_O_DT, 2 * SSM_HEADS), ((0, 0), (0, 0), (0, LANES - 2 * SSM_HEADS))).astype(BF16)
    return dict(w_main=w_main, w_dt=w_dt,
                w_att=w_att_out.astype(BF16), w_ssd=w_ssd_out.astype(BF16),
                w_sc=w_sc_out.astype(BF16), w_o=w_o.astype(BF16),
                w_gate_up=w_gate_up.astype(BF16), w_down=w_down.astype(BF16))


def _tile_plan(seq):
    return dict(
        in_proj=(1024, 1024),
        swiglu=(1024, 512),
        down=(512, D_FF // 4),
        w_o=(512, D_MODEL),
        merge=(512, 512),
        conv=min(seq, 512))


def _trunk_layer(x, mod, mod_row, l, lw, p, batch, seq, latent):
    plan = _tile_plan(seq)
    outs = in_proj(x, p['g_pre_mix'], mod, mod_row, lw['w_main'][l], lw['w_dt'], l, *plan['in_proj'],
                   emit_kv=latent is None)
    proj, dt_raw = outs[0], outs[1]
    if latent is None:
        y_att = context_attention(proj, p['sink'], batch, seq)
        h0_f = h0_b = None
    else:
        ctx_k, ctx_v, state, rope = latent
        y_att = latent_attention(proj, ctx_k.reshape(batch, -1, KV_W).astype(BF16),
                                 ctx_v.reshape(batch, -1, KV_W).astype(BF16), p['sink'],
                                 rope[0], rope[1], batch, seq)
        h0_f, h0_b = (state, 2 * l), (state, 2 * l + 1)
    xbc = ssm_conv(proj, p['ssm_conv_w'], p['ssm_conv_b'], seq, plan['conv'])
    dt_bias = _pad_lanes(p['ssm_dt_bias'])
    a_log = _pad_lanes(p['ssm_a_log'])
    d_exp = jnp.repeat(p['ssm_d'].astype(F32), SSM_HEAD_DIM).reshape(1, SSM_INNER)
    emit_state = latent is None
    y_f, h_f = ssd_scan(xbc, dt_raw, dt_bias, a_log, h0_f, batch, seq, emit_state)
    y_ssd, h_b = ssd_scan(xbc, dt_raw, dt_bias, a_log, h0_b, batch, seq, emit_state,
                          finish=(y_f, proj, d_exp, p['ssm_norm_g']))
    y_sc = short_conv(proj, p['sc_conv_w'], seq, plan['conv'])
    merged = merge_branches(y_att, y_ssd, y_sc, lw['w_att'], lw['w_ssd'], lw['w_sc'], l, proj,
                            *plan['merge'])
    x = matmul_norm_residual(merged, lw['w_o'], l, x, p['g_post_mix'], mod, mod_row, 2, *plan['w_o'])
    act = norm_swiglu(x, p['g_pre_ffn'], mod, mod_row, lw['w_gate_up'], l, *plan['swiglu'])
    x = matmul_norm_residual(act, lw['w_down'], l, x, p['g_post_ffn'], mod, mod_row, 5, *plan['down'])
    new_ctx = None
    if latent is None:
        kv = outs[2]
        k = kv[:, :KV_W].reshape(batch, seq, N_KV_HEADS, HEAD_DIM)
        v = kv[:, KV_W:].reshape(batch, seq, N_KV_HEADS, HEAD_DIM)
        new_ctx = (k, v, h_f, h_b)
    return x, new_ctx


def kernel(x_prompt, x_sample, cache_k, cache_v, state_ssm, c, c_ctx, w_mod, b_mod, g_pre_mix, w_in, sink, ssm_conv_w, ssm_conv_b, ssm_dt_bias, ssm_a_log, ssm_d, ssm_norm_g, sc_conv_w, w_att_out, w_ssd_out, w_sc_out, w_o, g_post_mix, g_pre_ffn, w_gate_up, w_down, g_post_ffn):
    batch, seq, _ = x_prompt.shape
    dec_batch, dec_seq, _ = x_sample.shape
    cond = jnp.concatenate([c_ctx[None, :], c, jnp.zeros((8 - 1 - dec_batch, D_MODEL), F32)], axis=0)
    mod = modulation(cond, w_mod, b_mod).reshape(DEPTH, 8, N_MOD, D_MODEL)
    rope = rope_tables(dec_seq)
    y_prompt = x_prompt.reshape(batch * seq, D_MODEL)
    y_sample = x_sample.reshape(dec_batch * dec_seq, D_MODEL)
    lw = _prepare_weights(w_in, w_att_out, w_ssd_out, w_sc_out, w_o, w_gate_up, w_down)
    state_in = state_ssm.reshape(dec_batch, 2 * DEPTH, SSM_HEADS * SSM_HEAD_DIM, SSM_STATE)
    ks_out, vs_out, ss_out = [], [], []
    for l in range(DEPTH):
        p = {'g_pre_mix': g_pre_mix[l], 'sink': sink[l], 'ssm_conv_w': ssm_conv_w[l],
             'ssm_conv_b': ssm_conv_b[l], 'ssm_dt_bias': ssm_dt_bias[l], 'ssm_a_log': ssm_a_log[l],
             'ssm_d': ssm_d[l], 'ssm_norm_g': ssm_norm_g[l], 'sc_conv_w': sc_conv_w[l],
             'g_post_mix': g_post_mix[l], 'g_pre_ffn': g_pre_ffn[l], 'g_post_ffn': g_post_ffn[l]}
        y_prompt, (k_l, v_l, h_f, h_b) = _trunk_layer(
            y_prompt, mod[l], lambda r: 0, l, lw, p, batch, seq, None)
        ks_out.append(k_l)
        vs_out.append(v_l)
        ss_out += [h_f, h_b]
        y_sample, _ = _trunk_layer(
            y_sample, mod[l], lambda r: 1 + r // dec_seq, l, lw, p, dec_batch, dec_seq,
            (cache_k[:, l], cache_v[:, l], state_in, rope))
    new_state = jnp.stack(ss_out, axis=1).reshape(
        batch, DEPTH, 2, SSM_HEADS, SSM_HEAD_DIM, SSM_STATE)
    return (y_prompt.reshape(batch, seq, D_MODEL), y_sample.reshape(dec_batch, dec_seq, D_MODEL),
            jnp.stack(ks_out, axis=1), jnp.stack(vs_out, axis=1), new_state)
```

```python
import functools

import jax
import jax.numpy as jnp
from jax import lax
from jax.experimental import pallas as pl
from jax.experimental.pallas import tpu as pltpu

F32 = jnp.float32
BF16 = jnp.bfloat16

D_MODEL = 2048
DEPTH = 2
GRID_W = 64
N_HEADS = 16
N_KV_HEADS = 4
KV_REP = N_HEADS // N_KV_HEADS
HEAD_DIM = 128
Q_W = N_HEADS * HEAD_DIM
KV_W = N_KV_HEADS * HEAD_DIM
WINDOW = 128
ATT_BLOCK = 128
ATT_SCALE = HEAD_DIM ** -0.5
ROPE_THETA = 10000.0
SSM_HEADS = 32
SSM_HEAD_DIM = 64
SSM_INNER = SSM_HEADS * SSM_HEAD_DIM
SSM_GROUPS = 4
SSM_STATE = 128
SSM_BC = SSM_GROUPS * SSM_STATE
SSM_CONV_CH = SSM_INNER + 2 * SSM_BC
SSM_CONV = 5
SSM_CHUNK = 128
SC_WIDTH = D_MODEL
SC_CONV = 3
N_BRANCH = 3
D_FF = ((8 * D_MODEL + 3 * 256 - 1) // (3 * 256)) * 256
N_MOD = 6
EPS = 1e-6
LOG2E = 1.4426950408889634

_O_Q = 0
_O_K = _O_Q + Q_W
_O_V = _O_K + KV_W
_O_Z = _O_V + KV_W
_O_XS = _O_Z + SSM_INNER
_O_BM = _O_XS + SSM_INNER
_O_CM = _O_BM + SSM_BC
_O_DT = _O_CM + SSM_BC
_O_SCB = _O_DT + 2 * SSM_HEADS
_O_SCC = _O_SCB + SC_WIDTH
_O_SCH = _O_SCC + SC_WIDTH
_O_G = _O_SCH + SC_WIDTH
D_IN = _O_G + N_BRANCH * D_MODEL

P_Q = 0
P_Z = 2048
P_SCB = 4096
P_SCC = 6144
P_SCH = 8192
P_G = 10240
P_XS = 16384
P_BM = 18432
P_CM = 18944
P_K = 19456
P_V = 19968
P_N = 20480

SSD_PROBLEMS = 2
ATT_STACK = KV_REP
LANES = 128
HALO = 16
VMEM_LIMIT = 56 * 1024 * 1024


def _cparams(sem):
    return pltpu.CompilerParams(dimension_semantics=sem, vmem_limit_bytes=VMEM_LIMIT)


def _rms(x, g):
    return x * lax.rsqrt(jnp.mean(x * x, axis=-1, keepdims=True) + EPS) * g


def _silu(x):
    return x * jax.nn.sigmoid(x)


def _mod_kernel(c_ref, w_ref, b_ref, o_ref):
    a = _silu(c_ref[...]).astype(BF16)
    o_ref[0] = jnp.dot(a, w_ref[0].astype(BF16), preferred_element_type=F32) + b_ref[0]


def modulation(cond, w_mod, b_mod):
    rows = cond.shape[0]
    n = N_MOD * D_MODEL
    tn = 1024
    return pl.pallas_call(
        _mod_kernel,
        grid=(DEPTH, n // tn),
        in_specs=[pl.BlockSpec((rows, D_MODEL), lambda l, j: (0, 0)),
                  pl.BlockSpec((1, D_MODEL, tn), lambda l, j: (l, 0, j)),
                  pl.BlockSpec((1, 1, tn), lambda l, j: (l, 0, j))],
        out_specs=pl.BlockSpec((1, rows, tn), lambda l, j: (l, 0, j)),
        out_shape=jax.ShapeDtypeStruct((DEPTH, rows, n), F32),
        compiler_params=_cparams(("parallel", "parallel")),
        name="modulation",
    )(cond, w_mod, b_mod.reshape(DEPTH, 1, n))


def _norm_mod(x_ref, g_ref, mod_ref, shift_i, scale_i):
    y = _rms(x_ref[...], g_ref[...])
    return y * (1.0 + mod_ref[0, scale_i:scale_i + 1, :]) + mod_ref[0, shift_i:shift_i + 1, :]


def _by_parity(i, fn, buf_a, buf_b):
    @pl.when(i % 2 == 0)
    def _():
        fn(buf_a, buf_b)

    @pl.when(i % 2 == 1)
    def _():
        fn(buf_b, buf_a)


def _in_proj_kernel(x_ref, g_ref, mod_ref, w_ref, wdt_ref, o_ref, odt_ref, *rest, emit_kv):
    okv_ref, h_ref = rest if emit_kv else (None, rest[0])

    @pl.when(pl.program_id(1) == 0)
    def _():
        h = _norm_mod(x_ref, g_ref, mod_ref, 0, 1).astype(BF16)
        h_ref[...] = h
        odt_ref[...] = jnp.dot(h, wdt_ref[...], preferred_element_type=F32)

    res = jnp.dot(h_ref[...], w_ref[...], preferred_element_type=F32)
    o_ref[...] = res.astype(o_ref.dtype)
    if emit_kv:
        @pl.when(pl.program_id(1) == pl.num_programs(1) - 1)
        def _():
            okv_ref[...] = res


def in_proj(x, g, mod, mod_row, w, w_dt, l, tm, tn, emit_kv):
    m = x.shape[0]
    n = w.shape[2]
    assert w.shape[0] == 1 and P_N - P_K == tn and 2 * KV_W == tn
    out_specs = [pl.BlockSpec((tm, tn), lambda i, j: (i, j)),
                 pl.BlockSpec((tm, LANES), lambda i, j: (i, 0))]
    out_shape = [jax.ShapeDtypeStruct((m, n), BF16), jax.ShapeDtypeStruct((m, LANES), F32)]
    if emit_kv:
        out_specs.append(pl.BlockSpec((tm, tn), lambda i, j: (i, 0)))
        out_shape.append(jax.ShapeDtypeStruct((m, tn), F32))
    return pl.pallas_call(
        functools.partial(_in_proj_kernel, emit_kv=emit_kv),
        grid=(m // tm, n // tn),
        in_specs=[pl.BlockSpec((tm, D_MODEL), lambda i, j: (i, 0)),
                  pl.BlockSpec((1, D_MODEL), lambda i, j: (0, 0)),
                  pl.BlockSpec((1, N_MOD, D_MODEL), lambda i, j: (mod_row(i * tm), 0, 0)),
                  pl.BlockSpec((None, D_MODEL, tn), lambda i, j: (0, 0, j)),
                  pl.BlockSpec((None, D_MODEL, LANES), lambda i, j: (l, 0, 0))],
        out_specs=out_specs,
        out_shape=out_shape,
        scratch_shapes=[pltpu.VMEM((tm, D_MODEL), BF16)],
        compiler_params=_cparams(("parallel", "arbitrary")),
        name="in_proj",
    )(x, g.reshape(1, D_MODEL), mod, w, w_dt)


def _nm_swiglu_kernel(x_ref, g_ref, mod_ref, wg_ref, wu_ref, o_ref, h_ref):
    @pl.when(pl.program_id(1) == 0)
    def _():
        h_ref[...] = _norm_mod(x_ref, g_ref, mod_ref, 3, 4).astype(BF16)

    h = h_ref[...]
    a = jnp.dot(h, wg_ref[...], preferred_element_type=F32)
    b = jnp.dot(h, wu_ref[...], preferred_element_type=F32)
    o_ref[...] = (_silu(a) * b).astype(o_ref.dtype)


def norm_swiglu(x, g, mod, mod_row, w_gate_up, l, tm, tn):
    m = x.shape[0]
    nb = D_FF // tn
    return pl.pallas_call(
        _nm_swiglu_kernel,
        grid=(m // tm, nb),
        in_specs=[pl.BlockSpec((tm, D_MODEL), lambda i, j: (i, 0)),
                  pl.BlockSpec((1, D_MODEL), lambda i, j: (0, 0)),
                  pl.BlockSpec((1, N_MOD, D_MODEL), lambda i, j: (mod_row(i * tm), 0, 0)),
                  pl.BlockSpec((None, D_MODEL, tn), lambda i, j: (l, 0, j)),
                  pl.BlockSpec((None, D_MODEL, tn), lambda i, j: (l, 0, j + nb))],
        out_specs=pl.BlockSpec((tm, tn), lambda i, j: (i, j)),
        out_shape=jax.ShapeDtypeStruct((m, D_FF), BF16),
        scratch_shapes=[pltpu.VMEM((tm, D_MODEL), BF16)],
        compiler_params=_cparams(("parallel", "arbitrary")),
        name="norm_swiglu",
    )(x, g.reshape(1, D_MODEL), mod, w_gate_up, w_gate_up)


def _matmul_norm_res_kernel(a_ref, w_ref, x_ref, g_ref, mod_ref, o_ref, acca_ref, accb_ref,
                            *, gate_i, nt, ksteps, tm):
    i = pl.program_id(0)
    k = pl.program_id(1)
    rows = tm // ksteps
    r0 = pl.multiple_of(k * rows, rows)

    @pl.when((i == 0) & (k == 0))
    def _():
        acca_ref[...] = jnp.zeros_like(acca_ref)
        accb_ref[...] = jnp.zeros_like(accb_ref)

    def row(acc_cur, acc_prev):
        def epilogue():
            y = _rms(acc_prev[pl.ds(r0, rows), :], g_ref[...])
            o_ref[pl.ds(r0, rows), :] = (x_ref[pl.ds(r0, rows), :]
                                         + mod_ref[0, gate_i:gate_i + 1, :] * y)

        @pl.when(i < nt)
        def _():
            prod = jnp.dot(a_ref[...], w_ref[...], preferred_element_type=F32)
            acc_cur[...] = jnp.where(k == 0, prod, acc_cur[...] + prod)
            epilogue()

        @pl.when(i == nt)
        def _():
            epilogue()

    _by_parity(i, row, acca_ref, accb_ref)


def _matmul_norm_res_single_kernel(a_ref, w_ref, x_ref, g_ref, mod_ref, o_ref, *, gate_i):
    y = _rms(jnp.dot(a_ref[...], w_ref[...], preferred_element_type=F32), g_ref[...])
    o_ref[...] = x_ref[...] + mod_ref[0, gate_i:gate_i + 1, :] * y


def matmul_norm_residual(a, w, l, x, g, mod, mod_row, gate_i, tm, tk):
    m, kdim = a.shape
    nt = m // tm
    ksteps = kdim // tk
    if ksteps == 1:
        return pl.pallas_call(
            functools.partial(_matmul_norm_res_single_kernel, gate_i=gate_i),
            grid=(nt,),
            in_specs=[pl.BlockSpec((tm, kdim), lambda i: (i, 0)),
                      pl.BlockSpec((None, kdim, D_MODEL), lambda i: (l, 0, 0)),
                      pl.BlockSpec((tm, D_MODEL), lambda i: (i, 0)),
                      pl.BlockSpec((1, D_MODEL), lambda i: (0, 0)),
                      pl.BlockSpec((1, N_MOD, D_MODEL), lambda i: (mod_row(i * tm), 0, 0))],
            out_specs=pl.BlockSpec((tm, D_MODEL), lambda i: (i, 0)),
            out_shape=jax.ShapeDtypeStruct((m, D_MODEL), F32),
            compiler_params=_cparams(("parallel",)),
            name="matmul_norm_residual_1k",
        )(a, w, x, g.reshape(1, D_MODEL), mod)

    def kk(i, k):
        return jnp.where(i == nt, ksteps - 1, k)

    def prev(i):
        return jnp.maximum(i - 1, 0)

    return pl.pallas_call(
        functools.partial(_matmul_norm_res_kernel, gate_i=gate_i, nt=nt, ksteps=ksteps, tm=tm),
        grid=(nt + 1, ksteps),
        in_specs=[pl.BlockSpec((tm, tk), lambda i, k: (jnp.minimum(i, nt - 1), kk(i, k))),
                  pl.BlockSpec((None, tk, D_MODEL), lambda i, k: (l, kk(i, k), 0)),
                  pl.BlockSpec((tm, D_MODEL), lambda i, k: (prev(i), 0)),
                  pl.BlockSpec((1, D_MODEL), lambda i, k: (0, 0)),
                  pl.BlockSpec((1, N_MOD, D_MODEL), lambda i, k: (mod_row(prev(i) * tm), 0, 0))],
        out_specs=pl.BlockSpec((tm, D_MODEL), lambda i, k: (prev(i), 0)),
        out_shape=jax.ShapeDtypeStruct((m, D_MODEL), F32),
        scratch_shapes=[pltpu.VMEM((tm, D_MODEL), F32), pltpu.VMEM((tm, D_MODEL), F32)],
        compiler_params=_cparams(("arbitrary", "arbitrary")),
        name="matmul_norm_residual",
    )(a, w, x, g.reshape(1, D_MODEL), mod)


def _merge_kernel(ya_ref, ys_ref, yc_ref, wa_ref, ws_ref, wc_ref, ga_ref, gs_ref, gc_ref, o_ref):
    def branch(y_ref, w_ref, gate_ref):
        return jax.nn.sigmoid(gate_ref[...].astype(F32)) * jnp.dot(
            y_ref[...], w_ref[...], preferred_element_type=F32)

    o_ref[...] = (branch(ya_ref, wa_ref, ga_ref) + branch(ys_ref, ws_ref, gs_ref)
                  + branch(yc_ref, wc_ref, gc_ref)).astype(o_ref.dtype)


def merge_branches(y_att, y_ssd, y_sc, w_att, w_ssd, w_sc, l, proj, tm, tn):
    m = y_att.shape[0]
    gb = P_G // tn
    nb = D_MODEL // tn
    y_spec = pl.BlockSpec((tm, D_MODEL), lambda i, j: (i, 0))
    w_spec = pl.BlockSpec((None, D_MODEL, tn), lambda i, j: (l, 0, j))

    def g_spec(b):
        return pl.BlockSpec((tm, tn), lambda i, j: (i, gb + b * nb + j))

    return pl.pallas_call(
        _merge_kernel,
        grid=(m // tm, nb),
        in_specs=[y_spec, y_spec, y_spec, w_spec, w_spec, w_spec, g_spec(0), g_spec(1), g_spec(2)],
        out_specs=pl.BlockSpec((tm, tn), lambda i, j: (i, j)),
        out_shape=jax.ShapeDtypeStruct((m, D_MODEL), BF16),
        compiler_params=_cparams(("parallel", "parallel")),
        name="merge_branches",
    )(y_att, y_ssd, y_sc, w_att, w_ssd, w_sc, proj, proj, proj)


def _softmax_pv(s, sink, v):
    m = jnp.maximum(jnp.max(s, axis=-1, keepdims=True), sink)
    p = jnp.exp(s - m)
    den = jnp.sum(p, axis=-1, keepdims=True) + jnp.exp(sink - m)
    return jnp.dot(p.astype(BF16), v, preferred_element_type=F32) / den


def _ctx_attn_kernel(sink_ref, q_ref, k_ref, v_ref, o_ref):
    for g in range(N_KV_HEADS):
        k_g = k_ref[:, g * HEAD_DIM:(g + 1) * HEAD_DIM]
        v_g = v_ref[:, g * HEAD_DIM:(g + 1) * HEAD_DIM]
        for r in range(KV_REP):
            h = g * KV_REP + r
            q_h = q_ref[:, h * HEAD_DIM:(h + 1) * HEAD_DIM]
            s = lax.dot_general(q_h, k_g, (((1,), (1,)), ((), ())),
                                preferred_element_type=F32) * ATT_SCALE
            o = _softmax_pv(s, sink_ref[h], v_g)
            o_ref[:, h * HEAD_DIM:(h + 1) * HEAD_DIM] = o.astype(o_ref.dtype)


def context_attention(proj, sink, batch, seq):
    m = batch * seq
    return pl.pallas_call(
        _ctx_attn_kernel,
        grid=(batch,),
        in_specs=[pl.BlockSpec(memory_space=pltpu.SMEM),
                  pl.BlockSpec((seq, Q_W), lambda b: (b, P_Q // Q_W)),
                  pl.BlockSpec((seq, KV_W), lambda b: (b, P_K // KV_W)),
                  pl.BlockSpec((seq, KV_W), lambda b: (b, P_V // KV_W))],
        out_specs=pl.BlockSpec((seq, Q_W), lambda b: (b, 0)),
        out_shape=jax.ShapeDtypeStruct((m, Q_W), BF16),
        compiler_params=_cparams(("parallel",)),
        name="context_attention",
    )(sink, proj, proj, proj)


def _rope_tile(x, cos, sin_signed, even_lane):
    nxt = pltpu.roll(x, LANES - 1, axis=1)
    prv = pltpu.roll(x, 1, axis=1)
    return x * cos + jnp.where(even_lane, nxt, prv) * sin_signed


def _lat_attn_kernel(sink_ref, q_ref, k_ref, v_ref, ck_ref, cv_ref, cosq_ref, sinq_ref,
                     cosk_ref, sink_tab_ref, o_ref, krot_ref, *, seq, past):
    i = pl.program_id(1)
    band = ATT_BLOCK + 2 * WINDOW
    rows = 256

    @pl.when(i == 0)
    def _():
        def body(c, carry):
            r0 = pl.multiple_of(c * rows, rows)
            cos = cosk_ref[pl.ds(r0, rows), :]
            sin = sink_tab_ref[pl.ds(r0, rows), :]
            even = lax.broadcasted_iota(jnp.int32, (rows, LANES), 1) % 2 == 0
            for g in range(N_KV_HEADS):
                x = k_ref[pl.ds(r0, rows), g * HEAD_DIM:(g + 1) * HEAD_DIM].astype(F32)
                krot_ref[pl.ds(r0, rows), g * HEAD_DIM:(g + 1) * HEAD_DIM] = _rope_tile(
                    x, cos, sin, even).astype(BF16)
            return carry

        lax.fori_loop(0, seq // rows, body, 0)

    start = pl.multiple_of(jnp.clip(i * ATT_BLOCK - WINDOW, 0, seq - band), ATT_BLOCK)
    kb = krot_ref[pl.ds(start, band), :]
    vb = v_ref[pl.ds(start, band), :]
    ck = ck_ref[0]
    cv = cv_ref[0]
    stacked = ATT_STACK * ATT_BLOCK
    row = lax.broadcasted_iota(jnp.int32, (stacked, band), 0)
    col = lax.broadcasted_iota(jnp.int32, (stacked, band), 1)
    keep = jnp.abs(start + col - (i * ATT_BLOCK + row % ATT_BLOCK)) <= WINDOW
    head_of_row = lax.broadcasted_iota(jnp.int32, (stacked, 1), 0) // ATT_BLOCK
    cos_q = cosq_ref[...] * (ATT_SCALE * LOG2E)
    sin_q = sinq_ref[...] * (ATT_SCALE * LOG2E)
    even_q = lax.broadcasted_iota(jnp.int32, (ATT_BLOCK, LANES), 1) % 2 == 0
    for g in range(N_KV_HEADS):
        sl = slice(g * HEAD_DIM, (g + 1) * HEAD_DIM)
        kcat = jnp.concatenate([kb[:, sl], ck[:, sl]], axis=0)
        vcat = jnp.concatenate([vb[:, sl], cv[:, sl]], axis=0)
        for h0 in range(g * KV_REP, (g + 1) * KV_REP, ATT_STACK):
            q_parts = []
            sink2 = jnp.zeros((stacked, 1), F32)
            for r in range(ATT_STACK):
                hs = slice((h0 + r) * HEAD_DIM, (h0 + r + 1) * HEAD_DIM)
                q_parts.append(_rope_tile(q_ref[:, hs].astype(F32), cos_q, sin_q, even_q).astype(BF16))
                sink2 = jnp.where(head_of_row == r, sink_ref[h0 + r] * LOG2E, sink2)
            q_g = jnp.concatenate(q_parts, axis=0)
            s = lax.dot_general(q_g, kcat, (((1,), (1,)), ((), ())), preferred_element_type=F32)
            s_loc = jnp.where(keep, s[:, :band], -jnp.inf)
            s_ctx = s[:, band:]
            m = jnp.maximum(jnp.maximum(jnp.max(s_loc, axis=-1, keepdims=True),
                                        jnp.max(s_ctx, axis=-1, keepdims=True)), sink2)
            p_loc = jnp.exp2(s_loc - m)
            p_ctx = jnp.exp2(s_ctx - m)
            den = (jnp.sum(p_loc, axis=-1, keepdims=True) + jnp.sum(p_ctx, axis=-1, keepdims=True)
                   + jnp.exp2(sink2 - m))
            p = jnp.concatenate([p_loc.astype(BF16), p_ctx.astype(BF16)], axis=1)
            o = jnp.dot(p, vcat, preferred_element_type=F32) / den
            for r in range(ATT_STACK):
                o_ref[:, (h0 + r) * HEAD_DIM:(h0 + r + 1) * HEAD_DIM] = o[
                    r * ATT_BLOCK:(r + 1) * ATT_BLOCK].astype(o_ref.dtype)


def latent_attention(proj, ctx_k, ctx_v, sink, cos_t, sin_t, batch, seq):
    m = batch * seq
    nb = seq // ATT_BLOCK
    past = ctx_k.shape[1]
    return pl.pallas_call(
        functools.partial(_lat_attn_kernel, seq=seq, past=past),
        grid=(batch, nb),
        in_specs=[pl.BlockSpec(memory_space=pltpu.SMEM),
                  pl.BlockSpec((ATT_BLOCK, Q_W), lambda b, i: (b * nb + i, P_Q // Q_W)),
                  pl.BlockSpec((seq, KV_W), lambda b, i: (b, P_K // KV_W)),
                  pl.BlockSpec((seq, KV_W), lambda b, i: (b, P_V // KV_W)),
                  pl.BlockSpec((1, past, KV_W), lambda b, i: (b, 0, 0)),
                  pl.BlockSpec((1, past, KV_W), lambda b, i: (b, 0, 0)),
                  pl.BlockSpec((ATT_BLOCK, LANES), lambda b, i: (i, 0)),
                  pl.BlockSpec((ATT_BLOCK, LANES), lambda b, i: (i, 0)),
                  pl.BlockSpec((seq, LANES), lambda b, i: (0, 0)),
                  pl.BlockSpec((seq, LANES), lambda b, i: (0, 0))],
        out_specs=pl.BlockSpec((ATT_BLOCK, Q_W), lambda b, i: (b * nb + i, 0)),
        out_shape=jax.ShapeDtypeStruct((m, Q_W), BF16),
        scratch_shapes=[pltpu.VMEM((seq, KV_W), BF16)],
        compiler_params=_cparams(("parallel", "arbitrary")),
        name="latent_attention",
    )(sink, proj, proj, proj, ctx_k, ctx_v, cos_t, sin_t, cos_t, sin_t)


def rope_tables(seq):
    rows = seq // GRID_W
    row = jnp.repeat(jnp.arange(rows, dtype=F32), GRID_W)
    col = jnp.tile(jnp.arange(GRID_W, dtype=F32), rows)
    n_pairs = HEAD_DIM // 4
    inv = ROPE_THETA ** (-jnp.arange(n_pairs, dtype=F32) / n_pairs)
    ang = jnp.concatenate([row[:, None] * inv, col[:, None] * inv], axis=-1)
    ang = jnp.repeat(ang, 2, axis=-1)
    sign = jnp.where(jnp.arange(HEAD_DIM) % 2 == 0, -1.0, 1.0).astype(F32)
    return jnp.cos(ang), jnp.sin(ang) * sign


def _conv_rolled(prev, main, nxt, w_ref, taps, tile):
    half = (taps - 1) // 2
    sub = 8
    acc = w_ref[half:half + 1, :] * main
    for k in range(taps):
        if k != half:
            acc = acc + w_ref[k:k + 1, :] * pltpu.roll(main, (half - k) % tile, axis=0)

    def edge(window):
        out = None
        for k in range(taps):
            term = w_ref[k:k + 1, :] * window[sub - half + k:2 * sub - half + k]
            out = term if out is None else out + term
        return out

    top = edge(jnp.concatenate([prev, main[:2 * sub]], axis=0))
    bottom = edge(jnp.concatenate([main[tile - 2 * sub:], nxt], axis=0))
    return jnp.concatenate([top, acc[sub:tile - sub], bottom], axis=0)


def _ssm_conv_kernel(prev_ref, main_ref, next_ref, w_ref, b_ref, o_ref, *, nt, tile):
    t = pl.program_id(0) % nt
    prev = jnp.where(t == 0, 0.0, prev_ref[...].astype(F32)[HALO - 8:])
    nxt = jnp.where(t == nt - 1, 0.0, next_ref[...].astype(F32)[:8])
    y = _conv_rolled(prev, main_ref[...].astype(F32), nxt, w_ref, SSM_CONV, tile) + b_ref[...]
    o_ref[...] = _silu(y).astype(o_ref.dtype)


def _halo_specs(tile, width, col0, m):
    hb = tile // HALO
    last = m // HALO - 1
    prev = pl.BlockSpec((HALO, width), lambda i, j: (jnp.maximum(i * hb - 1, 0), col0 + j))
    main = pl.BlockSpec((tile, width), lambda i, j: (i, col0 + j))
    nxt = pl.BlockSpec((HALO, width), lambda i, j: (jnp.minimum((i + 1) * hb, last), col0 + j))
    return prev, main, nxt


def ssm_conv(proj, w, b, seq, tile):
    m = proj.shape[0]
    width = 1024
    nt = seq // tile
    prev, main, nxt = _halo_specs(tile, width, P_XS // width, m)
    return pl.pallas_call(
        functools.partial(_ssm_conv_kernel, nt=nt, tile=tile),
        grid=(m // tile, SSM_CONV_CH // width),
        in_specs=[prev, main, nxt,
                  pl.BlockSpec((SSM_CONV, width), lambda i, j: (0, j)),
                  pl.BlockSpec((1, width), lambda i, j: (0, j))],
        out_specs=pl.BlockSpec((tile, width), lambda i, j: (i, j)),
        out_shape=jax.ShapeDtypeStruct((m, SSM_CONV_CH), BF16),
        compiler_params=_cparams(("parallel", "parallel")),
        name="ssm_conv",
    )(proj, proj, proj, w, b.reshape(1, SSM_CONV_CH))


def _short_conv_kernel(b_ref, cp_ref, cm_ref, cn_ref, hp_ref, hm_ref, hn_ref, w_ref, o_ref,
                       *, nt, tile):
    t = pl.program_id(0) % nt

    def prod(c_ref, h_ref):
        return c_ref[...].astype(F32) * h_ref[...].astype(F32)

    prev = jnp.where(t == 0, 0.0, prod(cp_ref, hp_ref)[HALO - 8:])
    nxt = jnp.where(t == nt - 1, 0.0, prod(cn_ref, hn_ref)[:8])
    y = _conv_rolled(prev, prod(cm_ref, hm_ref), nxt, w_ref, SC_CONV, tile)
    o_ref[...] = (b_ref[...].astype(F32) * y).astype(o_ref.dtype)


def short_conv(proj, w, seq, tile):
    m = proj.shape[0]
    width = 1024
    nt = seq // tile
    cp, cm, cn = _halo_specs(tile, width, P_SCC // width, m)
    hp, hm, hn = _halo_specs(tile, width, P_SCH // width, m)
    return pl.pallas_call(
        functools.partial(_short_conv_kernel, nt=nt, tile=tile),
        grid=(m // tile, SC_WIDTH // width),
        in_specs=[pl.BlockSpec((tile, width), lambda i, j: (i, P_SCB // width + j)),
                  cp, cm, cn, hp, hm, hn,
                  pl.BlockSpec((SC_CONV, width), lambda i, j: (0, j))],
        out_specs=pl.BlockSpec((tile, width), lambda i, j: (i, j)),
        out_shape=jax.ShapeDtypeStruct((m, SC_WIDTH), BF16),
        compiler_params=_cparams(("parallel", "parallel")),
        name="short_conv",
    )(proj, proj, proj, proj, proj, proj, proj, w)


def _split3(a):
    a1 = a.astype(BF16)
    r1 = a - a1.astype(F32)
    a2 = r1.astype(BF16)
    a3 = (r1 - a2.astype(F32)).astype(BF16)
    return a1, a2, a3


def _softplus(x):
    return jnp.maximum(x, 0.0) + jnp.log1p(jnp.exp(-jnp.abs(x)))


def _ssd_kernel(*refs, backward, has_h0, emit_state, nprob):
    refs = list(refs)
    x_ref, b_ref, c_ref, dt_ref, dtb_ref, alog_ref, sel_ref = refs[:7]
    del refs[:7]
    h0_ref = refs.pop(0) if has_h0 else None
    if backward:
        yf_ref, z_ref, d_ref, gn_ref = refs[:4]
        del refs[:4]
    y_ref = refs.pop(0)
    hout_ref = refs.pop(0) if emit_state else None
    st_ref = refs.pop(0)
    acc_ref = refs.pop(0) if backward else y_ref
    q = SSM_CHUNK
    gw = (SSM_HEADS // SSM_GROUPS) * SSM_HEAD_DIM

    @pl.when(pl.program_id(1) == 0)
    def _():
        for p in range(nprob):
            for g in range(SSM_GROUPS):
                if has_h0:
                    st_ref[p, g] = h0_ref[p, 0, g * gw:(g + 1) * gw, :].T
                else:
                    st_ref[p, g] = jnp.zeros((SSM_STATE, gw), F32)

    lo = SSM_HEADS if backward else 0
    neg_a = -LOG2E * jnp.exp(alog_ref[...])
    ii = lax.broadcasted_iota(jnp.int32, (q, q), 0)
    jj = lax.broadcasted_iota(jnp.int32, (q, q), 1)
    tri = (jj >= ii) if backward else (jj <= ii)
    tri_b = jnp.where(tri, 1.0, 0.0).astype(BF16)
    lane_lo = lax.broadcasted_iota(jnp.int32, (1, LANES), 1) < SSM_HEAD_DIM
    heads_per_group = SSM_HEADS // SSM_GROUPS
    pairs_per_group = heads_per_group // 2
    edge = 0 if backward else q - 1
    probs = range(nprob)

    pre = []
    for p in probs:
        dt = _softplus(dt_ref[p] + dtb_ref[...])
        a2 = dt * neg_a
        u2 = None
        for piece in _split3(a2):
            d = jnp.dot(tri_b, piece, preferred_element_type=F32)
            u2 = d if u2 is None else u2 + d
        v_t = (u2 - jnp.log2(dt)).T
        t_row = u2[edge:edge + 1, :]
        w_c = dt * jnp.exp2(t_row - u2)
        w_hi = w_c.astype(BF16)
        w_lo = (w_c - w_hi.astype(F32)).astype(BF16)
        w_exp = (jnp.dot(w_hi, sel_ref[...], preferred_element_type=F32)
                 + jnp.dot(w_lo, sel_ref[...], preferred_element_type=F32))
        pre.append((u2, v_t, t_row, w_exp))

    for g in range(SSM_GROUPS):
        grp = []
        for p in probs:
            b_g = b_ref[p, :, g * SSM_STATE:(g + 1) * SSM_STATE]
            c_g = c_ref[p, :, g * SSM_STATE:(g + 1) * SSM_STATE]
            cb = lax.dot_general(c_g, b_g, (((1,), (1,)), ((), ())), preferred_element_type=F32)
            h_t = st_ref[p, g]
            y_off = jnp.dot(c_g, h_t.astype(BF16), preferred_element_type=F32)
            grp.append((b_g, cb, h_t, y_off, [], []))
        for pr in range(pairs_per_group):
            h_a = g * heads_per_group + 2 * pr
            ca = lo + h_a
            cbk = ca + 1
            pc = (h_a // 2) * LANES
            for p in probs:
                u2, v_t, t_row, w_exp = pre[p]
                b_g, cb, h_t, y_off, xs_parts, dec_parts = grp[p]
                x_b = x_ref[p, :, pc:pc + LANES]
                up = jnp.where(lane_lo, u2[:, ca:ca + 1], u2[:, cbk:cbk + 1])
                tp = jnp.where(lane_lo, t_row[:, ca:ca + 1], t_row[:, cbk:cbk + 1])
                ys = []
                for col in (ca, cbk):
                    seg = u2[:, col:col + 1] - v_t[col:col + 1, :]
                    decay_dt = jnp.exp2(jnp.where(tri, seg, -jnp.inf))
                    ys.append(jnp.dot((cb * decay_dt).astype(BF16), x_b, preferred_element_type=F32))
                acc_ref[p, :, pc:pc + LANES] = (jnp.where(lane_lo, ys[0], ys[1])
                                                + jnp.exp2(up) * y_off[:, pr * LANES:(pr + 1) * LANES])
                xs_parts.append((x_b.astype(F32) * w_exp[:, pc:pc + LANES]).astype(BF16))
                dec_parts.append(jnp.exp2(tp))
        for p in probs:
            b_g, cb, h_t, y_off, xs_parts, dec_parts = grp[p]
            xs_g = jnp.concatenate(xs_parts, axis=1)
            dec_g = jnp.concatenate(dec_parts, axis=1)
            upd = lax.dot_general(b_g, xs_g, (((0,), (0,)), ((), ())), preferred_element_type=F32)
            st_ref[p, g] = dec_g * h_t + upd

    if backward:
        for p in probs:
            y = acc_ref[p] + yf_ref[p] + x_ref[p].astype(F32) * d_ref[...]
            y = y * _silu(z_ref[p].astype(F32))
            y_ref[p] = _rms(y, gn_ref[...]).astype(y_ref.dtype)

    if emit_state:
        @pl.when(pl.program_id(1) == pl.num_programs(1) - 1)
        def _():
            for p in range(nprob):
                for g in range(SSM_GROUPS):
                    hout_ref[p, g * gw:(g + 1) * gw, :] = st_ref[p, g].T


def _head_selector(backward):
    lane = jnp.arange(LANES)[:, None]
    head = jnp.arange(SSM_INNER)[None, :] // SSM_HEAD_DIM
    return (lane == head + (SSM_HEADS if backward else 0)).astype(BF16)


def ssd_scan(xbc, dt_raw, dt_bias, a_log, h0, batch, seq, emit_state, finish=None):
    backward = finish is not None
    m = batch * seq
    nc = seq // SSM_CHUNK
    gw = SSM_INNER // SSM_GROUPS
    hp = SSM_HEADS * SSM_HEAD_DIM

    nprob = SSD_PROBLEMS
    assert batch % nprob == 0

    def chunk(c):
        return nc - 1 - c if backward else c

    def seq_spec(width, col):
        return pl.BlockSpec((nprob, SSM_CHUNK, width), lambda b, c: (b, chunk(c), col))

    def by_seq(a):
        return a.reshape(batch, seq, a.shape[-1])

    vec = pl.BlockSpec((1, SSM_INNER), lambda b, c: (0, 0))
    lane_vec = pl.BlockSpec((1, LANES), lambda b, c: (0, 0))
    in_specs = [seq_spec(SSM_INNER, 0),
                seq_spec(SSM_BC, SSM_INNER // SSM_BC),
                seq_spec(SSM_BC, SSM_INNER // SSM_BC + 1),
                seq_spec(LANES, 0),
                lane_vec, lane_vec,
                pl.BlockSpec((LANES, SSM_INNER), lambda b, c: (0, 0))]
    xbc3 = by_seq(xbc)
    args = [xbc3, xbc3, xbc3, by_seq(dt_raw), dt_bias, a_log, _head_selector(backward)]
    if h0 is not None:
        state, slab = h0
        in_specs.append(pl.BlockSpec((nprob, 1, hp, SSM_STATE), lambda b, c: (b, slab, 0, 0)))
        args.append(state)
    scratch = [pltpu.VMEM((nprob, SSM_GROUPS, SSM_STATE, gw), F32)]
    if backward:
        y_fwd, proj, d_exp, norm_g = finish
        in_specs += [seq_spec(SSM_INNER, 0), seq_spec(SSM_INNER, P_Z // SSM_INNER), vec, vec]
        args += [by_seq(y_fwd), by_seq(proj), d_exp, norm_g.reshape(1, SSM_INNER)]
        scratch.append(pltpu.VMEM((nprob, SSM_CHUNK, SSM_INNER), F32))
    out_specs = [seq_spec(SSM_INNER, 0)]
    out_shape = [jax.ShapeDtypeStruct((batch, seq, SSM_INNER), BF16 if backward else F32)]
    if emit_state:
        out_specs.append(pl.BlockSpec((nprob, hp, SSM_STATE), lambda b, c: (b, 0, 0)))
        out_shape.append(jax.ShapeDtypeStruct((batch, hp, SSM_STATE), F32))
    out = pl.pallas_call(
        functools.partial(_ssd_kernel, backward=backward, has_h0=h0 is not None,
                          emit_state=emit_state, nprob=nprob),
        grid=(batch // nprob, nc),
        in_specs=in_specs,
        out_specs=out_specs,
        out_shape=out_shape,
        scratch_shapes=scratch,
        compiler_params=_cparams(("parallel", "arbitrary")),
        name="ssd_scan_bwd" if backward else "ssd_scan_fwd",
    )(*args)
    y = out[0].reshape(m, SSM_INNER)
    return (y, out[1]) if emit_state else (y, None)


def _pad_lanes(v):
    flat = v.reshape(1, 2 * SSM_HEADS).astype(F32)
    return jnp.pad(flat, ((0, 0), (0, LANES - 2 * SSM_HEADS)))


def _prepare_weights(w_in, w_att_out, w_ssd_out, w_sc_out, w_o, w_gate_up, w_down):
    def cols(o, n):
        return w_in[:, :, o:o + n]

    def reordered(l):
        w = w_in[l]
        parts = [(_O_Q, Q_W), (_O_Z, SSM_INNER), (_O_SCB, SC_WIDTH), (_O_SCC, SC_WIDTH),
                 (_O_SCH, SC_WIDTH), (_O_G, N_BRANCH * D_MODEL), (_O_XS, SSM_INNER),
                 (_O_BM, SSM_BC), (_O_CM, SSM_BC), (_O_K, KV_W), (_O_V, KV_W)]
        return jnp.concatenate([w[:, o:o + n] for o, n in parts], axis=1).astype(BF16)[None]

    w_main = [reordered(l) for l in range(DEPTH)]
    w_dt = jnp.pad(cols(_O_DT, 2 * SSM_HEADS), ((0, 0), (0, 0), (0, LANES - 2 * SSM_HEADS))).astype(BF16)
    return dict(w_main=w_main, w_dt=w_dt,
                w_att=w_att_out.astype(BF16), w_ssd=w_ssd_out.astype(BF16),
                w_sc=w_sc_out.astype(BF16), w_o=w_o.astype(BF16),
                w_gate_up=w_gate_up.astype(BF16), w_down=w_down.astype(BF16))


def _tile_plan(seq):
    return dict(
        in_proj=(1024, 1024),
        swiglu=(1024, 512),
        down=(512, D_FF // 4),
        w_o=(512, D_MODEL),
        merge=(1024, 256),
        conv=min(seq, 512))


def _trunk_layer(x, mod, mod_row, l, lw, p, batch, seq, latent):
    plan = _tile_plan(seq)
    outs = in_proj(x, p['g_pre_mix'], mod, mod_row, lw['w_main'][l], lw['w_dt'], l, *plan['in_proj'],
                   emit_kv=latent is None)
    proj, dt_raw = outs[0], outs[1]
    if latent is None:
        y_att = context_attention(proj, p['sink'], batch, seq)
        h0_f = h0_b = None
    else:
        ctx_k, ctx_v, state, rope = latent
        y_att = latent_attention(proj, ctx_k.reshape(batch, -1, KV_W).astype(BF16),
                                 ctx_v.reshape(batch, -1, KV_W).astype(BF16), p['sink'],
                                 rope[0], rope[1], batch, seq)
        h0_f, h0_b = (state, 2 * l), (state, 2 * l + 1)
    xbc = ssm_conv(proj, p['ssm_conv_w'], p['ssm_conv_b'], seq, plan['conv'])
    dt_bias = _pad_lanes(p['ssm_dt_bias'])
    a_log = _pad_lanes(p['ssm_a_log'])
    d_exp = jnp.repeat(p['ssm_d'].astype(F32), SSM_HEAD_DIM).reshape(1, SSM_INNER)
    emit_state = latent is None
    y_f, h_f = ssd_scan(xbc, dt_raw, dt_bias, a_log, h0_f, batch, seq, emit_state)
    y_ssd, h_b = ssd_scan(xbc, dt_raw, dt_bias, a_log, h0_b, batch, seq, emit_state,
                          finish=(y_f, proj, d_exp, p['ssm_norm_g']))
    y_sc = short_conv(proj, p['sc_conv_w'], seq, plan['conv'])
    merged = merge_branches(y_att, y_ssd, y_sc, lw['w_att'], lw['w_ssd'], lw['w_sc'], l, proj,
                            *plan['merge'])
    x = matmul_norm_residual(merged, lw['w_o'], l, x, p['g_post_mix'], mod, mod_row, 2, *plan['w_o'])
    act = norm_swiglu(x, p['g_pre_ffn'], mod, mod_row, lw['w_gate_up'], l, *plan['swiglu'])
    x = matmul_norm_residual(act, lw['w_down'], l, x, p['g_post_ffn'], mod, mod_row, 5, *plan['down'])
    new_ctx = None
    if latent is None:
        kv = outs[2]
        k = kv[:, :KV_W].reshape(batch, seq, N_KV_HEADS, HEAD_DIM)
        v = kv[:, KV_W:].reshape(batch, seq, N_KV_HEADS, HEAD_DIM)
        new_ctx = (k, v, h_f, h_b)
    return x, new_ctx


def kernel(x_prompt, x_sample, cache_k, cache_v, state_ssm, c, c_ctx, w_mod, b_mod, g_pre_mix, w_in, sink, ssm_conv_w, ssm_conv_b, ssm_dt_bias, ssm_a_log, ssm_d, ssm_norm_g, sc_conv_w, w_att_out, w_ssd_out, w_sc_out, w_o, g_post_mix, g_pre_ffn, w_gate_up, w_down, g_post_ffn):
    batch, seq, _ = x_prompt.shape
    dec_batch, dec_seq, _ = x_sample.shape
    cond = jnp.concatenate([c_ctx[None, :], c, jnp.zeros((8 - 1 - dec_batch, D_MODEL), F32)], axis=0)
    mod = modulation(cond, w_mod, b_mod).reshape(DEPTH, 8, N_MOD, D_MODEL)
    rope = rope_tables(dec_seq)
    y_prompt = x_prompt.reshape(batch * seq, D_MODEL)
    y_sample = x_sample.reshape(dec_batch * dec_seq, D_MODEL)
    lw = _prepare_weights(w_in, w_att_out, w_ssd_out, w_sc_out, w_o, w_gate_up, w_down)
    state_in = state_ssm.reshape(dec_batch, 2 * DEPTH, SSM_HEADS * SSM_HEAD_DIM, SSM_STATE)
    ks_out, vs_out, ss_out = [], [], []
    for l in range(DEPTH):
        p = {'g_pre_mix': g_pre_mix[l], 'sink': sink[l], 'ssm_conv_w': ssm_conv_w[l],
             'ssm_conv_b': ssm_conv_b[l], 'ssm_dt_bias': ssm_dt_bias[l], 'ssm_a_log': ssm_a_log[l],
             'ssm_d': ssm_d[l], 'ssm_norm_g': ssm_norm_g[l], 'sc_conv_w': sc_conv_w[l],
             'g_post_mix': g_post_mix[l], 'g_pre_ffn': g_pre_ffn[l], 'g_post_ffn': g_post_ffn[l]}
        y_prompt, (k_l, v_l, h_f, h_b) = _trunk_layer(
            y_prompt, mod[l], lambda r: 0, l, lw, p, batch, seq, None)
        ks_out.append(k_l)
        vs_out.append(v_l)
        ss_out += [h_f, h_b]
        y_sample, _ = _trunk_layer(
            y_sample, mod[l], lambda r: 1 + r // dec_seq, l, lw, p, dec_batch, dec_seq,
            (cache_k[:, l], cache_v[:, l], state_in, rope))
    new_state = jnp.stack(ss_out, axis=1).reshape(
        batch, DEPTH, 2, SSM_HEADS, SSM_HEAD_DIM, SSM_STATE)
    return (y_prompt.reshape(batch, seq, D_MODEL), y_sample.reshape(dec_batch, dec_seq, D_MODEL),
            jnp.stack(ks_out, axis=1), jnp.stack(vs_out, axis=1), new_state)
```

```python
import functools

import jax
import jax.numpy as jnp
from jax import lax
from jax.experimental import pallas as pl
from jax.experimental.pallas import tpu as pltpu

F32 = jnp.float32
BF16 = jnp.bfloat16

D_MODEL = 2048
DEPTH = 2
GRID_W = 64
N_HEADS = 16
N_KV_HEADS = 4
KV_REP = N_HEADS // N_KV_HEADS
HEAD_DIM = 128
Q_W = N_HEADS * HEAD_DIM
KV_W = N_KV_HEADS * HEAD_DIM
WINDOW = 128
ATT_BLOCK = 128
ATT_SCALE = HEAD_DIM ** -0.5
ROPE_THETA = 10000.0
SSM_HEADS = 32
SSM_HEAD_DIM = 64
SSM_INNER = SSM_HEADS * SSM_HEAD_DIM
SSM_GROUPS = 4
SSM_STATE = 128
SSM_BC = SSM_GROUPS * SSM_STATE
SSM_CONV_CH = SSM_INNER + 2 * SSM_BC
SSM_CONV = 5
SSM_CHUNK = 128
SC_WIDTH = D_MODEL
SC_CONV = 3
N_BRANCH = 3
D_FF = ((8 * D_MODEL + 3 * 256 - 1) // (3 * 256)) * 256
N_MOD = 6
EPS = 1e-6
LOG2E = 1.4426950408889634

_O_Q = 0
_O_K = _O_Q + Q_W
_O_V = _O_K + KV_W
_O_Z = _O_V + KV_W
_O_XS = _O_Z + SSM_INNER
_O_BM = _O_XS + SSM_INNER
_O_CM = _O_BM + SSM_BC
_O_DT = _O_CM + SSM_BC
_O_SCB = _O_DT + 2 * SSM_HEADS
_O_SCC = _O_SCB + SC_WIDTH
_O_SCH = _O_SCC + SC_WIDTH
_O_G = _O_SCH + SC_WIDTH
D_IN = _O_G + N_BRANCH * D_MODEL

P_Q = 0
P_Z = 2048
P_SCB = 4096
P_SCC = 6144
P_SCH = 8192
P_G = 10240
P_XS = 16384
P_BM = 18432
P_CM = 18944
P_K = 19456
P_V = 19968
P_N = 20480

SSD_PROBLEMS = 2
ATT_STACK = KV_REP
LANES = 128
HALO = 16
VMEM_LIMIT = 56 * 1024 * 1024


def _cparams(sem):
    return pltpu.CompilerParams(dimension_semantics=sem, vmem_limit_bytes=VMEM_LIMIT)


def _rms(x, g):
    return x * lax.rsqrt(jnp.mean(x * x, axis=-1, keepdims=True) + EPS) * g


def _silu(x):
    return x * jax.nn.sigmoid(x)


def _mod_kernel(c_ref, w_ref, b_ref, o_ref):
    a = _silu(c_ref[...]).astype(BF16)
    o_ref[0] = jnp.dot(a, w_ref[0].astype(BF16), preferred_element_type=F32) + b_ref[0]


def modulation(cond, w_mod, b_mod):
    rows = cond.shape[0]
    n = N_MOD * D_MODEL
    tn = 1024
    return pl.pallas_call(
        _mod_kernel,
        grid=(DEPTH, n // tn),
        in_specs=[pl.BlockSpec((rows, D_MODEL), lambda l, j: (0, 0)),
                  pl.BlockSpec((1, D_MODEL, tn), lambda l, j: (l, 0, j)),
                  pl.BlockSpec((1, 1, tn), lambda l, j: (l, 0, j))],
        out_specs=pl.BlockSpec((1, rows, tn), lambda l, j: (l, 0, j)),
        out_shape=jax.ShapeDtypeStruct((DEPTH, rows, n), F32),
        compiler_params=_cparams(("parallel", "parallel")),
        name="modulation",
    )(cond, w_mod, b_mod.reshape(DEPTH, 1, n))


def _norm_mod(x_ref, g_ref, mod_ref, shift_i, scale_i):
    y = _rms(x_ref[...], g_ref[...])
    return y * (1.0 + mod_ref[0, scale_i:scale_i + 1, :]) + mod_ref[0, shift_i:shift_i + 1, :]


def _by_parity(i, fn, buf_a, buf_b):
    @pl.when(i % 2 == 0)
    def _():
        fn(buf_a, buf_b)

    @pl.when(i % 2 == 1)
    def _():
        fn(buf_b, buf_a)


def _in_proj_kernel(x_ref, g_ref, mod_ref, w_ref, wdt_ref, o_ref, odt_ref, *rest, emit_kv):
    okv_ref, h_ref = rest if emit_kv else (None, rest[0])

    @pl.when(pl.program_id(1) == 0)
    def _():
        h = _norm_mod(x_ref, g_ref, mod_ref, 0, 1).astype(BF16)
        h_ref[...] = h
        odt_ref[...] = jnp.dot(h, wdt_ref[...], preferred_element_type=F32)

    res = jnp.dot(h_ref[...], w_ref[...], preferred_element_type=F32)
    o_ref[...] = res.astype(o_ref.dtype)
    if emit_kv:
        @pl.when(pl.program_id(1) == pl.num_programs(1) - 1)
        def _():
            okv_ref[...] = res


def in_proj(x, g, mod, mod_row, w, w_dt, l, tm, tn, emit_kv):
    m = x.shape[0]
    n = w.shape[2]
    assert w.shape[0] == 1 and P_N - P_K == tn and 2 * KV_W == tn
    out_specs = [pl.BlockSpec((tm, tn), lambda i, j: (i, j)),
                 pl.BlockSpec((tm, LANES), lambda i, j: (i, 0))]
    out_shape = [jax.ShapeDtypeStruct((m, n), BF16), jax.ShapeDtypeStruct((m, LANES), F32)]
    if emit_kv:
        out_specs.append(pl.BlockSpec((tm, tn), lambda i, j: (i, 0)))
        out_shape.append(jax.ShapeDtypeStruct((m, tn), F32))
    return pl.pallas_call(
        functools.partial(_in_proj_kernel, emit_kv=emit_kv),
        grid=(m // tm, n // tn),
        in_specs=[pl.BlockSpec((tm, D_MODEL), lambda i, j: (i, 0)),
                  pl.BlockSpec((1, D_MODEL), lambda i, j: (0, 0)),
                  pl.BlockSpec((1, N_MOD, D_MODEL), lambda i, j: (mod_row(i * tm), 0, 0)),
                  pl.BlockSpec((None, D_MODEL, tn), lambda i, j: (0, 0, j)),
                  pl.BlockSpec((None, D_MODEL, LANES), lambda i, j: (l, 0, 0))],
        out_specs=out_specs,
        out_shape=out_shape,
        scratch_shapes=[pltpu.VMEM((tm, D_MODEL), BF16)],
        compiler_params=_cparams(("parallel", "arbitrary")),
        name="in_proj",
    )(x, g.reshape(1, D_MODEL), mod, w, w_dt)


def _nm_swiglu_kernel(x_ref, g_ref, mod_ref, wg_ref, wu_ref, o_ref, h_ref):
    @pl.when(pl.program_id(1) == 0)
    def _():
        h_ref[...] = _norm_mod(x_ref, g_ref, mod_ref, 3, 4).astype(BF16)

    h = h_ref[...]
    a = jnp.dot(h, wg_ref[...], preferred_element_type=F32)
    b = jnp.dot(h, wu_ref[...], preferred_element_type=F32)
    o_ref[...] = (_silu(a) * b).astype(o_ref.dtype)


def norm_swiglu(x, g, mod, mod_row, w_gate_up, l, tm, tn):
    m = x.shape[0]
    nb = D_FF // tn
    return pl.pallas_call(
        _nm_swiglu_kernel,
        grid=(m // tm, nb),
        in_specs=[pl.BlockSpec((tm, D_MODEL), lambda i, j: (i, 0)),
                  pl.BlockSpec((1, D_MODEL), lambda i, j: (0, 0)),
                  pl.BlockSpec((1, N_MOD, D_MODEL), lambda i, j: (mod_row(i * tm), 0, 0)),
                  pl.BlockSpec((None, D_MODEL, tn), lambda i, j: (l, 0, j)),
                  pl.BlockSpec((None, D_MODEL, tn), lambda i, j: (l, 0, j + nb))],
        out_specs=pl.BlockSpec((tm, tn), lambda i, j: (i, j)),
        out_shape=jax.ShapeDtypeStruct((m, D_FF), BF16),
        scratch_shapes=[pltpu.VMEM((tm, D_MODEL), BF16)],
        compiler_params=_cparams(("parallel", "arbitrary")),
        name="norm_swiglu",
    )(x, g.reshape(1, D_MODEL), mod, w_gate_up, w_gate_up)


def _matmul_norm_res_kernel(a_ref, w_ref, x_ref, g_ref, mod_ref, o_ref, acca_ref, accb_ref,
                            *, gate_i, nt, ksteps, tm):
    i = pl.program_id(0)
    k = pl.program_id(1)
    rows = tm // ksteps
    r0 = pl.multiple_of(k * rows, rows)

    @pl.when((i == 0) & (k == 0))
    def _():
        acca_ref[...] = jnp.zeros_like(acca_ref)
        accb_ref[...] = jnp.zeros_like(accb_ref)

    def row(acc_cur, acc_prev):
        def epilogue():
            y = _rms(acc_prev[pl.ds(r0, rows), :], g_ref[...])
            o_ref[pl.ds(r0, rows), :] = (x_ref[pl.ds(r0, rows), :]
                                         + mod_ref[0, gate_i:gate_i + 1, :] * y)

        @pl.when(i < nt)
        def _():
            prod = jnp.dot(a_ref[...], w_ref[...], preferred_element_type=F32)
            acc_cur[...] = jnp.where(k == 0, prod, acc_cur[...] + prod)
            epilogue()

        @pl.when(i == nt)
        def _():
            epilogue()

    _by_parity(i, row, acca_ref, accb_ref)


def _matmul_norm_res_single_kernel(a_ref, w_ref, x_ref, g_ref, mod_ref, o_ref, *, gate_i):
    y = _rms(jnp.dot(a_ref[...], w_ref[...], preferred_element_type=F32), g_ref[...])
    o_ref[...] = x_ref[...] + mod_ref[0, gate_i:gate_i + 1, :] * y


def matmul_norm_residual(a, w, l, x, g, mod, mod_row, gate_i, tm, tk):
    m, kdim = a.shape
    nt = m // tm
    ksteps = kdim // tk
    if ksteps == 1:
        return pl.pallas_call(
            functools.partial(_matmul_norm_res_single_kernel, gate_i=gate_i),
            grid=(nt,),
            in_specs=[pl.BlockSpec((tm, kdim), lambda i: (i, 0)),
                      pl.BlockSpec((None, kdim, D_MODEL), lambda i: (l, 0, 0)),
                      pl.BlockSpec((tm, D_MODEL), lambda i: (i, 0)),
                      pl.BlockSpec((1, D_MODEL), lambda i: (0, 0)),
                      pl.BlockSpec((1, N_MOD, D_MODEL), lambda i: (mod_row(i * tm), 0, 0))],
            out_specs=pl.BlockSpec((tm, D_MODEL), lambda i: (i, 0)),
            out_shape=jax.ShapeDtypeStruct((m, D_MODEL), F32),
            compiler_params=_cparams(("parallel",)),
            name="matmul_norm_residual_1k",
        )(a, w, x, g.reshape(1, D_MODEL), mod)

    def kk(i, k):
        return jnp.where(i == nt, ksteps - 1, k)

    def prev(i):
        return jnp.maximum(i - 1, 0)

    return pl.pallas_call(
        functools.partial(_matmul_norm_res_kernel, gate_i=gate_i, nt=nt, ksteps=ksteps, tm=tm),
        grid=(nt + 1, ksteps),
        in_specs=[pl.BlockSpec((tm, tk), lambda i, k: (jnp.minimum(i, nt - 1), kk(i, k))),
                  pl.BlockSpec((None, tk, D_MODEL), lambda i, k: (l, kk(i, k), 0)),
                  pl.BlockSpec((tm, D_MODEL), lambda i, k: (prev(i), 0)),
                  pl.BlockSpec((1, D_MODEL), lambda i, k: (0, 0)),
                  pl.BlockSpec((1, N_MOD, D_MODEL), lambda i, k: (mod_row(prev(i) * tm), 0, 0))],
        out_specs=pl.BlockSpec((tm, D_MODEL), lambda i, k: (prev(i), 0)),
        out_shape=jax.ShapeDtypeStruct((m, D_MODEL), F32),
        scratch_shapes=[pltpu.VMEM((tm, D_MODEL), F32), pltpu.VMEM((tm, D_MODEL), F32)],
        compiler_params=_cparams(("arbitrary", "arbitrary")),
        name="matmul_norm_residual",
    )(a, w, x, g.reshape(1, D_MODEL), mod)


def _merge_kernel(ya_ref, ys_ref, yc_ref, wa_ref, ws_ref, wc_ref, ga_ref, gs_ref, gc_ref, o_ref):
    def branch(y_ref, w_ref, gate_ref):
        return jax.nn.sigmoid(gate_ref[...].astype(F32)) * jnp.dot(
            y_ref[...], w_ref[...], preferred_element_type=F32)

    o_ref[...] = (branch(ya_ref, wa_ref, ga_ref) + branch(ys_ref, ws_ref, gs_ref)
                  + branch(yc_ref, wc_ref, gc_ref)).astype(o_ref.dtype)


def merge_branches(y_att, y_ssd, y_sc, w_att, w_ssd, w_sc, l, proj, tm, tn):
    m = y_att.shape[0]
    gb = P_G // tn
    nb = D_MODEL // tn
    y_spec = pl.BlockSpec((tm, D_MODEL), lambda i, j: (i, 0))
    w_spec = pl.BlockSpec((None, D_MODEL, tn), lambda i, j: (l, 0, j))

    def g_spec(b):
        return pl.BlockSpec((tm, tn), lambda i, j: (i, gb + b * nb + j))

    return pl.pallas_call(
        _merge_kernel,
        grid=(m // tm, nb),
        in_specs=[y_spec, y_spec, y_spec, w_spec, w_spec, w_spec, g_spec(0), g_spec(1), g_spec(2)],
        out_specs=pl.BlockSpec((tm, tn), lambda i, j: (i, j)),
        out_shape=jax.ShapeDtypeStruct((m, D_MODEL), BF16),
        compiler_params=_cparams(("parallel", "parallel")),
        name="merge_branches",
    )(y_att, y_ssd, y_sc, w_att, w_ssd, w_sc, proj, proj, proj)


def _ctx_attn_kernel(sink_ref, q_ref, k_ref, v_ref, o_ref):
    seq = q_ref.shape[0]
    head_of_row = lax.broadcasted_iota(jnp.int32, (KV_REP * seq, 1), 0) // seq
    for g in range(N_KV_HEADS):
        k_g = k_ref[:, g * HEAD_DIM:(g + 1) * HEAD_DIM]
        v_g = v_ref[:, g * HEAD_DIM:(g + 1) * HEAD_DIM]
        q_g = jnp.concatenate([q_ref[:, (g * KV_REP + r) * HEAD_DIM:(g * KV_REP + r + 1) * HEAD_DIM]
                               for r in range(KV_REP)], axis=0)
        sink2 = jnp.zeros((KV_REP * seq, 1), F32)
        for r in range(KV_REP):
            sink2 = jnp.where(head_of_row == r, sink_ref[g * KV_REP + r] * LOG2E, sink2)
        s = lax.dot_general(q_g, k_g, (((1,), (1,)), ((), ())),
                            preferred_element_type=F32) * (ATT_SCALE * LOG2E)
        m = jnp.maximum(jnp.max(s, axis=-1, keepdims=True), sink2)
        p = jnp.exp2(s - m)
        den = jnp.sum(p, axis=-1, keepdims=True) + jnp.exp2(sink2 - m)
        o = jnp.dot(p.astype(BF16), v_g, preferred_element_type=F32) / den
        for r in range(KV_REP):
            h = g * KV_REP + r
            o_ref[:, h * HEAD_DIM:(h + 1) * HEAD_DIM] = o[r * seq:(r + 1) * seq].astype(o_ref.dtype)


def context_attention(proj, sink, batch, seq):
    m = batch * seq
    return pl.pallas_call(
        _ctx_attn_kernel,
        grid=(batch,),
        in_specs=[pl.BlockSpec(memory_space=pltpu.SMEM),
                  pl.BlockSpec((seq, Q_W), lambda b: (b, P_Q // Q_W)),
                  pl.BlockSpec((seq, KV_W), lambda b: (b, P_K // KV_W)),
                  pl.BlockSpec((seq, KV_W), lambda b: (b, P_V // KV_W))],
        out_specs=pl.BlockSpec((seq, Q_W), lambda b: (b, 0)),
        out_shape=jax.ShapeDtypeStruct((m, Q_W), BF16),
        compiler_params=_cparams(("parallel",)),
        name="context_attention",
    )(sink, proj, proj, proj)


def _rope_tile(x, cos, sin_signed, even_lane):
    nxt = pltpu.roll(x, LANES - 1, axis=1)
    prv = pltpu.roll(x, 1, axis=1)
    return x * cos + jnp.where(even_lane, nxt, prv) * sin_signed


def _lat_attn_kernel(sink_ref, q_ref, k_ref, v_ref, ck_ref, cv_ref, cosq_ref, sinq_ref,
                     cosk_ref, sink_tab_ref, o_ref, krot_ref, *, seq, past):
    i = pl.program_id(1)
    band = ATT_BLOCK + 2 * WINDOW
    rows = 256

    @pl.when(i == 0)
    def _():
        def body(c, carry):
            r0 = pl.multiple_of(c * rows, rows)
            cos = cosk_ref[pl.ds(r0, rows), :]
            sin = sink_tab_ref[pl.ds(r0, rows), :]
            even = lax.broadcasted_iota(jnp.int32, (rows, LANES), 1) % 2 == 0
            for g in range(N_KV_HEADS):
                x = k_ref[pl.ds(r0, rows), g * HEAD_DIM:(g + 1) * HEAD_DIM].astype(F32)
                krot_ref[pl.ds(r0, rows), g * HEAD_DIM:(g + 1) * HEAD_DIM] = _rope_tile(
                    x, cos, sin, even).astype(BF16)
            return carry

        lax.fori_loop(0, seq // rows, body, 0)

    start = pl.multiple_of(jnp.clip(i * ATT_BLOCK - WINDOW, 0, seq - band), ATT_BLOCK)
    kb = krot_ref[pl.ds(start, band), :]
    vb = v_ref[pl.ds(start, band), :]
    ck = ck_ref[0]
    cv = cv_ref[0]
    stacked = ATT_STACK * ATT_BLOCK
    row = lax.broadcasted_iota(jnp.int32, (stacked, band), 0)
    col = lax.broadcasted_iota(jnp.int32, (stacked, band), 1)
    keep = jnp.abs(start + col - (i * ATT_BLOCK + row % ATT_BLOCK)) <= WINDOW
    head_of_row = lax.broadcasted_iota(jnp.int32, (stacked, 1), 0) // ATT_BLOCK
    cos_q = cosq_ref[...] * (ATT_SCALE * LOG2E)
    sin_q = sinq_ref[...] * (ATT_SCALE * LOG2E)
    even_q = lax.broadcasted_iota(jnp.int32, (ATT_BLOCK, LANES), 1) % 2 == 0
    for g in range(N_KV_HEADS):
        sl = slice(g * HEAD_DIM, (g + 1) * HEAD_DIM)
        kcat = jnp.concatenate([kb[:, sl], ck[:, sl]], axis=0)
        vcat = jnp.concatenate([vb[:, sl], cv[:, sl]], axis=0)
        for h0 in range(g * KV_REP, (g + 1) * KV_REP, ATT_STACK):
            q_parts = []
            sink2 = jnp.zeros((stacked, 1), F32)
            for r in range(ATT_STACK):
                hs = slice((h0 + r) * HEAD_DIM, (h0 + r + 1) * HEAD_DIM)
                q_parts.append(_rope_tile(q_ref[:, hs].astype(F32), cos_q, sin_q, even_q).astype(BF16))
                sink2 = jnp.where(head_of_row == r, sink_ref[h0 + r] * LOG2E, sink2)
            q_g = jnp.concatenate(q_parts, axis=0)
            s = lax.dot_general(q_g, kcat, (((1,), (1,)), ((), ())), preferred_element_type=F32)
            s_loc = jnp.where(keep, s[:, :band], -jnp.inf)
            s_ctx = s[:, band:]
            m = jnp.maximum(jnp.maximum(jnp.max(s_loc, axis=-1, keepdims=True),
                                        jnp.max(s_ctx, axis=-1, keepdims=True)), sink2)
            p_loc = jnp.exp2(s_loc - m)
            p_ctx = jnp.exp2(s_ctx - m)
            den = (jnp.sum(p_loc, axis=-1, keepdims=True) + jnp.sum(p_ctx, axis=-1, keepdims=True)
                   + jnp.exp2(sink2 - m))
            p = jnp.concatenate([p_loc.astype(BF16), p_ctx.astype(BF16)], axis=1)
            o = jnp.dot(p, vcat, preferred_element_type=F32) / den
            for r in range(ATT_STACK):
                o_ref[:, (h0 + r) * HEAD_DIM:(h0 + r + 1) * HEAD_DIM] = o[
                    r * ATT_BLOCK:(r + 1) * ATT_BLOCK].astype(o_ref.dtype)


def latent_attention(proj, ctx_k, ctx_v, sink, cos_t, sin_t, batch, seq):
    m = batch * seq
    nb = seq // ATT_BLOCK
    past = ctx_k.shape[1]
    return pl.pallas_call(
        functools.partial(_lat_attn_kernel, seq=seq, past=past),
        grid=(batch, nb),
        in_specs=[pl.BlockSpec(memory_space=pltpu.SMEM),
                  pl.BlockSpec((ATT_BLOCK, Q_W), lambda b, i: (b * nb + i, P_Q // Q_W)),
                  pl.BlockSpec((seq, KV_W), lambda b, i: (b, P_K // KV_W)),
                  pl.BlockSpec((seq, KV_W), lambda b, i: (b, P_V // KV_W)),
                  pl.BlockSpec((1, past, KV_W), lambda b, i: (b, 0, 0)),
                  pl.BlockSpec((1, past, KV_W), lambda b, i: (b, 0, 0)),
                  pl.BlockSpec((ATT_BLOCK, LANES), lambda b, i: (i, 0)),
                  pl.BlockSpec((ATT_BLOCK, LANES), lambda b, i: (i, 0)),
                  pl.BlockSpec((seq, LANES), lambda b, i: (0, 0)),
                  pl.BlockSpec((seq, LANES), lambda b, i: (0, 0))],
        out_specs=pl.BlockSpec((ATT_BLOCK, Q_W), lambda b, i: (b * nb + i, 0)),
        out_shape=jax.ShapeDtypeStruct((m, Q_W), BF16),
        scratch_shapes=[pltpu.VMEM((seq, KV_W), BF16)],
        compiler_params=_cparams(("parallel", "arbitrary")),
        name="latent_attention",
    )(sink, proj, proj, proj, ctx_k, ctx_v, cos_t, sin_t, cos_t, sin_t)


def rope_tables(seq):
    rows = seq // GRID_W
    row = jnp.repeat(jnp.arange(rows, dtype=F32), GRID_W)
    col = jnp.tile(jnp.arange(GRID_W, dtype=F32), rows)
    n_pairs = HEAD_DIM // 4
    inv = ROPE_THETA ** (-jnp.arange(n_pairs, dtype=F32) / n_pairs)
    ang = jnp.concatenate([row[:, None] * inv, col[:, None] * inv], axis=-1)
    ang = jnp.repeat(ang, 2, axis=-1)
    sign = jnp.where(jnp.arange(HEAD_DIM) % 2 == 0, -1.0, 1.0).astype(F32)
    return jnp.cos(ang), jnp.sin(ang) * sign


def _conv_rolled(prev, main, nxt, w_ref, taps, tile):
    half = (taps - 1) // 2
    sub = 8
    acc = w_ref[half:half + 1, :] * main
    for k in range(taps):
        if k != half:
            acc = acc + w_ref[k:k + 1, :] * pltpu.roll(main, (half - k) % tile, axis=0)

    def edge(window):
        out = None
        for k in range(taps):
            term = w_ref[k:k + 1, :] * window[sub - half + k:2 * sub - half + k]
            out = term if out is None else out + term
        return out

    top = edge(jnp.concatenate([prev, main[:2 * sub]], axis=0))
    bottom = edge(jnp.concatenate([main[tile - 2 * sub:], nxt], axis=0))
    return jnp.concatenate([top, acc[sub:tile - sub], bottom], axis=0)


def _ssm_conv_kernel(prev_ref, main_ref, next_ref, w_ref, b_ref, o_ref, *, nt, tile):
    t = pl.program_id(0) % nt
    prev = jnp.where(t == 0, 0.0, prev_ref[...].astype(F32)[HALO - 8:])
    nxt = jnp.where(t == nt - 1, 0.0, next_ref[...].astype(F32)[:8])
    y = _conv_rolled(prev, main_ref[...].astype(F32), nxt, w_ref, SSM_CONV, tile) + b_ref[...]
    o_ref[...] = _silu(y).astype(o_ref.dtype)


def _halo_specs(tile, width, col0, m):
    hb = tile // HALO
    last = m // HALO - 1
    prev = pl.BlockSpec((HALO, width), lambda i, j: (jnp.maximum(i * hb - 1, 0), col0 + j))
    main = pl.BlockSpec((tile, width), lambda i, j: (i, col0 + j))
    nxt = pl.BlockSpec((HALO, width), lambda i, j: (jnp.minimum((i + 1) * hb, last), col0 + j))
    return prev, main, nxt


def ssm_conv(proj, w, b, seq, tile):
    m = proj.shape[0]
    width = 1024
    nt = seq // tile
    prev, main, nxt = _halo_specs(tile, width, P_XS // width, m)
    return pl.pallas_call(
        functools.partial(_ssm_conv_kernel, nt=nt, tile=tile),
        grid=(m // tile, SSM_CONV_CH // width),
        in_specs=[prev, main, nxt,
                  pl.BlockSpec((SSM_CONV, width), lambda i, j: (0, j)),
                  pl.BlockSpec((1, width), lambda i, j: (0, j))],
        out_specs=pl.BlockSpec((tile, width), lambda i, j: (i, j)),
        out_shape=jax.ShapeDtypeStruct((m, SSM_CONV_CH), BF16),
        compiler_params=_cparams(("parallel", "parallel")),
        name="ssm_conv",
    )(proj, proj, proj, w, b.reshape(1, SSM_CONV_CH))


def _short_conv_kernel(b_ref, cp_ref, cm_ref, cn_ref, hp_ref, hm_ref, hn_ref, w_ref, o_ref,
                       *, nt, tile):
    t = pl.program_id(0) % nt

    def prod(c_ref, h_ref):
        return c_ref[...].astype(F32) * h_ref[...].astype(F32)

    prev = jnp.where(t == 0, 0.0, prod(cp_ref, hp_ref)[HALO - 8:])
    nxt = jnp.where(t == nt - 1, 0.0, prod(cn_ref, hn_ref)[:8])
    y = _conv_rolled(prev, prod(cm_ref, hm_ref), nxt, w_ref, SC_CONV, tile)
    o_ref[...] = (b_ref[...].astype(F32) * y).astype(o_ref.dtype)


def short_conv(proj, w, seq, tile):
    m = proj.shape[0]
    width = 1024
    nt = seq // tile
    cp, cm, cn = _halo_specs(tile, width, P_SCC // width, m)
    hp, hm, hn = _halo_specs(tile, width, P_SCH // width, m)
    return pl.pallas_call(
        functools.partial(_short_conv_kernel, nt=nt, tile=tile),
        grid=(m // tile, SC_WIDTH // width),
        in_specs=[pl.BlockSpec((tile, width), lambda i, j: (i, P_SCB // width + j)),
                  cp, cm, cn, hp, hm, hn,
                  pl.BlockSpec((SC_CONV, width), lambda i, j: (0, j))],
        out_specs=pl.BlockSpec((tile, width), lambda i, j: (i, j)),
        out_shape=jax.ShapeDtypeStruct((m, SC_WIDTH), BF16),
        compiler_params=_cparams(("parallel", "parallel")),
        name="short_conv",
    )(proj, proj, proj, proj, proj, proj, proj, w)


def _split3(a):
    a1 = a.astype(BF16)
    r1 = a - a1.astype(F32)
    a2 = r1.astype(BF16)
    a3 = (r1 - a2.astype(F32)).astype(BF16)
    return a1, a2, a3


def _softplus(x):
    return jnp.maximum(x, 0.0) + jnp.log1p(jnp.exp(-jnp.abs(x)))


def _ssd_kernel(*refs, backward, has_h0, emit_state, nprob):
    refs = list(refs)
    x_ref, b_ref, c_ref, dt_ref, dtb_ref, alog_ref, sel_ref = refs[:7]
    del refs[:7]
    h0_ref = refs.pop(0) if has_h0 else None
    if backward:
        yf_ref, z_ref, d_ref, gn_ref = refs[:4]
        del refs[:4]
    y_ref = refs.pop(0)
    hout_ref = refs.pop(0) if emit_state else None
    st_ref = refs.pop(0)
    acc_ref = refs.pop(0) if backward else y_ref
    q = SSM_CHUNK
    gw = (SSM_HEADS // SSM_GROUPS) * SSM_HEAD_DIM

    @pl.when(pl.program_id(1) == 0)
    def _():
        for p in range(nprob):
            for g in range(SSM_GROUPS):
                if has_h0:
                    st_ref[p, g] = h0_ref[p, 0, g * gw:(g + 1) * gw, :].T
                else:
                    st_ref[p, g] = jnp.zeros((SSM_STATE, gw), F32)

    lo = SSM_HEADS if backward else 0
    neg_a = -LOG2E * jnp.exp(alog_ref[...])
    ii = lax.broadcasted_iota(jnp.int32, (q, q), 0)
    jj = lax.broadcasted_iota(jnp.int32, (q, q), 1)
    tri = (jj >= ii) if backward else (jj <= ii)
    tri_b = jnp.where(tri, 1.0, 0.0).astype(BF16)
    lane_lo = lax.broadcasted_iota(jnp.int32, (1, LANES), 1) < SSM_HEAD_DIM
    heads_per_group = SSM_HEADS // SSM_GROUPS
    pairs_per_group = heads_per_group // 2
    edge = 0 if backward else q - 1
    probs = range(nprob)

    pre = []
    for p in probs:
        dt = _softplus(dt_ref[p] + dtb_ref[...])
        a2 = dt * neg_a
        u2 = None
        for piece in _split3(a2):
            d = jnp.dot(tri_b, piece, preferred_element_type=F32)
            u2 = d if u2 is None else u2 + d
        v_t = (u2 - jnp.log2(dt)).T
        t_row = u2[edge:edge + 1, :]
        w_c = dt * jnp.exp2(t_row - u2)
        w_hi = w_c.astype(BF16)
        w_lo = (w_c - w_hi.astype(F32)).astype(BF16)
        w_exp = (jnp.dot(w_hi, sel_ref[...], preferred_element_type=F32)
                 + jnp.dot(w_lo, sel_ref[...], preferred_element_type=F32))
        pre.append((u2, v_t, t_row, w_exp))

    for g in range(SSM_GROUPS):
        grp = []
        for p in probs:
            b_g = b_ref[p, :, g * SSM_STATE:(g + 1) * SSM_STATE]
            c_g = c_ref[p, :, g * SSM_STATE:(g + 1) * SSM_STATE]
            cb = lax.dot_general(c_g, b_g, (((1,), (1,)), ((), ())), preferred_element_type=F32)
            h_t = st_ref[p, g]
            y_off = jnp.dot(c_g, h_t.astype(BF16), preferred_element_type=F32)
            grp.append((b_g, cb, h_t, y_off, [], []))
        for pr in range(pairs_per_group):
            h_a = g * heads_per_group + 2 * pr
            ca = lo + h_a
            cbk = ca + 1
            pc = (h_a // 2) * LANES
            for p in probs:
                u2, v_t, t_row, w_exp = pre[p]
                b_g, cb, h_t, y_off, xs_parts, dec_parts = grp[p]
                x_b = x_ref[p, :, pc:pc + LANES]
                up = jnp.where(lane_lo, u2[:, ca:ca + 1], u2[:, cbk:cbk + 1])
                tp = jnp.where(lane_lo, t_row[:, ca:ca + 1], t_row[:, cbk:cbk + 1])
                ys = []
                for col in (ca, cbk):
                    seg = u2[:, col:col + 1] - v_t[col:col + 1, :]
                    decay_dt = jnp.exp2(jnp.where(tri, seg, -jnp.inf))
                    ys.append(jnp.dot((cb * decay_dt).astype(BF16), x_b, preferred_element_type=F32))
                acc_ref[p, :, pc:pc + LANES] = (jnp.where(lane_lo, ys[0], ys[1])
                                                + jnp.exp2(up) * y_off[:, pr * LANES:(pr + 1) * LANES])
                xs_parts.append((x_b.astype(F32) * w_exp[:, pc:pc + LANES]).astype(BF16))
                dec_parts.append(jnp.exp2(tp))
        for p in probs:
            b_g, cb, h_t, y_off, xs_parts, dec_parts = grp[p]
            xs_g = jnp.concatenate(xs_parts, axis=1)
            dec_g = jnp.concatenate(dec_parts, axis=1)
            upd = lax.dot_general(b_g, xs_g, (((0,), (0,)), ((), ())), preferred_element_type=F32)
            st_ref[p, g] = dec_g * h_t + upd

    if backward:
        for p in probs:
            y = acc_ref[p] + yf_ref[p] + x_ref[p].astype(F32) * d_ref[...]
            y = y * _silu(z_ref[p].astype(F32))
            y_ref[p] = _rms(y, gn_ref[...]).astype(y_ref.dtype)

    if emit_state:
        @pl.when(pl.program_id(1) == pl.num_programs(1) - 1)
        def _():
            for p in range(nprob):
                for g in range(SSM_GROUPS):
                    hout_ref[p, g * gw:(g + 1) * gw, :] = st_ref[p, g].T


def _head_selector(backward):
    lane = jnp.arange(LANES)[:, None]
    head = jnp.arange(SSM_INNER)[None, :] // SSM_HEAD_DIM
    return (lane == head + (SSM_HEADS if backward else 0)).astype(BF16)


def ssd_scan(xbc, dt_raw, dt_bias, a_log, h0, batch, seq, emit_state, finish=None):
    backward = finish is not None
    m = batch * seq
    nc = seq // SSM_CHUNK
    gw = SSM_INNER // SSM_GROUPS
    hp = SSM_HEADS * SSM_HEAD_DIM

    nprob = SSD_PROBLEMS
    assert batch % nprob == 0

    def chunk(c):
        return nc - 1 - c if backward else c

    def seq_spec(width, col):
        return pl.BlockSpec((nprob, SSM_CHUNK, width), lambda b, c: (b, chunk(c), col))

    def by_seq(a):
        return a.reshape(batch, seq, a.shape[-1])

    vec = pl.BlockSpec((1, SSM_INNER), lambda b, c: (0, 0))
    lane_vec = pl.BlockSpec((1, LANES), lambda b, c: (0, 0))
    in_specs = [seq_spec(SSM_INNER, 0),
                seq_spec(SSM_BC, SSM_INNER // SSM_BC),
                seq_spec(SSM_BC, SSM_INNER // SSM_BC + 1),
                seq_spec(LANES, 0),
                lane_vec, lane_vec,
                pl.BlockSpec((LANES, SSM_INNER), lambda b, c: (0, 0))]
    xbc3 = by_seq(xbc)
    args = [xbc3, xbc3, xbc3, by_seq(dt_raw), dt_bias, a_log, _head_selector(backward)]
    if h0 is not None:
        state, slab = h0
        in_specs.append(pl.BlockSpec((nprob, 1, hp, SSM_STATE), lambda b, c: (b, slab, 0, 0)))
        args.append(state)
    scratch = [pltpu.VMEM((nprob, SSM_GROUPS, SSM_STATE, gw), F32)]
    if backward:
        y_fwd, proj, d_exp, norm_g = finish
        in_specs += [seq_spec(SSM_INNER, 0), seq_spec(SSM_INNER, P_Z // SSM_INNER), vec, vec]
        args += [by_seq(y_fwd), by_seq(proj), d_exp, norm_g.reshape(1, SSM_INNER)]
        scratch.append(pltpu.VMEM((nprob, SSM_CHUNK, SSM_INNER), F32))
    out_specs = [seq_spec(SSM_INNER, 0)]
    out_shape = [jax.ShapeDtypeStruct((batch, seq, SSM_INNER), BF16 if backward else F32)]
    if emit_state:
        out_specs.append(pl.BlockSpec((nprob, hp, SSM_STATE), lambda b, c: (b, 0, 0)))
        out_shape.append(jax.ShapeDtypeStruct((batch, hp, SSM_STATE), F32))
    out = pl.pallas_call(
        functools.partial(_ssd_kernel, backward=backward, has_h0=h0 is not None,
                          emit_state=emit_state, nprob=nprob),
        grid=(batch // nprob, nc),
        in_specs=in_specs,
        out_specs=out_specs,
        out_shape=out_shape,
        scratch_shapes=scratch,
        compiler_params=_cparams(("parallel", "arbitrary")),
        name="ssd_scan_bwd" if backward else "ssd_scan_fwd",
    )(*args)
    y = out[0].reshape(m, SSM_INNER)
    return (y, out[1]) if emit_state else (y, None)


def _pad_lanes(v):
    flat = v.reshape(1, 2 * SSM_HEADS).astype(F32)
    return jnp.pad(flat, ((0, 0), (0, LANES - 2 * SSM_HEADS)))


def _prepare_weights(w_in, w_att_out, w_ssd_out, w_sc_out, w_o, w_gate_up, w_down):
    def cols(o, n):
        return w_in[:, :, o:o + n]

    def reordered(l):
        w = w_in[l]
        parts = [(_O_Q, Q_W), (_O_Z, SSM_INNER), (_O_SCB, SC_WIDTH), (_O_SCC, SC_WIDTH),
                 (_O_SCH, SC_WIDTH), (_O_G, N_BRANCH * D_MODEL), (_O_XS, SSM_INNER),
                 (_O_BM, SSM_BC), (_O_CM, SSM_BC), (_O_K, KV_W), (_O_V, KV_W)]
        return jnp.concatenate([w[:, o:o + n] for o, n in parts], axis=1).astype(BF16)[None]

    w_main = [reordered(l) for l in range(DEPTH)]
    w_dt = jnp.pad(cols(_O_DT, 2 * SSM_HEADS), ((0, 0), (0, 0), (0, LANES - 2 * SSM_HEADS))).astype(BF16)
    return dict(w_main=w_main, w_dt=w_dt,
                w_att=w_att_out.astype(BF16), w_ssd=w_ssd_out.astype(BF16),
                w_sc=w_sc_out.astype(BF16), w_o=w_o.astype(BF16),
                w_gate_up=w_gate_up.astype(BF16), w_down=w_down.astype(BF16))


def _tile_plan(seq):
    return dict(
        in_proj=(1024, 1024),
        swiglu=(1024, 512),
        down=(512, D_FF // 4),
        w_o=(512, D_MODEL),
        merge=(1024, 256),
        conv=min(seq, 512))


def _trunk_layer(x, mod, mod_row, l, lw, p, batch, seq, latent):
    plan = _tile_plan(seq)
    outs = in_proj(x, p['g_pre_mix'], mod, mod_row, lw['w_main'][l], lw['w_dt'], l, *plan['in_proj'],
                   emit_kv=latent is None)
    proj, dt_raw = outs[0], outs[1]
    if latent is None:
        y_att = context_attention(proj, p['sink'], batch, seq)
        h0_f = h0_b = None
    else:
        ctx_k, ctx_v, state, rope = latent
        y_att = latent_attention(proj, ctx_k.reshape(batch, -1, KV_W).astype(BF16),
                                 ctx_v.reshape(batch, -1, KV_W).astype(BF16), p['sink'],
                                 rope[0], rope[1], batch, seq)
        h0_f, h0_b = (state, 2 * l), (state, 2 * l + 1)
    xbc = ssm_conv(proj, p['ssm_conv_w'], p['ssm_conv_b'], seq, plan['conv'])
    dt_bias = _pad_lanes(p['ssm_dt_bias'])
    a_log = _pad_lanes(p['ssm_a_log'])
    d_exp = jnp.repeat(p['ssm_d'].astype(F32), SSM_HEAD_DIM).reshape(1, SSM_INNER)
    emit_state = latent is None
    y_f, h_f = ssd_scan(xbc, dt_raw, dt_bias, a_log, h0_f, batch, seq, emit_state)
    y_ssd, h_b = ssd_scan(xbc, dt_raw, dt_bias, a_log, h0_b, batch, seq, emit_state,
                          finish=(y_f, proj, d_exp, p['ssm_norm_g']))
    y_sc = short_conv(proj, p['sc_conv_w'], seq, plan['conv'])
    merged = merge_branches(y_att, y_ssd, y_sc, lw['w_att'], lw['w_ssd'], lw['w_sc'], l, proj,
                            *plan['merge'])
    x = matmul_norm_residual(merged, lw['w_o'], l, x, p['g_post_mix'], mod, mod_row, 2, *plan['w_o'])
    act = norm_swiglu(x, p['g_pre_ffn'], mod, mod_row, lw['w_gate_up'], l, *plan['swiglu'])
    x = matmul_norm_residual(act, lw['w_down'], l, x, p['g_post_ffn'], mod, mod_row, 5, *plan['down'])
    new_ctx = None
    if latent is None:
        kv = outs[2]
        k = kv[:, :KV_W].reshape(batch, seq, N_KV_HEADS, HEAD_DIM)
        v = kv[:, KV_W:].reshape(batch, seq, N_KV_HEADS, HEAD_DIM)
        new_ctx = (k, v, h_f, h_b)
    return x, new_ctx


def kernel(x_prompt, x_sample, cache_k, cache_v, state_ssm, c, c_ctx, w_mod, b_mod, g_pre_mix, w_in, sink, ssm_conv_w, ssm_conv_b, ssm_dt_bias, ssm_a_log, ssm_d, ssm_norm_g, sc_conv_w, w_att_out, w_ssd_out, w_sc_out, w_o, g_post_mix, g_pre_ffn, w_gate_up, w_down, g_post_ffn):
    batch, seq, _ = x_prompt.shape
    dec_batch, dec_seq, _ = x_sample.shape
    cond = jnp.concatenate([c_ctx[None, :], c, jnp.zeros((8 - 1 - dec_batch, D_MODEL), F32)], axis=0)
    mod = modulation(cond, w_mod, b_mod).reshape(DEPTH, 8, N_MOD, D_MODEL)
    rope = rope_tables(dec_seq)
    y_prompt = x_prompt.reshape(batch * seq, D_MODEL)
    y_sample = x_sample.reshape(dec_batch * dec_seq, D_MODEL)
    lw = _prepare_weights(w_in, w_att_out, w_ssd_out, w_sc_out, w_o, w_gate_up, w_down)
    state_in = state_ssm.reshape(dec_batch, 2 * DEPTH, SSM_HEADS * SSM_HEAD_DIM, SSM_STATE)
    ks_out, vs_out, ss_out = [], [], []
    for l in range(DEPTH):
        p = {'g_pre_mix': g_pre_mix[l], 'sink': sink[l], 'ssm_conv_w': ssm_conv_w[l],
             'ssm_conv_b': ssm_conv_b[l], 'ssm_dt_bias': ssm_dt_bias[l], 'ssm_a_log': ssm_a_log[l],
             'ssm_d': ssm_d[l], 'ssm_norm_g': ssm_norm_g[l], 'sc_conv_w': sc_conv_w[l],
             'g_post_mix': g_post_mix[l], 'g_pre_ffn': g_pre_ffn[l], 'g_post_ffn': g_post_ffn[l]}
        y_prompt, (k_l, v_l, h_f, h_b) = _trunk_layer(
            y_prompt, mod[l], lambda r: 0, l, lw, p, batch, seq, None)
        ks_out.append(k_l)
        vs_out.append(v_l)
        ss_out += [h_f, h_b]
        y_sample, _ = _trunk_layer(
            y_sample, mod[l], lambda r: 1 + r // dec_seq, l, lw, p, dec_batch, dec_seq,
            (cache_k[:, l], cache_v[:, l], state_in, rope))
    new_state = jnp.stack(ss_out, axis=1).reshape(
        batch, DEPTH, 2, SSM_HEADS, SSM_HEAD_DIM, SSM_STATE)
    return (y_prompt.reshape(batch, seq, D_MODEL), y_sample.reshape(dec_batch, dec_seq, D_MODEL),
            jnp.stack(ks_out, axis=1), jnp.stack(vs_out, axis=1), new_state)
```

```python
import functools

import jax
import jax.numpy as jnp
from jax import lax
from jax.experimental import pallas as pl
from jax.experimental.pallas import tpu as pltpu

F32 = jnp.float32
BF16 = jnp.bfloat16

D_MODEL = 2048
DEPTH = 2
GRID_W = 64
N_HEADS = 16
N_KV_HEADS = 4
KV_REP = N_HEADS // N_KV_HEADS
HEAD_DIM = 128
Q_W = N_HEADS * HEAD_DIM
KV_W = N_KV_HEADS * HEAD_DIM
WINDOW = 128
ATT_BLOCK = 128
ATT_SCALE = HEAD_DIM ** -0.5
ROPE_THETA = 10000.0
SSM_HEADS = 32
SSM_HEAD_DIM = 64
SSM_INNER = SSM_HEADS * SSM_HEAD_DIM
SSM_GROUPS = 4
SSM_STATE = 128
SSM_BC = SSM_GROUPS * SSM_STATE
SSM_CONV_CH = SSM_INNER + 2 * SSM_BC
SSM_CONV = 5
SSM_CHUNK = 128
SC_WIDTH = D_MODEL
SC_CONV = 3
N_BRANCH = 3
D_FF = ((8 * D_MODEL + 3 * 256 - 1) // (3 * 256)) * 256
N_MOD = 6
EPS = 1e-6
LOG2E = 1.4426950408889634

_O_Q = 0
_O_K = _O_Q + Q_W
_O_V = _O_K + KV_W
_O_Z = _O_V + KV_W
_O_XS = _O_Z + SSM_INNER
_O_BM = _O_XS + SSM_INNER
_O_CM = _O_BM + SSM_BC
_O_DT = _O_CM + SSM_BC
_O_SCB = _O_DT + 2 * SSM_HEADS
_O_SCC = _O_SCB + SC_WIDTH
_O_SCH = _O_SCC + SC_WIDTH
_O_G = _O_SCH + SC_WIDTH
D_IN = _O_G + N_BRANCH * D_MODEL

P_Q = 0
P_Z = 2048
P_SCB = 4096
P_SCC = 6144
P_SCH = 8192
P_G = 10240
P_XS = 16384
P_BM = 18432
P_CM = 18944
P_K = 19456
P_V = 19968
P_N = 20480

SSD_PROBLEMS = 2
ATT_STACK = KV_REP
LANES = 128
HALO = 16
VMEM_LIMIT = 56 * 1024 * 1024


def _cparams(sem):
    return pltpu.CompilerParams(dimension_semantics=sem, vmem_limit_bytes=VMEM_LIMIT)


def _rms(x, g):
    return x * lax.rsqrt(jnp.mean(x * x, axis=-1, keepdims=True) + EPS) * g


def _silu(x):
    return x * jax.nn.sigmoid(x)


def _mod_kernel(c_ref, w_ref, b_ref, o_ref):
    a = _silu(c_ref[...]).astype(BF16)
    o_ref[0] = jnp.dot(a, w_ref[0].astype(BF16), preferred_element_type=F32) + b_ref[0]


def modulation(cond, w_mod, b_mod):
    rows = cond.shape[0]
    n = N_MOD * D_MODEL
    tn = 1024
    return pl.pallas_call(
        _mod_kernel,
        grid=(DEPTH, n // tn),
        in_specs=[pl.BlockSpec((rows, D_MODEL), lambda l, j: (0, 0)),
                  pl.BlockSpec((1, D_MODEL, tn), lambda l, j: (l, 0, j)),
                  pl.BlockSpec((1, 1, tn), lambda l, j: (l, 0, j))],
        out_specs=pl.BlockSpec((1, rows, tn), lambda l, j: (l, 0, j)),
        out_shape=jax.ShapeDtypeStruct((DEPTH, rows, n), F32),
        compiler_params=_cparams(("parallel", "parallel")),
        name="modulation",
    )(cond, w_mod, b_mod.reshape(DEPTH, 1, n))


def _norm_mod(x_ref, g_ref, mod_ref, shift_i, scale_i):
    y = _rms(x_ref[...], g_ref[...])
    return y * (1.0 + mod_ref[0, scale_i:scale_i + 1, :]) + mod_ref[0, shift_i:shift_i + 1, :]


def _by_parity(i, fn, buf_a, buf_b):
    @pl.when(i % 2 == 0)
    def _():
        fn(buf_a, buf_b)

    @pl.when(i % 2 == 1)
    def _():
        fn(buf_b, buf_a)


def _in_proj_kernel(x_ref, g_ref, mod_ref, w_ref, wdt_ref, o_ref, odt_ref, *rest, emit_kv):
    okv_ref, h_ref = rest if emit_kv else (None, rest[0])

    @pl.when(pl.program_id(1) == 0)
    def _():
        h = _norm_mod(x_ref, g_ref, mod_ref, 0, 1).astype(BF16)
        h_ref[...] = h
        odt_ref[...] = jnp.dot(h, wdt_ref[...], preferred_element_type=F32)

    res = jnp.dot(h_ref[...], w_ref[...], preferred_element_type=F32)
    o_ref[...] = res.astype(o_ref.dtype)
    if emit_kv:
        @pl.when(pl.program_id(1) == pl.num_programs(1) - 1)
        def _():
            okv_ref[...] = res


def in_proj(x, g, mod, mod_row, w, w_dt, l, tm, tn, emit_kv):
    m = x.shape[0]
    n = w.shape[2]
    assert w.shape[0] == 1 and P_N - P_K == tn and 2 * KV_W == tn
    out_specs = [pl.BlockSpec((tm, tn), lambda i, j: (i, j)),
                 pl.BlockSpec((tm, LANES), lambda i, j: (i, 0))]
    out_shape = [jax.ShapeDtypeStruct((m, n), BF16), jax.ShapeDtypeStruct((m, LANES), F32)]
    if emit_kv:
        out_specs.append(pl.BlockSpec((tm, tn), lambda i, j: (i, 0)))
        out_shape.append(jax.ShapeDtypeStruct((m, tn), F32))
    return pl.pallas_call(
        functools.partial(_in_proj_kernel, emit_kv=emit_kv),
        grid=(m // tm, n // tn),
        in_specs=[pl.BlockSpec((tm, D_MODEL), lambda i, j: (i, 0)),
                  pl.BlockSpec((1, D_MODEL), lambda i, j: (0, 0)),
                  pl.BlockSpec((1, N_MOD, D_MODEL), lambda i, j: (mod_row(i * tm), 0, 0)),
                  pl.BlockSpec((None, D_MODEL, tn), lambda i, j: (0, 0, j)),
                  pl.BlockSpec((None, D_MODEL, LANES), lambda i, j: (l, 0, 0))],
        out_specs=out_specs,
        out_shape=out_shape,
        scratch_shapes=[pltpu.VMEM((tm, D_MODEL), BF16)],
        compiler_params=_cparams(("parallel", "arbitrary")),
        name="in_proj",
    )(x, g.reshape(1, D_MODEL), mod, w, w_dt)


def _nm_swiglu_kernel(x_ref, g_ref, mod_ref, wg_ref, wu_ref, o_ref, h_ref):
    @pl.when(pl.program_id(1) == 0)
    def _():
        h_ref[...] = _norm_mod(x_ref, g_ref, mod_ref, 3, 4).astype(BF16)

    h = h_ref[...]
    a = jnp.dot(h, wg_ref[...], preferred_element_type=F32)
    b = jnp.dot(h, wu_ref[...], preferred_element_type=F32)
    o_ref[...] = (_silu(a) * b).astype(o_ref.dtype)


def norm_swiglu(x, g, mod, mod_row, w_gate_up, l, tm, tn):
    m = x.shape[0]
    nb = D_FF // tn
    return pl.pallas_call(
        _nm_swiglu_kernel,
        grid=(m // tm, nb),
        in_specs=[pl.BlockSpec((tm, D_MODEL), lambda i, j: (i, 0)),
                  pl.BlockSpec((1, D_MODEL), lambda i, j: (0, 0)),
                  pl.BlockSpec((1, N_MOD, D_MODEL), lambda i, j: (mod_row(i * tm), 0, 0)),
                  pl.BlockSpec((None, D_MODEL, tn), lambda i, j: (l, 0, j)),
                  pl.BlockSpec((None, D_MODEL, tn), lambda i, j: (l, 0, j + nb))],
        out_specs=pl.BlockSpec((tm, tn), lambda i, j: (i, j)),
        out_shape=jax.ShapeDtypeStruct((m, D_FF), BF16),
        scratch_shapes=[pltpu.VMEM((tm, D_MODEL), BF16)],
        compiler_params=_cparams(("parallel", "arbitrary")),
        name="norm_swiglu",
    )(x, g.reshape(1, D_MODEL), mod, w_gate_up, w_gate_up)


def _matmul_norm_res_kernel(a_ref, w_ref, x_ref, g_ref, mod_ref, o_ref, acca_ref, accb_ref,
                            *, gate_i, nt, ksteps, tm):
    i = pl.program_id(0)
    k = pl.program_id(1)
    rows = tm // ksteps
    r0 = pl.multiple_of(k * rows, rows)

    @pl.when((i == 0) & (k == 0))
    def _():
        acca_ref[...] = jnp.zeros_like(acca_ref)
        accb_ref[...] = jnp.zeros_like(accb_ref)

    def row(acc_cur, acc_prev):
        def epilogue():
            y = _rms(acc_prev[pl.ds(r0, rows), :], g_ref[...])
            o_ref[pl.ds(r0, rows), :] = (x_ref[pl.ds(r0, rows), :]
                                         + mod_ref[0, gate_i:gate_i + 1, :] * y)

        @pl.when(i < nt)
        def _():
            prod = jnp.dot(a_ref[...], w_ref[...], preferred_element_type=F32)
            acc_cur[...] = jnp.where(k == 0, prod, acc_cur[...] + prod)
            epilogue()

        @pl.when(i == nt)
        def _():
            epilogue()

    _by_parity(i, row, acca_ref, accb_ref)


def _matmul_norm_res_single_kernel(a_ref, w_ref, x_ref, g_ref, mod_ref, o_ref, *, gate_i):
    y = _rms(jnp.dot(a_ref[...], w_ref[...], preferred_element_type=F32), g_ref[...])
    o_ref[...] = x_ref[...] + mod_ref[0, gate_i:gate_i + 1, :] * y


def matmul_norm_residual(a, w, l, x, g, mod, mod_row, gate_i, tm, tk):
    m, kdim = a.shape
    nt = m // tm
    ksteps = kdim // tk
    if ksteps == 1:
        w_mode = pl.Buffered(1) if kdim > D_MODEL else None
        return pl.pallas_call(
            functools.partial(_matmul_norm_res_single_kernel, gate_i=gate_i),
            grid=(nt,),
            in_specs=[pl.BlockSpec((tm, kdim), lambda i: (i, 0)),
                      pl.BlockSpec((None, kdim, D_MODEL), lambda i: (l, 0, 0), pipeline_mode=w_mode),
                      pl.BlockSpec((tm, D_MODEL), lambda i: (i, 0)),
                      pl.BlockSpec((1, D_MODEL), lambda i: (0, 0)),
                      pl.BlockSpec((1, N_MOD, D_MODEL), lambda i: (mod_row(i * tm), 0, 0))],
            out_specs=pl.BlockSpec((tm, D_MODEL), lambda i: (i, 0)),
            out_shape=jax.ShapeDtypeStruct((m, D_MODEL), F32),
            compiler_params=_cparams(("parallel",)),
            name="matmul_norm_residual_1k",
        )(a, w, x, g.reshape(1, D_MODEL), mod)

    def kk(i, k):
        return jnp.where(i == nt, ksteps - 1, k)

    def prev(i):
        return jnp.maximum(i - 1, 0)

    return pl.pallas_call(
        functools.partial(_matmul_norm_res_kernel, gate_i=gate_i, nt=nt, ksteps=ksteps, tm=tm),
        grid=(nt + 1, ksteps),
        in_specs=[pl.BlockSpec((tm, tk), lambda i, k: (jnp.minimum(i, nt - 1), kk(i, k))),
                  pl.BlockSpec((None, tk, D_MODEL), lambda i, k: (l, kk(i, k), 0)),
                  pl.BlockSpec((tm, D_MODEL), lambda i, k: (prev(i), 0)),
                  pl.BlockSpec((1, D_MODEL), lambda i, k: (0, 0)),
                  pl.BlockSpec((1, N_MOD, D_MODEL), lambda i, k: (mod_row(prev(i) * tm), 0, 0))],
        out_specs=pl.BlockSpec((tm, D_MODEL), lambda i, k: (prev(i), 0)),
        out_shape=jax.ShapeDtypeStruct((m, D_MODEL), F32),
        scratch_shapes=[pltpu.VMEM((tm, D_MODEL), F32), pltpu.VMEM((tm, D_MODEL), F32)],
        compiler_params=_cparams(("arbitrary", "arbitrary")),
        name="matmul_norm_residual",
    )(a, w, x, g.reshape(1, D_MODEL), mod)


def _merge_kernel(ya_ref, ys_ref, yc_ref, wa_ref, ws_ref, wc_ref, ga_ref, gs_ref, gc_ref, o_ref):
    def branch(y_ref, w_ref, gate_ref):
        return jax.nn.sigmoid(gate_ref[...].astype(F32)) * jnp.dot(
            y_ref[...], w_ref[...], preferred_element_type=F32)

    o_ref[...] = (branch(ya_ref, wa_ref, ga_ref) + branch(ys_ref, ws_ref, gs_ref)
                  + branch(yc_ref, wc_ref, gc_ref)).astype(o_ref.dtype)


def merge_branches(y_att, y_ssd, y_sc, w_att, w_ssd, w_sc, l, proj, tm, tn):
    m = y_att.shape[0]
    gb = P_G // tn
    nb = D_MODEL // tn
    y_spec = pl.BlockSpec((tm, D_MODEL), lambda i, j: (i, 0))
    w_spec = pl.BlockSpec((None, D_MODEL, tn), lambda i, j: (l, 0, j))

    def g_spec(b):
        return pl.BlockSpec((tm, tn), lambda i, j: (i, gb + b * nb + j))

    return pl.pallas_call(
        _merge_kernel,
        grid=(m // tm, nb),
        in_specs=[y_spec, y_spec, y_spec, w_spec, w_spec, w_spec, g_spec(0), g_spec(1), g_spec(2)],
        out_specs=pl.BlockSpec((tm, tn), lambda i, j: (i, j)),
        out_shape=jax.ShapeDtypeStruct((m, D_MODEL), BF16),
        compiler_params=_cparams(("parallel", "parallel")),
        name="merge_branches",
    )(y_att, y_ssd, y_sc, w_att, w_ssd, w_sc, proj, proj, proj)


def _ctx_attn_kernel(sink_ref, q_ref, k_ref, v_ref, o_ref):
    seq = q_ref.shape[0]
    head_of_row = lax.broadcasted_iota(jnp.int32, (KV_REP * seq, 1), 0) // seq
    for g in range(N_KV_HEADS):
        k_g = k_ref[:, g * HEAD_DIM:(g + 1) * HEAD_DIM]
        v_g = v_ref[:, g * HEAD_DIM:(g + 1) * HEAD_DIM]
        q_g = jnp.concatenate([q_ref[:, (g * KV_REP + r) * HEAD_DIM:(g * KV_REP + r + 1) * HEAD_DIM]
                               for r in range(KV_REP)], axis=0)
        sink2 = jnp.zeros((KV_REP * seq, 1), F32)
        for r in range(KV_REP):
            sink2 = jnp.where(head_of_row == r, sink_ref[g * KV_REP + r] * LOG2E, sink2)
        s = lax.dot_general(q_g, k_g, (((1,), (1,)), ((), ())),
                            preferred_element_type=F32) * (ATT_SCALE * LOG2E)
        m = jnp.maximum(jnp.max(s, axis=-1, keepdims=True), sink2)
        p = jnp.exp2(s - m)
        den = jnp.sum(p, axis=-1, keepdims=True) + jnp.exp2(sink2 - m)
        o = jnp.dot(p.astype(BF16), v_g, preferred_element_type=F32) / den
        for r in range(KV_REP):
            h = g * KV_REP + r
            o_ref[:, h * HEAD_DIM:(h + 1) * HEAD_DIM] = o[r * seq:(r + 1) * seq].astype(o_ref.dtype)


def context_attention(proj, sink, batch, seq):
    m = batch * seq
    return pl.pallas_call(
        _ctx_attn_kernel,
        grid=(batch,),
        in_specs=[pl.BlockSpec(memory_space=pltpu.SMEM),
                  pl.BlockSpec((seq, Q_W), lambda b: (b, P_Q // Q_W)),
                  pl.BlockSpec((seq, KV_W), lambda b: (b, P_K // KV_W)),
                  pl.BlockSpec((seq, KV_W), lambda b: (b, P_V // KV_W))],
        out_specs=pl.BlockSpec((seq, Q_W), lambda b: (b, 0)),
        out_shape=jax.ShapeDtypeStruct((m, Q_W), BF16),
        compiler_params=_cparams(("parallel",)),
        name="context_attention",
    )(sink, proj, proj, proj)


def _rope_tile(x, cos, sin_signed, even_lane):
    nxt = pltpu.roll(x, LANES - 1, axis=1)
    prv = pltpu.roll(x, 1, axis=1)
    return x * cos + jnp.where(even_lane, nxt, prv) * sin_signed


def _lat_attn_kernel(sink_ref, q_ref, k_ref, v_ref, ck_ref, cv_ref, cosq_ref, sinq_ref,
                     cosk_ref, sink_tab_ref, o_ref, krot_ref, *, seq, past):
    i = pl.program_id(1)
    band = ATT_BLOCK + 2 * WINDOW
    rows = 256

    @pl.when(i == 0)
    def _():
        def body(c, carry):
            r0 = pl.multiple_of(c * rows, rows)
            cos = cosk_ref[pl.ds(r0, rows), :]
            sin = sink_tab_ref[pl.ds(r0, rows), :]
            even = lax.broadcasted_iota(jnp.int32, (rows, LANES), 1) % 2 == 0
            for g in range(N_KV_HEADS):
                x = k_ref[pl.ds(r0, rows), g * HEAD_DIM:(g + 1) * HEAD_DIM].astype(F32)
                krot_ref[pl.ds(r0, rows), g * HEAD_DIM:(g + 1) * HEAD_DIM] = _rope_tile(
                    x, cos, sin, even).astype(BF16)
            return carry

        lax.fori_loop(0, seq // rows, body, 0)

    start = pl.multiple_of(jnp.clip(i * ATT_BLOCK - WINDOW, 0, seq - band), ATT_BLOCK)
    kb = krot_ref[pl.ds(start, band), :]
    vb = v_ref[pl.ds(start, band), :]
    ck = ck_ref[0]
    cv = cv_ref[0]
    stacked = ATT_STACK * ATT_BLOCK
    row = lax.broadcasted_iota(jnp.int32, (stacked, band), 0)
    col = lax.broadcasted_iota(jnp.int32, (stacked, band), 1)
    keep = jnp.abs(start + col - (i * ATT_BLOCK + row % ATT_BLOCK)) <= WINDOW
    head_of_row = lax.broadcasted_iota(jnp.int32, (stacked, 1), 0) // ATT_BLOCK
    cos_q = cosq_ref[...] * (ATT_SCALE * LOG2E)
    sin_q = sinq_ref[...] * (ATT_SCALE * LOG2E)
    even_q = lax.broadcasted_iota(jnp.int32, (ATT_BLOCK, LANES), 1) % 2 == 0
    for g in range(N_KV_HEADS):
        sl = slice(g * HEAD_DIM, (g + 1) * HEAD_DIM)
        kcat = jnp.concatenate([kb[:, sl], ck[:, sl]], axis=0)
        vcat = jnp.concatenate([vb[:, sl], cv[:, sl]], axis=0)
        for h0 in range(g * KV_REP, (g + 1) * KV_REP, ATT_STACK):
            q_parts = []
            sink2 = jnp.zeros((stacked, 1), F32)
            for r in range(ATT_STACK):
                hs = slice((h0 + r) * HEAD_DIM, (h0 + r + 1) * HEAD_DIM)
                q_parts.append(_rope_tile(q_ref[:, hs].astype(F32), cos_q, sin_q, even_q).astype(BF16))
                sink2 = jnp.where(head_of_row == r, sink_ref[h0 + r] * LOG2E, sink2)
            q_g = jnp.concatenate(q_parts, axis=0)
            s = lax.dot_general(q_g, kcat, (((1,), (1,)), ((), ())), preferred_element_type=F32)
            s_loc = jnp.where(keep, s[:, :band], -jnp.inf)
            s_ctx = s[:, band:]
            m = jnp.maximum(jnp.maximum(jnp.max(s_loc, axis=-1, keepdims=True),
                                        jnp.max(s_ctx, axis=-1, keepdims=True)), sink2)
            p_loc = jnp.exp2(s_loc - m)
            p_ctx = jnp.exp2(s_ctx - m)
            den = (jnp.sum(p_loc, axis=-1, keepdims=True) + jnp.sum(p_ctx, axis=-1, keepdims=True)
                   + jnp.exp2(sink2 - m))
            p = jnp.concatenate([p_loc.astype(BF16), p_ctx.astype(BF16)], axis=1)
            o = jnp.dot(p, vcat, preferred_element_type=F32) / den
            for r in range(ATT_STACK):
                o_ref[:, (h0 + r) * HEAD_DIM:(h0 + r + 1) * HEAD_DIM] = o[
                    r * ATT_BLOCK:(r + 1) * ATT_BLOCK].astype(o_ref.dtype)


def latent_attention(proj, ctx_k, ctx_v, sink, cos_t, sin_t, batch, seq):
    m = batch * seq
    nb = seq // ATT_BLOCK
    past = ctx_k.shape[1]
    return pl.pallas_call(
        functools.partial(_lat_attn_kernel, seq=seq, past=past),
        grid=(batch, nb),
        in_specs=[pl.BlockSpec(memory_space=pltpu.SMEM),
                  pl.BlockSpec((ATT_BLOCK, Q_W), lambda b, i: (b * nb + i, P_Q // Q_W)),
                  pl.BlockSpec((seq, KV_W), lambda b, i: (b, P_K // KV_W)),
                  pl.BlockSpec((seq, KV_W), lambda b, i: (b, P_V // KV_W)),
                  pl.BlockSpec((1, past, KV_W), lambda b, i: (b, 0, 0)),
                  pl.BlockSpec((1, past, KV_W), lambda b, i: (b, 0, 0)),
                  pl.BlockSpec((ATT_BLOCK, LANES), lambda b, i: (i, 0)),
                  pl.BlockSpec((ATT_BLOCK, LANES), lambda b, i: (i, 0)),
                  pl.BlockSpec((seq, LANES), lambda b, i: (0, 0)),
                  pl.BlockSpec((seq, LANES), lambda b, i: (0, 0))],
        out_specs=pl.BlockSpec((ATT_BLOCK, Q_W), lambda b, i: (b * nb + i, 0)),
        out_shape=jax.ShapeDtypeStruct((m, Q_W), BF16),
        scratch_shapes=[pltpu.VMEM((seq, KV_W), BF16)],
        compiler_params=_cparams(("parallel", "arbitrary")),
        name="latent_attention",
    )(sink, proj, proj, proj, ctx_k, ctx_v, cos_t, sin_t, cos_t, sin_t)


def rope_tables(seq):
    rows = seq // GRID_W
    row = jnp.repeat(jnp.arange(rows, dtype=F32), GRID_W)
    col = jnp.tile(jnp.arange(GRID_W, dtype=F32), rows)
    n_pairs = HEAD_DIM // 4
    inv = ROPE_THETA ** (-jnp.arange(n_pairs, dtype=F32) / n_pairs)
    ang = jnp.concatenate([row[:, None] * inv, col[:, None] * inv], axis=-1)
    ang = jnp.repeat(ang, 2, axis=-1)
    sign = jnp.where(jnp.arange(HEAD_DIM) % 2 == 0, -1.0, 1.0).astype(F32)
    return jnp.cos(ang), jnp.sin(ang) * sign


def _conv_rolled(prev, main, nxt, w_ref, taps, tile):
    half = (taps - 1) // 2
    sub = 8
    acc = w_ref[half:half + 1, :] * main
    for k in range(taps):
        if k != half:
            acc = acc + w_ref[k:k + 1, :] * pltpu.roll(main, (half - k) % tile, axis=0)

    def edge(window):
        out = None
        for k in range(taps):
            term = w_ref[k:k + 1, :] * window[sub - half + k:2 * sub - half + k]
            out = term if out is None else out + term
        return out

    top = edge(jnp.concatenate([prev, main[:2 * sub]], axis=0))
    bottom = edge(jnp.concatenate([main[tile - 2 * sub:], nxt], axis=0))
    return jnp.concatenate([top, acc[sub:tile - sub], bottom], axis=0)


def _ssm_conv_kernel(prev_ref, main_ref, next_ref, w_ref, b_ref, o_ref, *, nt, tile):
    t = pl.program_id(0) % nt
    prev = jnp.where(t == 0, 0.0, prev_ref[...].astype(F32)[HALO - 8:])
    nxt = jnp.where(t == nt - 1, 0.0, next_ref[...].astype(F32)[:8])
    y = _conv_rolled(prev, main_ref[...].astype(F32), nxt, w_ref, SSM_CONV, tile) + b_ref[...]
    o_ref[...] = _silu(y).astype(o_ref.dtype)


def _halo_specs(tile, width, col0, m):
    hb = tile // HALO
    last = m // HALO - 1
    prev = pl.BlockSpec((HALO, width), lambda i, j: (jnp.maximum(i * hb - 1, 0), col0 + j))
    main = pl.BlockSpec((tile, width), lambda i, j: (i, col0 + j))
    nxt = pl.BlockSpec((HALO, width), lambda i, j: (jnp.minimum((i + 1) * hb, last), col0 + j))
    return prev, main, nxt


def ssm_conv(proj, w, b, seq, tile):
    m = proj.shape[0]
    width = 1024
    nt = seq // tile
    prev, main, nxt = _halo_specs(tile, width, P_XS // width, m)
    return pl.pallas_call(
        functools.partial(_ssm_conv_kernel, nt=nt, tile=tile),
        grid=(m // tile, SSM_CONV_CH // width),
        in_specs=[prev, main, nxt,
                  pl.BlockSpec((SSM_CONV, width), lambda i, j: (0, j)),
                  pl.BlockSpec((1, width), lambda i, j: (0, j))],
        out_specs=pl.BlockSpec((tile, width), lambda i, j: (i, j)),
        out_shape=jax.ShapeDtypeStruct((m, SSM_CONV_CH), BF16),
        compiler_params=_cparams(("parallel", "parallel")),
        name="ssm_conv",
    )(proj, proj, proj, w, b.reshape(1, SSM_CONV_CH))


def _short_conv_kernel(b_ref, cp_ref, cm_ref, cn_ref, hp_ref, hm_ref, hn_ref, w_ref, o_ref,
                       *, nt, tile):
    t = pl.program_id(0) % nt

    def prod(c_ref, h_ref):
        return c_ref[...].astype(F32) * h_ref[...].astype(F32)

    prev = jnp.where(t == 0, 0.0, prod(cp_ref, hp_ref)[HALO - 8:])
    nxt = jnp.where(t == nt - 1, 0.0, prod(cn_ref, hn_ref)[:8])
    y = _conv_rolled(prev, prod(cm_ref, hm_ref), nxt, w_ref, SC_CONV, tile)
    o_ref[...] = (b_ref[...].astype(F32) * y).astype(o_ref.dtype)


def short_conv(proj, w, seq, tile):
    m = proj.shape[0]
    width = 1024
    nt = seq // tile
    cp, cm, cn = _halo_specs(tile, width, P_SCC // width, m)
    hp, hm, hn = _halo_specs(tile, width, P_SCH // width, m)
    return pl.pallas_call(
        functools.partial(_short_conv_kernel, nt=nt, tile=tile),
        grid=(m // tile, SC_WIDTH // width),
        in_specs=[pl.BlockSpec((tile, width), lambda i, j: (i, P_SCB // width + j)),
                  cp, cm, cn, hp, hm, hn,
                  pl.BlockSpec((SC_CONV, width), lambda i, j: (0, j))],
        out_specs=pl.BlockSpec((tile, width), lambda i, j: (i, j)),
        out_shape=jax.ShapeDtypeStruct((m, SC_WIDTH), BF16),
        compiler_params=_cparams(("parallel", "parallel")),
        name="short_conv",
    )(proj, proj, proj, proj, proj, proj, proj, w)


def _split3(a):
    a1 = a.astype(BF16)
    r1 = a - a1.astype(F32)
    a2 = r1.astype(BF16)
    a3 = (r1 - a2.astype(F32)).astype(BF16)
    return a1, a2, a3


def _softplus(x):
    return jnp.maximum(x, 0.0) + jnp.log1p(jnp.exp(-jnp.abs(x)))


def _ssd_kernel(*refs, backward, has_h0, emit_state, nprob):
    refs = list(refs)
    x_ref, b_ref, c_ref, dt_ref, dtb_ref, alog_ref, sel_ref = refs[:7]
    del refs[:7]
    h0_ref = refs.pop(0) if has_h0 else None
    if backward:
        yf_ref, z_ref, d_ref, gn_ref = refs[:4]
        del refs[:4]
    y_ref = refs.pop(0)
    hout_ref = refs.pop(0) if emit_state else None
    st_ref = refs.pop(0)
    acc_ref = refs.pop(0) if backward else y_ref
    q = SSM_CHUNK
    gw = (SSM_HEADS // SSM_GROUPS) * SSM_HEAD_DIM

    @pl.when(pl.program_id(1) == 0)
    def _():
        for p in range(nprob):
            for g in range(SSM_GROUPS):
                if has_h0:
                    st_ref[p, g] = h0_ref[p, 0, g * gw:(g + 1) * gw, :].T
                else:
                    st_ref[p, g] = jnp.zeros((SSM_STATE, gw), F32)

    lo = SSM_HEADS if backward else 0
    neg_a = -LOG2E * jnp.exp(alog_ref[...])
    ii = lax.broadcasted_iota(jnp.int32, (q, q), 0)
    jj = lax.broadcasted_iota(jnp.int32, (q, q), 1)
    tri = (jj >= ii) if backward else (jj <= ii)
    tri_b = jnp.where(tri, 1.0, 0.0).astype(BF16)
    lane_lo = lax.broadcasted_iota(jnp.int32, (1, LANES), 1) < SSM_HEAD_DIM
    heads_per_group = SSM_HEADS // SSM_GROUPS
    pairs_per_group = heads_per_group // 2
    edge = 0 if backward else q - 1
    probs = range(nprob)

    pre = []
    for p in probs:
        dt = _softplus(dt_ref[p] + dtb_ref[...])
        a2 = dt * neg_a
        u2 = None
        for piece in _split3(a2):
            d = jnp.dot(tri_b, piece, preferred_element_type=F32)
            u2 = d if u2 is None else u2 + d
        v_t = (u2 - jnp.log2(dt)).T
        t_row = u2[edge:edge + 1, :]
        w_c = dt * jnp.exp2(t_row - u2)
        w_hi = w_c.astype(BF16)
        w_lo = (w_c - w_hi.astype(F32)).astype(BF16)
        w_exp = (jnp.dot(w_hi, sel_ref[...], preferred_element_type=F32)
                 + jnp.dot(w_lo, sel_ref[...], preferred_element_type=F32))
        pre.append((u2, v_t, t_row, w_exp))

    for g in range(SSM_GROUPS):
        grp = []
        for p in probs:
            b_g = b_ref[p, :, g * SSM_STATE:(g + 1) * SSM_STATE]
            c_g = c_ref[p, :, g * SSM_STATE:(g + 1) * SSM_STATE]
            cb = lax.dot_general(c_g, b_g, (((1,), (1,)), ((), ())), preferred_element_type=F32)
            h_t = st_ref[p, g]
            y_off = jnp.dot(c_g, h_t.astype(BF16), preferred_element_type=F32)
            grp.append((b_g, cb, h_t, y_off, [], []))
        for pr in range(pairs_per_group):
            h_a = g * heads_per_group + 2 * pr
            ca = lo + h_a
            cbk = ca + 1
            pc = (h_a // 2) * LANES
            for p in probs:
                u2, v_t, t_row, w_exp = pre[p]
                b_g, cb, h_t, y_off, xs_parts, dec_parts = grp[p]
                x_b = x_ref[p, :, pc:pc + LANES]
                up = jnp.where(lane_lo, u2[:, ca:ca + 1], u2[:, cbk:cbk + 1])
                tp = jnp.where(lane_lo, t_row[:, ca:ca + 1], t_row[:, cbk:cbk + 1])
                ys = []
                for col in (ca, cbk):
                    seg = u2[:, col:col + 1] - v_t[col:col + 1, :]
                    decay_dt = jnp.exp2(jnp.where(tri, seg, -jnp.inf))
                    ys.append(jnp.dot((cb * decay_dt).astype(BF16), x_b, preferred_element_type=F32))
                acc_ref[p, :, pc:pc + LANES] = (jnp.where(lane_lo, ys[0], ys[1])
                                                + jnp.exp2(up) * y_off[:, pr * LANES:(pr + 1) * LANES])
                xs_parts.append((x_b.astype(F32) * w_exp[:, pc:pc + LANES]).astype(BF16))
                dec_parts.append(jnp.exp2(tp))
        for p in probs:
            b_g, cb, h_t, y_off, xs_parts, dec_parts = grp[p]
            xs_g = jnp.concatenate(xs_parts, axis=1)
            dec_g = jnp.concatenate(dec_parts, axis=1)
            upd = lax.dot_general(b_g, xs_g, (((0,), (0,)), ((), ())), preferred_element_type=F32)
            st_ref[p, g] = dec_g * h_t + upd

    if backward:
        for p in probs:
            y = acc_ref[p] + yf_ref[p] + x_ref[p].astype(F32) * d_ref[...]
            y = y * _silu(z_ref[p].astype(F32))
            y_ref[p] = _rms(y, gn_ref[...]).astype(y_ref.dtype)

    if emit_state:
        @pl.when(pl.program_id(1) == pl.num_programs(1) - 1)
        def _():
            for p in range(nprob):
                for g in range(SSM_GROUPS):
                    hout_ref[p, g * gw:(g + 1) * gw, :] = st_ref[p, g].T


def _head_selector(backward):
    lane = jnp.arange(LANES)[:, None]
    head = jnp.arange(SSM_INNER)[None, :] // SSM_HEAD_DIM
    return (lane == head + (SSM_HEADS if backward else 0)).astype(BF16)


def ssd_scan(xbc, dt_raw, dt_bias, a_log, h0, batch, seq, emit_state, finish=None):
    backward = finish is not None
    m = batch * seq
    nc = seq // SSM_CHUNK
    gw = SSM_INNER // SSM_GROUPS
    hp = SSM_HEADS * SSM_HEAD_DIM

    nprob = SSD_PROBLEMS
    assert batch % nprob == 0

    def chunk(c):
        return nc - 1 - c if backward else c

    def seq_spec(width, col):
        return pl.BlockSpec((nprob, SSM_CHUNK, width), lambda b, c: (b, chunk(c), col))

    def by_seq(a):
        return a.reshape(batch, seq, a.shape[-1])

    vec = pl.BlockSpec((1, SSM_INNER), lambda b, c: (0, 0))
    lane_vec = pl.BlockSpec((1, LANES), lambda b, c: (0, 0))
    in_specs = [seq_spec(SSM_INNER, 0),
                seq_spec(SSM_BC, SSM_INNER // SSM_BC),
                seq_spec(SSM_BC, SSM_INNER // SSM_BC + 1),
                seq_spec(LANES, 0),
                lane_vec, lane_vec,
                pl.BlockSpec((LANES, SSM_INNER), lambda b, c: (0, 0))]
    xbc3 = by_seq(xbc)
    args = [xbc3, xbc3, xbc3, by_seq(dt_raw), dt_bias, a_log, _head_selector(backward)]
    if h0 is not None:
        state, slab = h0
        in_specs.append(pl.BlockSpec((nprob, 1, hp, SSM_STATE), lambda b, c: (b, slab, 0, 0)))
        args.append(state)
    scratch = [pltpu.VMEM((nprob, SSM_GROUPS, SSM_STATE, gw), F32)]
    if backward:
        y_fwd, proj, d_exp, norm_g = finish
        in_specs += [seq_spec(SSM_INNER, 0), seq_spec(SSM_INNER, P_Z // SSM_INNER), vec, vec]
        args += [by_seq(y_fwd), by_seq(proj), d_exp, norm_g.reshape(1, SSM_INNER)]
        scratch.append(pltpu.VMEM((nprob, SSM_CHUNK, SSM_INNER), F32))
    out_specs = [seq_spec(SSM_INNER, 0)]
    out_shape = [jax.ShapeDtypeStruct((batch, seq, SSM_INNER), BF16 if backward else F32)]
    if emit_state:
        out_specs.append(pl.BlockSpec((nprob, hp, SSM_STATE), lambda b, c: (b, 0, 0)))
        out_shape.append(jax.ShapeDtypeStruct((batch, hp, SSM_STATE), F32))
    out = pl.pallas_call(
        functools.partial(_ssd_kernel, backward=backward, has_h0=h0 is not None,
                          emit_state=emit_state, nprob=nprob),
        grid=(batch // nprob, nc),
        in_specs=in_specs,
        out_specs=out_specs,
        out_shape=out_shape,
        scratch_shapes=scratch,
        compiler_params=_cparams(("parallel", "arbitrary")),
        name="ssd_scan_bwd" if backward else "ssd_scan_fwd",
    )(*args)
    y = out[0].reshape(m, SSM_INNER)
    return (y, out[1]) if emit_state else (y, None)


def _pad_lanes(v):
    flat = v.reshape(1, 2 * SSM_HEADS).astype(F32)
    return jnp.pad(flat, ((0, 0), (0, LANES - 2 * SSM_HEADS)))


def _prepare_weights(w_in, w_att_out, w_ssd_out, w_sc_out, w_o, w_gate_up, w_down):
    def cols(o, n):
        return w_in[:, :, o:o + n]

    def reordered(l):
        w = w_in[l]
        parts = [(_O_Q, Q_W), (_O_Z, SSM_INNER), (_O_SCB, SC_WIDTH), (_O_SCC, SC_WIDTH),
                 (_O_SCH, SC_WIDTH), (_O_G, N_BRANCH * D_MODEL), (_O_XS, SSM_INNER),
                 (_O_BM, SSM_BC), (_O_CM, SSM_BC), (_O_K, KV_W), (_O_V, KV_W)]
        return jnp.concatenate([w[:, o:o + n] for o, n in parts], axis=1).astype(BF16)[None]

    w_main = [reordered(l) for l in range(DEPTH)]
    w_dt = jnp.pad(cols(_O_DT, 2 * SSM_HEADS), ((0, 0), (0, 0), (0, LANES - 2 * SSM_HEADS))).astype(BF16)
    return dict(w_main=w_main, w_dt=w_dt,
                w_att=w_att_out.astype(BF16), w_ssd=w_ssd_out.astype(BF16),
                w_sc=w_sc_out.astype(BF16), w_o=w_o.astype(BF16),
                w_gate_up=w_gate_up.astype(BF16), w_down=w_down.astype(BF16))


def _tile_plan(seq):
    return dict(
        in_proj=(1024, 1024),
        swiglu=(1024, 512),
        down=(256, D_FF),
        w_o=(512, D_MODEL),
        merge=(1024, 256),
        conv=min(seq, 512))


def _trunk_layer(x, mod, mod_row, l, lw, p, batch, seq, latent):
    plan = _tile_plan(seq)
    outs = in_proj(x, p['g_pre_mix'], mod, mod_row, lw['w_main'][l], lw['w_dt'], l, *plan['in_proj'],
                   emit_kv=latent is None)
    proj, dt_raw = outs[0], outs[1]
    if latent is None:
        y_att = context_attention(proj, p['sink'], batch, seq)
        h0_f = h0_b = None
    else:
        ctx_k, ctx_v, state, rope = latent
        y_att = latent_attention(proj, ctx_k.reshape(batch, -1, KV_W).astype(BF16),
                                 ctx_v.reshape(batch, -1, KV_W).astype(BF16), p['sink'],
                                 rope[0], rope[1], batch, seq)
        h0_f, h0_b = (state, 2 * l), (state, 2 * l + 1)
    xbc = ssm_conv(proj, p['ssm_conv_w'], p['ssm_conv_b'], seq, plan['conv'])
    dt_bias = _pad_lanes(p['ssm_dt_bias'])
    a_log = _pad_lanes(p['ssm_a_log'])
    d_exp = jnp.repeat(p['ssm_d'].astype(F32), SSM_HEAD_DIM).reshape(1, SSM_INNER)
    emit_state = latent is None
    y_f, h_f = ssd_scan(xbc, dt_raw, dt_bias, a_log, h0_f, batch, seq, emit_state)
    y_ssd, h_b = ssd_scan(xbc, dt_raw, dt_bias, a_log, h0_b, batch, seq, emit_state,
                          finish=(y_f, proj, d_exp, p['ssm_norm_g']))
    y_sc = short_conv(proj, p['sc_conv_w'], seq, plan['conv'])
    merged = merge_branches(y_att, y_ssd, y_sc, lw['w_att'], lw['w_ssd'], lw['w_sc'], l, proj,
                            *plan['merge'])
    x = matmul_norm_residual(merged, lw['w_o'], l, x, p['g_post_mix'], mod, mod_row, 2, *plan['w_o'])
    act = norm_swiglu(x, p['g_pre_ffn'], mod, mod_row, lw['w_gate_up'], l, *plan['swiglu'])
    x = matmul_norm_residual(act, lw['w_down'], l, x, p['g_post_ffn'], mod, mod_row, 5, *plan['down'])
    new_ctx = None
    if latent is None:
        kv = outs[2]
        k = kv[:, :KV_W].reshape(batch, seq, N_KV_HEADS, HEAD_DIM)
        v = kv[:, KV_W:].reshape(batch, seq, N_KV_HEADS, HEAD_DIM)
        new_ctx = (k, v, h_f, h_b)
    return x, new_ctx


def kernel(x_prompt, x_sample, cache_k, cache_v, state_ssm, c, c_ctx, w_mod, b_mod, g_pre_mix, w_in, sink, ssm_conv_w, ssm_conv_b, ssm_dt_bias, ssm_a_log, ssm_d, ssm_norm_g, sc_conv_w, w_att_out, w_ssd_out, w_sc_out, w_o, g_post_mix, g_pre_ffn, w_gate_up, w_down, g_post_ffn):
    batch, seq, _ = x_prompt.shape
    dec_batch, dec_seq, _ = x_sample.shape
    cond = jnp.concatenate([c_ctx[None, :], c, jnp.zeros((8 - 1 - dec_batch, D_MODEL), F32)], axis=0)
    mod = modulation(cond, w_mod, b_mod).reshape(DEPTH, 8, N_MOD, D_MODEL)
    rope = rope_tables(dec_seq)
    y_prompt = x_prompt.reshape(batch * seq, D_MODEL)
    y_sample = x_sample.reshape(dec_batch * dec_seq, D_MODEL)
    lw = _prepare_weights(w_in, w_att_out, w_ssd_out, w_sc_out, w_o, w_gate_up, w_down)
    state_in = state_ssm.reshape(dec_batch, 2 * DEPTH, SSM_HEADS * SSM_HEAD_DIM, SSM_STATE)
    ks_out, vs_out, ss_out = [], [], []
    for l in range(DEPTH):
        p = {'g_pre_mix': g_pre_mix[l], 'sink': sink[l], 'ssm_conv_w': ssm_conv_w[l],
             'ssm_conv_b': ssm_conv_b[l], 'ssm_dt_bias': ssm_dt_bias[l], 'ssm_a_log': ssm_a_log[l],
             'ssm_d': ssm_d[l], 'ssm_norm_g': ssm_norm_g[l], 'sc_conv_w': sc_conv_w[l],
             'g_post_mix': g_post_mix[l], 'g_pre_ffn': g_pre_ffn[l], 'g_post_ffn': g_post_ffn[l]}
        y_prompt, (k_l, v_l, h_f, h_b) = _trunk_layer(
            y_prompt, mod[l], lambda r: 0, l, lw, p, batch, seq, None)
        ks_out.append(k_l)
        vs_out.append(v_l)
        ss_out += [h_f, h_b]
        y_sample, _ = _trunk_layer(
            y_sample, mod[l], lambda r: 1 + r // dec_seq, l, lw, p, dec_batch, dec_seq,
            (cache_k[:, l], cache_v[:, l], state_in, rope))
    new_state = jnp.stack(ss_out, axis=1).reshape(
        batch, DEPTH, 2, SSM_HEADS, SSM_HEAD_DIM, SSM_STATE)
    return (y_prompt.reshape(batch, seq, D_MODEL), y_sample.reshape(dec_batch, dec_seq, D_MODEL),
            jnp.stack(ks_out, axis=1), jnp.stack(vs_out, axis=1), new_state)
```
